```python
import jax, jax.numpy as jnp
from jax import lax
import numpy as np

D_MODEL = 1024
BATCH = 4
SEQ = 4096
DEPTH = 2

BLOCK_Q = 128
EPS = 1e-6
NEG_INF = -1e30
ROPE_THETA = 10000.0

MLA_HEADS = 8
MLA_Q_RANK = 256
MLA_KV_RANK = 128
MLA_NOPE = 64
MLA_ROPE = 32
MLA_QK = MLA_NOPE + MLA_ROPE
MLA_V = 64
SB_HEADS = 8
SB_DIM = 64
SB_WIDTH = SB_HEADS * SB_DIM
FOX_HEADS = 16
FOX_DIM = 64
FOX_WIDTH = FOX_HEADS * FOX_DIM
D_FF = ((8 * D_MODEL // 3 + 255) // 256) * 256

EVEN_IN = MLA_Q_RANK + MLA_KV_RANK + MLA_ROPE + 3 * SB_WIDTH
EVEN_OUT = MLA_HEADS * MLA_V + SB_WIDTH
ODD_IN = 3 * FOX_WIDTH + FOX_HEADS
ODD_OUT = FOX_WIDTH

kernel_name = 'hybrid_mla_stickbreaking_fox_block'


def rms_norm(x, g):
    xf = x.astype(jnp.float32)
    y = xf * lax.rsqrt(jnp.mean(xf * xf, axis=-1, keepdims=True) + EPS)
    return (y * g.astype(jnp.float32)).astype(x.dtype)


def split_last(x, sizes):
    return jnp.split(x, [int(v) for v in np.cumsum(sizes)[:-1]], axis=-1)


def heads_first(x):
    return x.transpose(0, 2, 1, 3)


def merge_heads(x):
    b, h, s, d = x.shape
    return x.transpose(0, 2, 1, 3).reshape(b, s, h * d)


def rope_tail(x, positions):
    nope, r = x[..., :-MLA_ROPE], x[..., -MLA_ROPE:]
    half = MLA_ROPE // 2
    inv_freq = ROPE_THETA ** (-jnp.arange(half, dtype=jnp.float32) / half)
    ang = positions.astype(jnp.float32)[..., None] * inv_freq
    cos, sin = jnp.cos(ang)[:, :, None, :], jnp.sin(ang)[:, :, None, :]
    r1, r2 = r[..., :half].astype(jnp.float32), r[..., half:].astype(jnp.float32)
    rot = jnp.concatenate([r1 * cos - r2 * sin, r2 * cos + r1 * sin], axis=-1)
    return jnp.concatenate([nope, rot.astype(x.dtype)], axis=-1)


def sweep_query_blocks(block_fn, q):
    b, h, s, _ = q.shape
    out = lax.map(block_fn, jnp.arange(s // BLOCK_Q))
    nb, _, _, bq, dv = out.shape
    return out.transpose(1, 2, 0, 3, 4).reshape(b, h, nb * bq, dv)


def causal_softmax_attention(q, k, v, scale, decay=None):
    s = k.shape[2]
    k_idx = jnp.arange(s)

    def block(i):
        start = i * BLOCK_Q
        qb = lax.dynamic_slice_in_dim(q, start, BLOCK_Q, axis=2)
        logits = jnp.einsum('bhqd,bhkd->bhqk', qb, k).astype(jnp.float32) * scale
        if decay is not None:
            cq = lax.dynamic_slice_in_dim(decay, start, BLOCK_Q, axis=2)
            logits = logits + cq[..., :, None] - decay[..., None, :]
        q_idx = start + jnp.arange(BLOCK_Q)
        mask = k_idx[None, :] <= q_idx[:, None]
        p = jax.nn.softmax(jnp.where(mask, logits, NEG_INF), axis=-1)
        return jnp.einsum('bhqk,bhkd->bhqd', p.astype(v.dtype), v)

    return sweep_query_blocks(block, q)


def stick_breaking_attention(q, k, v):
    s = k.shape[2]
    scale = q.shape[-1] ** -0.5
    k_idx = jnp.arange(s)

    def block(i):
        start = i * BLOCK_Q
        qb = lax.dynamic_slice_in_dim(q, start, BLOCK_Q, axis=2)
        z = jnp.einsum('bhqd,bhkd->bhqk', qb, k).astype(jnp.float32) * scale
        q_idx = start + jnp.arange(BLOCK_Q)
        mask = k_idx[None, :] < q_idx[:, None]
        log_beta = jax.nn.log_sigmoid(z)
        log_1m_beta = jnp.where(mask, jax.nn.log_sigmoid(-z), 0.0)
        suffix = lax.cumsum(log_1m_beta, axis=3, reverse=True) - log_1m_beta
        a = jnp.where(mask, jnp.exp(log_beta + suffix), 0.0)
        return jnp.einsum('bhqk,bhkd->bhqd', a.astype(v.dtype), v)

    return sweep_query_blocks(block, q)


def even_mixer(h, positions, w_in, q_a_norm, w_q_up, kv_a_norm, w_kv_up, q_norm, k_norm, w_o):
    b, s, _ = h.shape
    proj = h @ w_in
    cq, ckv, k_rope, sb_qkv = split_last(proj, [MLA_Q_RANK, MLA_KV_RANK, MLA_ROPE, 3 * SB_WIDTH])
    q = (rms_norm(cq, q_a_norm) @ w_q_up).reshape(b, s, MLA_HEADS, MLA_QK)
    kv = (rms_norm(ckv, kv_a_norm) @ w_kv_up).reshape(b, s, MLA_HEADS, MLA_NOPE + MLA_V)
    k_nope, v = kv[..., :MLA_NOPE], kv[..., MLA_NOPE:]
    k_rope = jnp.broadcast_to(k_rope[:, :, None, :], (b, s, MLA_HEADS, MLA_ROPE))
    k = jnp.concatenate([k_nope, k_rope], axis=-1)
    q = rope_tail(rms_norm(q, q_norm), positions)
    k = rope_tail(rms_norm(k, k_norm), positions)
    mla = causal_softmax_attention(heads_first(q), heads_first(k), heads_first(v), MLA_QK ** -0.5)
    sq, sk, sv = [heads_first(t.reshape(b, s, SB_HEADS, SB_DIM)) for t in jnp.split(sb_qkv, 3, axis=-1)]
    sb = stick_breaking_attention(sq, sk, sv)
    o = jnp.concatenate([merge_heads(mla), merge_heads(sb)], axis=-1)
    return o @ w_o


def odd_mixer(h, w_in, f_bias, q_norm, k_norm, w_o):
    b, s, _ = h.shape
    proj = h @ w_in
    q, k, v, f_logit = split_last(proj, [FOX_WIDTH, FOX_WIDTH, FOX_WIDTH, FOX_HEADS])
    q = rms_norm(q.reshape(b, s, FOX_HEADS, FOX_DIM), q_norm)
    k = rms_norm(k.reshape(b, s, FOX_HEADS, FOX_DIM), k_norm)
    v = v.reshape(b, s, FOX_HEADS, FOX_DIM)
    log_f = jax.nn.log_sigmoid(f_logit.astype(jnp.float32) + f_bias.astype(jnp.float32))
    decay = lax.cumsum(log_f, axis=1).transpose(0, 2, 1)
    o = causal_softmax_attention(heads_first(q), heads_first(k), heads_first(v), FOX_DIM ** -0.5, decay)
    return merge_heads(o) @ w_o


def swiglu(h, w_gate, w_up, w_down):
    return (jax.nn.silu(h @ w_gate) * (h @ w_up)) @ w_down


def setup_inputs(seed: int = 0) -> dict:
    key = jax.random.key(seed)
    ks = iter(jax.random.split(key, 32))

    def dense(shape):
        return jax.random.normal(next(ks), shape, jnp.float32) * shape[0] ** -0.5

    def gain(n):
        return 1.0 + 0.1 * jax.random.normal(next(ks), (n,), jnp.float32)

    x = jax.random.normal(next(ks), (BATCH, SEQ, D_MODEL), jnp.float32)
    offsets = jax.random.randint(next(ks), (BATCH, 1), 0, SEQ, dtype=jnp.int32)
    positions = (offsets + jnp.arange(SEQ, dtype=jnp.int32)[None, :]).astype(jnp.int32)
    return {
        'x': x,
        'positions': positions,
        'l0_attn_norm': gain(D_MODEL),
        'l0_w_in': dense((D_MODEL, EVEN_IN)),
        'l0_mla_q_a_norm': gain(MLA_Q_RANK),
        'l0_mla_w_q_up': dense((MLA_Q_RANK, MLA_HEADS * MLA_QK)),
        'l0_mla_kv_a_norm': gain(MLA_KV_RANK),
        'l0_mla_w_kv_up': dense((MLA_KV_RANK, MLA_HEADS * (MLA_NOPE + MLA_V))),
        'l0_mla_q_norm': gain(MLA_QK),
        'l0_mla_k_norm': gain(MLA_QK),
        'l0_w_o': dense((EVEN_OUT, D_MODEL)),
        'l0_ffn_norm': gain(D_MODEL),
        'l0_w_gate': dense((D_MODEL, D_FF)),
        'l0_w_up': dense((D_MODEL, D_FF)),
        'l0_w_down': dense((D_FF, D_MODEL)),
        'l1_attn_norm': gain(D_MODEL),
        'l1_w_in': dense((D_MODEL, ODD_IN)),
        'l1_fox_f_bias': 3.0 + 0.5 * jax.random.normal(next(ks), (FOX_HEADS,), jnp.float32),
        'l1_fox_q_norm': gain(FOX_DIM),
        'l1_fox_k_norm': gain(FOX_DIM),
        'l1_w_o': dense((ODD_OUT, D_MODEL)),
        'l1_ffn_norm': gain(D_MODEL),
        'l1_w_gate': dense((D_MODEL, D_FF)),
        'l1_w_up': dense((D_MODEL, D_FF)),
        'l1_w_down': dense((D_FF, D_MODEL)),
    }


def reference(x, positions,
              l0_attn_norm, l0_w_in, l0_mla_q_a_norm, l0_mla_w_q_up, l0_mla_kv_a_norm,
              l0_mla_w_kv_up, l0_mla_q_norm, l0_mla_k_norm, l0_w_o,
              l0_ffn_norm, l0_w_gate, l0_w_up, l0_w_down,
              l1_attn_norm, l1_w_in, l1_fox_f_bias, l1_fox_q_norm, l1_fox_k_norm, l1_w_o,
              l1_ffn_norm, l1_w_gate, l1_w_up, l1_w_down):
    mixer_params = (
        (l0_attn_norm, l0_w_in, l0_mla_q_a_norm, l0_mla_w_q_up, l0_mla_kv_a_norm,
         l0_mla_w_kv_up, l0_mla_q_norm, l0_mla_k_norm, l0_w_o),
        (l1_attn_norm, l1_w_in, l1_fox_f_bias, l1_fox_q_norm, l1_fox_k_norm, l1_w_o),
    )
    ffn_params = (
        (l0_ffn_norm, l0_w_gate, l0_w_up, l0_w_down),
        (l1_ffn_norm, l1_w_gate, l1_w_up, l1_w_down),
    )
    for layer in range(DEPTH):
        norm_g, *mp = mixer_params[layer]
        h = rms_norm(x, norm_g)
        if layer % 2 == 0:
            x = x + even_mixer(h, positions, *mp)
        else:
            x = x + odd_mixer(h, *mp)
        f_norm, w_gate, w_up, w_down = ffn_params[layer]
        x = x + swiglu(rms_norm(x, f_norm), w_gate, w_up, w_down)
    return x
```

```python
import functools

import jax
import jax.numpy as jnp
from jax import lax
from jax.experimental import pallas as pl
from jax.experimental.pallas import tpu as pltpu

F32 = jnp.float32
BF16 = jnp.bfloat16

D_MODEL = 1024
EPS = 1e-6
NEG_INF = -1e30
ROPE_THETA = 10000.0

MLA_HEADS = 8
MLA_Q_RANK = 256
MLA_KV_RANK = 128
MLA_NOPE = 64
MLA_ROPE = 32
MLA_QK = MLA_NOPE + MLA_ROPE
MLA_V = 64
SB_HEADS = 8
SB_DIM = 64
SB_WIDTH = SB_HEADS * SB_DIM
FOX_HEADS = 16
FOX_DIM = 64
FOX_WIDTH = FOX_HEADS * FOX_DIM

LANES = 128
V7X_VMEM_BYTES = 64 * 1024 * 1024
VMEM_LIMIT = V7X_VMEM_BYTES * 7 // 8

ROW_TILE = 512
ATT_TILE = 512
FFN_TILE = 256

AUG0 = FOX_DIM


def _compiler_params(n_axes):
    return pltpu.CompilerParams(
        dimension_semantics=("arbitrary",) * n_axes,
        vmem_limit_bytes=VMEM_LIMIT,
    )


def _full(shape):
    return pl.BlockSpec(shape, lambda *_: (0,) * len(shape), pipeline_mode=pl.Buffered(1))


def _lane_iota(rows):
    return lax.broadcasted_iota(jnp.int32, (rows, LANES), 1)


def _rms(x, gain, width):
    ss = jnp.sum(x * x, axis=-1, keepdims=True)
    return x * lax.rsqrt(ss * (1.0 / width) + EPS) * gain


def _log_sigmoid(z):
    return jnp.minimum(z, 0.0) - jnp.log(1.0 + jnp.exp(-jnp.abs(z)))


def _split3(x):
    a = x.astype(BF16)
    r = x - a.astype(F32)
    b = r.astype(BF16)
    c = (r - b.astype(F32)).astype(BF16)
    return a, b, c


def _dot(a, b):
    return jnp.dot(a, b, preferred_element_type=F32)


def _dot_nt(a, b):
    return lax.dot_general(a, b, (((1,), (1,)), ((), ())), preferred_element_type=F32)


def _in0_kernel(x_ref, pos_ref, g_ref, w_ref, qan_ref, wq_ref, kvan_ref, wkv_ref,
                qn_ref, kn_ref, invf_ref,
                qm_ref, km_ref, vtm_ref, sq_ref, sk_ref, vts_ref):
    x = x_ref[0]
    h = _rms(x, g_ref[...], D_MODEL).astype(BF16)
    proj = _dot(h, w_ref[...])
    cq = proj[:, 0:MLA_Q_RANK]
    ckv = proj[:, MLA_Q_RANK:MLA_Q_RANK + MLA_KV_RANK]
    o = MLA_Q_RANK + MLA_KV_RANK
    k_rope_group = proj[:, o:o + LANES]
    o += LANES
    sq = proj[:, o:o + SB_WIDTH]
    sk = proj[:, o + SB_WIDTH:o + 2 * SB_WIDTH]
    sv = proj[:, o + 2 * SB_WIDTH:o + 3 * SB_WIDTH]

    q_all = _dot(_rms(cq, qan_ref[...], MLA_Q_RANK).astype(BF16), wq_ref[...])
    kv_all = _dot(_rms(ckv, kvan_ref[...], MLA_KV_RANK).astype(BF16), wkv_ref[...])

    rows = x.shape[0]
    ang = pos_ref[0].astype(F32) * invf_ref[...]
    cos = jnp.cos(ang)
    sin = jnp.sin(ang)
    lane = _lane_iota(rows)
    half = MLA_ROPE // 2
    first = (lane >= MLA_NOPE) & (lane < MLA_NOPE + half)
    second = (lane >= MLA_NOPE + half) & (lane < MLA_QK)
    sin_first = jnp.where(first, -sin, 0.0)
    sin_second = jnp.where(second, sin, 0.0)

    def norm_rope(t, gain):
        y = _rms(t, gain, MLA_QK)
        return (y * cos + pltpu.roll(y, LANES - half, 1) * sin_first
                + pltpu.roll(y, half, 1) * sin_second)

    qn = qn_ref[...]
    kn = kn_ref[...]
    for hd in range(MLA_HEADS):
        sl = slice(hd * LANES, (hd + 1) * LANES)
        qm_ref[0, hd] = norm_rope(q_all[:, sl], qn).astype(BF16)
        km_ref[0, hd] = norm_rope(kv_all[:, sl] + k_rope_group, kn).astype(BF16)
    v = kv_all[:, MLA_HEADS * LANES:]
    vtm_ref[0, 0] = v.T.astype(BF16)
    sq_ref[0] = (sq * (SB_DIM ** -0.5)).astype(BF16)
    sk_ref[0] = sk.astype(BF16)
    vts_ref[0, 0] = sv.T.astype(BF16)


def _in0_call(x, pos3, g, w, qan, wq, kvan, wkv, qn, kn, invf):
    b, s, _ = x.shape
    tm = ROW_TILE
    nt = s // tm
    n_proj = w.shape[1]
    head_spec = pl.BlockSpec((1, MLA_HEADS, tm, LANES), lambda bi, i: (bi, 0, i, 0))
    vt_spec = pl.BlockSpec((1, 1, SB_WIDTH, tm), lambda bi, i: (bi, i, 0, 0))
    row_spec = pl.BlockSpec((1, tm, SB_WIDTH), lambda bi, i: (bi, i, 0))
    return pl.pallas_call(
        _in0_kernel,
        grid=(b, nt),
        in_specs=[
            pl.BlockSpec((1, tm, D_MODEL), lambda bi, i: (bi, i, 0)),
            pl.BlockSpec((1, tm, 1), lambda bi, i: (bi, i, 0)),
            _full((1, D_MODEL)),
            _full((D_MODEL, n_proj)),
            _full((1, MLA_Q_RANK)),
            _full(wq.shape),
            _full((1, MLA_KV_RANK)),
            _full(wkv.shape),
            _full((1, LANES)),
            _full((1, LANES)),
            _full((1, LANES)),
        ],
        out_specs=[head_spec, head_spec, vt_spec, row_spec, row_spec, vt_spec],
        out_shape=[
            jax.ShapeDtypeStruct((b, MLA_HEADS, s, LANES), BF16),
            jax.ShapeDtypeStruct((b, MLA_HEADS, s, LANES), BF16),
            jax.ShapeDtypeStruct((b, nt, MLA_HEADS * MLA_V, tm), BF16),
            jax.ShapeDtypeStruct((b, s, SB_WIDTH), BF16),
            jax.ShapeDtypeStruct((b, s, SB_WIDTH), BF16),
            jax.ShapeDtypeStruct((b, nt, SB_WIDTH, tm), BF16),
        ],
        compiler_params=_compiler_params(2),
        name="l0_in",
    )(x, pos3, g, w, qan, wq, kvan, wkv, qn, kn, invf)


def _softmax_attn_kernel(q_ref, k_ref, vt_ref, o_ref, *, head_dim):
    t = ATT_TILE
    qi = pl.program_id(2)
    row = lax.broadcasted_iota(jnp.int32, (t, t), 0)
    col = lax.broadcasted_iota(jnp.int32, (t, t), 1)
    visible = row <= col
    outs = []
    for hh in range(2):
        q = q_ref[0, hh]

        def step(j, carry, masked):
            m, l, acc = carry
            k = k_ref[0, hh, pl.ds(pl.multiple_of(j * t, t), t), :]
            s = _dot_nt(k, q)
            if masked:
                s = jnp.where(visible, s, NEG_INF)
            m_new = jnp.maximum(m, jnp.max(s, axis=0, keepdims=True))
            p = jnp.exp(s - m_new)
            alpha = jnp.exp(m - m_new)
            l = alpha * l + jnp.sum(p, axis=0, keepdims=True)
            vt = vt_ref[0, j, hh * head_dim:(hh + 1) * head_dim, :]
            acc = alpha * acc + _dot(vt, p.astype(BF16))
            return m_new, l, acc

        init = (jnp.full((1, t), NEG_INF, F32), jnp.zeros((1, t), F32),
                jnp.zeros((head_dim, t), F32))
        carry = lax.fori_loop(0, qi, functools.partial(step, masked=False), init)
        _, l, acc = step(qi, carry, True)
        outs.append(acc / l)
    o_ref[0] = jnp.concatenate(outs, axis=0).T.astype(o_ref.dtype)


def _softmax_attn_call(q, k, vt, name):
    b, h, s, _ = q.shape
    t = ATT_TILE
    nt = s // t
    head_dim = vt.shape[2] // h
    return pl.pallas_call(
        functools.partial(_softmax_attn_kernel, head_dim=head_dim),
        grid=(b, h // 2, nt),
        in_specs=[
            pl.BlockSpec((1, 2, t, LANES), lambda bi, p, i: (bi, p, i, 0)),
            pl.BlockSpec((1, 2, s, LANES), lambda bi, p, i: (bi, p, 0, 0)),
            pl.BlockSpec((1, nt, 2 * head_dim, t), lambda bi, p, i: (bi, 0, p, 0)),
        ],
        out_specs=pl.BlockSpec((1, t, 2 * head_dim), lambda bi, p, i: (bi, i, p)),
        out_shape=jax.ShapeDtypeStruct((b, s, h * head_dim), BF16),
        compiler_params=_compiler_params(3),
        name=name,
    )(q, k, vt)


def _sb_attn_kernel(q_ref, k_ref, vt_ref, o_ref):
    t = ATT_TILE
    qi = pl.program_id(2)
    row = lax.broadcasted_iota(jnp.int32, (t, t), 0)
    col = lax.broadcasted_iota(jnp.int32, (t, t), 1)
    visible = row < col
    later = (col > row).astype(BF16)
    later2 = jnp.concatenate([later, later], axis=1)
    lane = _lane_iota(t)
    q_pair = q_ref[0]
    outs = []
    for hh in range(2):
        in_head = (lane >= hh * SB_DIM) & (lane < (hh + 1) * SB_DIM)
        q = jnp.where(in_head, q_pair, jnp.zeros_like(q_pair))

        def step(j, carry, masked):
            run, acc = carry
            k = k_ref[0, pl.ds(pl.multiple_of(j * t, t), t), :]
            z = _dot_nt(k, q)
            log_beta = _log_sigmoid(z)
            log_1m = log_beta - z
            if masked:
                log_1m = jnp.where(visible, log_1m, 0.0)
            hi = log_1m.astype(BF16)
            lo = (log_1m - hi.astype(F32)).astype(BF16)
            suffix = _dot(later2, jnp.concatenate([hi, lo], axis=0))
            a = jnp.exp(log_beta + suffix + run)
            if masked:
                a = jnp.where(visible, a, 0.0)
            vt = vt_ref[0, j, hh * SB_DIM:(hh + 1) * SB_DIM, :]
            acc = acc + _dot(vt, a.astype(BF16))
            run = run + suffix[0:1, :] + log_1m[0:1, :]
            return run, acc

        carry = step(qi, (jnp.zeros((1, t), F32), jnp.zeros((SB_DIM, t), F32)), True)
        _, acc = lax.fori_loop(
            0, qi, lambda i, c: step(qi - 1 - i, c, False), carry)
        outs.append(acc)
    o_ref[0] = jnp.concatenate(outs, axis=0).T.astype(o_ref.dtype)


def _sb_attn_call(q, k, vt):
    b, s, w = q.shape
    t = ATT_TILE
    nt = s // t
    return pl.pallas_call(
        _sb_attn_kernel,
        grid=(b, w // LANES, nt),
        in_specs=[
            pl.BlockSpec((1, t, LANES), lambda bi, p, i: (bi, i, p)),
            pl.BlockSpec((1, s, LANES), lambda bi, p, i: (bi, 0, p)),
            pl.BlockSpec((1, nt, LANES, t), lambda bi, p, i: (bi, 0, p, 0)),
        ],
        out_specs=pl.BlockSpec((1, t, LANES), lambda bi, p, i: (bi, i, p)),
        out_shape=jax.ShapeDtypeStruct((b, s, w), BF16),
        compiler_params=_compiler_params(3),
        name="l0_sb_attn",
    )(q, k, vt)


def _out_ffn_kernel(*refs, n_mix):
    x_ref = refs[0]
    o_refs = refs[1:1 + n_mix]
    wo_refs = refs[1 + n_mix:1 + 2 * n_mix]
    g_ref, wg_ref, wu_ref, wd_ref, y_ref = refs[1 + 2 * n_mix:]
    y = x_ref[...]
    for o_ref, wo_ref in zip(o_refs, wo_refs):
        y = y + _dot(o_ref[...], wo_ref[...])
    h = _rms(y, g_ref[...], D_MODEL).astype(BF16)
    gate = _dot(h, wg_ref[...])
    up = _dot(h, wu_ref[...])
    act = gate * (1.0 / (1.0 + jnp.exp(-gate))) * up
    y_ref[...] = y + _dot(act.astype(BF16), wd_ref[...])


def _out_ffn_call(x2, mixes, wos, g, wg, wu, wd, name):
    n, d = x2.shape
    tm = FFN_TILE
    n_mix = len(mixes)
    row = lambda width: pl.BlockSpec((tm, width), lambda i: (i, 0))
    return pl.pallas_call(
        functools.partial(_out_ffn_kernel, n_mix=n_mix),
        grid=(n // tm,),
        in_specs=([row(d)] + [row(m.shape[1]) for m in mixes]
                  + [_full(w.shape) for w in wos]
                  + [_full((1, d)), _full(wg.shape), _full(wu.shape), _full(wd.shape)]),
        out_specs=row(d),
        out_shape=jax.ShapeDtypeStruct((n, d), F32),
        compiler_params=_compiler_params(1),
        name=name,
    )(x2, *mixes, *wos, g, wg, wu, wd)


def _in1_kernel(x_ref, g_ref, w_ref, fb_ref, qn_ref, kn_ref,
                qa_ref, ka_ref, vt_ref, carry_ref):
    i = pl.program_id(1)

    @pl.when(i == 0)
    def _():
        carry_ref[...] = jnp.zeros_like(carry_ref)

    x = x_ref[0]
    rows = x.shape[0]
    h = _rms(x, g_ref[...], D_MODEL).astype(BF16)
    proj = _dot(h, w_ref[...])
    q = proj[:, 0:FOX_WIDTH]
    k = proj[:, FOX_WIDTH:2 * FOX_WIDTH]
    v = proj[:, 2 * FOX_WIDTH:3 * FOX_WIDTH]
    f_logit = proj[:, 3 * FOX_WIDTH:]

    log_f = _log_sigmoid(f_logit + fb_ref[...])
    r_i = lax.broadcasted_iota(jnp.int32, (rows, rows), 0)
    c_i = lax.broadcasted_iota(jnp.int32, (rows, rows), 1)
    lower = (c_i <= r_i).astype(BF16)
    f1, f2, f3 = _split3(log_f)
    decay = carry_ref[...] + _dot(lower, f1) + _dot(lower, f2) + _dot(lower, f3)
    carry_ref[...] = decay[rows - 1:rows, :]
    d1, d2, d3 = (d.astype(F32) for d in _split3(decay))

    lane = _lane_iota(rows)
    low_half = lane < FOX_DIM
    qn = qn_ref[...]
    kn = kn_ref[...]

    def head_norm(pair, gain, in_head):
        ss = jnp.sum(jnp.where(in_head, pair * pair, 0.0), axis=-1, keepdims=True)
        return pair * lax.rsqrt(ss * (1.0 / FOX_DIM) + EPS) * gain

    def lane_is(n):
        return lane == AUG0 + n

    for hd in range(FOX_HEADS):
        sl = slice((hd // 2) * LANES, (hd // 2 + 1) * LANES)
        in_head = low_half if hd % 2 == 0 else jnp.logical_not(low_half)
        qh = head_norm(q[:, sl], qn, in_head)
        kh = head_norm(k[:, sl], kn, in_head)
        if hd % 2 == 1:
            qh = pltpu.roll(qh, FOX_DIM, 1)
            kh = pltpu.roll(kh, FOX_DIM, 1)
        c1 = d1[:, hd:hd + 1]
        c2 = d2[:, hd:hd + 1]
        c3 = d3[:, hd:hd + 1]
        one = jnp.ones_like(c1)
        zero = jnp.zeros_like(c1)
        aug_q = jnp.where(lane_is(0), c1, jnp.where(lane_is(1), c2, jnp.where(
            lane_is(2), c3, jnp.where((lane >= AUG0 + 3) & (lane < AUG0 + 6), one, zero))))
        aug_k = jnp.where(lane_is(3), -c1, jnp.where(lane_is(4), -c2, jnp.where(
            lane_is(5), -c3, jnp.where((lane >= AUG0) & (lane < AUG0 + 3), one, zero))))
        qa_ref[0, hd] = jnp.where(low_half, qh, aug_q).astype(BF16)
        ka_ref[0, hd] = jnp.where(low_half, kh, aug_k).astype(BF16)
    vt_ref[0, 0] = v.T.astype(BF16)


def _in1_call(x, g, w, fb, qn, kn):
    b, s, _ = x.shape
    tm = ROW_TILE
    nt = s // tm
    head_spec = pl.BlockSpec((1, FOX_HEADS, tm, LANES), lambda bi, i: (bi, 0, i, 0))
    return pl.pallas_call(
        _in1_kernel,
        grid=(b, nt),
        in_specs=[
            pl.BlockSpec((1, tm, D_MODEL), lambda bi, i: (bi, i, 0)),
            _full((1, D_MODEL)),
            _full(w.shape),
            _full((1, LANES)),
            _full((1, LANES)),
            _full((1, LANES)),
        ],
        out_specs=[head_spec, head_spec,
                   pl.BlockSpec((1, 1, FOX_WIDTH, tm), lambda bi, i: (bi, i, 0, 0))],
        out_shape=[
            jax.ShapeDtypeStruct((b, FOX_HEADS, s, LANES), BF16),
            jax.ShapeDtypeStruct((b, FOX_HEADS, s, LANES), BF16),
            jax.ShapeDtypeStruct((b, nt, FOX_WIDTH, tm), BF16),
        ],
        scratch_shapes=[pltpu.VMEM((1, LANES), F32)],
        compiler_params=_compiler_params(2),
        name="l1_in",
    )(x, g, w, fb, qn, kn)


def _pad_lanes(v, width=LANES):
    return jnp.pad(v, (0, width - v.shape[0])).reshape(1, width)


def _prep_layer0(w_in, w_q_up, w_kv_up, q_norm, k_norm):
    o = MLA_Q_RANK + MLA_KV_RANK
    w_rope = w_in[:, o:o + MLA_ROPE]
    rope_group = jnp.pad(w_rope, ((0, 0), (MLA_NOPE, LANES - MLA_QK)))
    w = jnp.concatenate([w_in[:, :o], rope_group, w_in[:, o + MLA_ROPE:]], axis=1)
    wq = jnp.pad(w_q_up.reshape(MLA_Q_RANK, MLA_HEADS, MLA_QK),
                 ((0, 0), (0, 0), (0, LANES - MLA_QK))).reshape(MLA_Q_RANK, MLA_HEADS * LANES)
    kv = w_kv_up.reshape(MLA_KV_RANK, MLA_HEADS, MLA_NOPE + MLA_V)
    wk = jnp.pad(kv[:, :, :MLA_NOPE], ((0, 0), (0, 0), (0, LANES - MLA_NOPE)))
    wkv = jnp.concatenate([wk.reshape(MLA_KV_RANK, MLA_HEADS * LANES),
                           kv[:, :, MLA_NOPE:].reshape(MLA_KV_RANK, MLA_HEADS * MLA_V)], axis=1)
    qn = _pad_lanes(q_norm * (MLA_QK ** -0.5))
    kn = _pad_lanes(k_norm)
    half = MLA_ROPE // 2
    inv_freq = ROPE_THETA ** (-jnp.arange(half, dtype=F32) / half)
    invf = jnp.zeros((LANES,), F32)
    invf = invf.at[MLA_NOPE:MLA_NOPE + half].set(inv_freq)
    invf = invf.at[MLA_NOPE + half:MLA_QK].set(inv_freq)
    return w.astype(BF16), wq.astype(BF16), wkv.astype(BF16), qn, kn, invf.reshape(1, LANES)


def kernel(x, positions, l0_attn_norm, l0_w_in, l0_mla_q_a_norm, l0_mla_w_q_up, l0_mla_kv_a_norm, l0_mla_w_kv_up, l0_mla_q_norm, l0_mla_k_norm, l0_w_o, l0_ffn_norm, l0_w_gate, l0_w_up, l0_w_down, l1_attn_norm, l1_w_in, l1_fox_f_bias, l1_fox_q_norm, l1_fox_k_norm, l1_w_o, l1_ffn_norm, l1_w_gate, l1_w_up, l1_w_down):
    b, s, d = x.shape
    assert d == D_MODEL and s % ATT_TILE == 0 and ATT_TILE == ROW_TILE
    row = lambda v: v.reshape(1, -1)

    w0, wq, wkv, qn0, kn0, invf = _prep_layer0(
        l0_w_in, l0_mla_w_q_up, l0_mla_w_kv_up, l0_mla_q_norm, l0_mla_k_norm)
    qm, km, vtm, sq, sk, vts = _in0_call(
        x, positions.reshape(b, s, 1), row(l0_attn_norm), w0, row(l0_mla_q_a_norm), wq,
        row(l0_mla_kv_a_norm), wkv, qn0, kn0, invf)
    o_mla = _softmax_attn_call(qm, km, vtm, "l0_mla_attn")
    o_sb = _sb_attn_call(sq, sk, vts)
    n_mla = MLA_HEADS * MLA_V
    x1 = _out_ffn_call(
        x.reshape(b * s, d),
        [o_mla.reshape(b * s, n_mla), o_sb.reshape(b * s, SB_WIDTH)],
        [l0_w_o[:n_mla].astype(BF16), l0_w_o[n_mla:].astype(BF16)],
        row(l0_ffn_norm), l0_w_gate.astype(BF16), l0_w_up.astype(BF16),
        l0_w_down.astype(BF16), "l0_out_ffn")

    w1 = jnp.pad(l1_w_in, ((0, 0), (0, LANES - FOX_HEADS))).astype(BF16)
    gain2 = lambda gvec: jnp.concatenate([gvec, gvec]).reshape(1, LANES)
    qa, ka, vtf = _in1_call(
        x1.reshape(b, s, d), row(l1_attn_norm), w1, _pad_lanes(l1_fox_f_bias),
        gain2(l1_fox_q_norm * (FOX_DIM ** -0.5)), gain2(l1_fox_k_norm))
    o_fox = _softmax_attn_call(qa, ka, vtf, "l1_fox_attn")
    out = _out_ffn_call(
        x1, [o_fox.reshape(b * s, FOX_WIDTH)], [l1_w_o.astype(BF16)],
        row(l1_ffn_norm), l1_w_gate.astype(BF16), l1_w_up.astype(BF16),
        l1_w_down.astype(BF16), "l1_out_ffn")
    return out.reshape(b, s, d)
```

```python
import functools

import jax
import jax.numpy as jnp
from jax import lax
from jax.experimental import pallas as pl
from jax.experimental.pallas import tpu as pltpu

F32 = jnp.float32
BF16 = jnp.bfloat16

D_MODEL = 1024
EPS = 1e-6
NEG_INF = -1e30
ROPE_THETA = 10000.0

MLA_HEADS = 8
MLA_Q_RANK = 256
MLA_KV_RANK = 128
MLA_NOPE = 64
MLA_ROPE = 32
MLA_QK = MLA_NOPE + MLA_ROPE
MLA_V = 64
SB_HEADS = 8
SB_DIM = 64
SB_WIDTH = SB_HEADS * SB_DIM
FOX_HEADS = 16
FOX_DIM = 64
FOX_WIDTH = FOX_HEADS * FOX_DIM

LANES = 128
V7X_VMEM_BYTES = 64 * 1024 * 1024
VMEM_LIMIT = V7X_VMEM_BYTES * 7 // 8

ROW_TILE = 512
K_TILE = 512
Q_TILE = 1024
CHAIN = 256
FFN_TILE = 256
SB_SUB = 128
LOG2E = 1.4426950408889634

AUG0 = FOX_DIM


def _compiler_params(n_axes):
    return pltpu.CompilerParams(
        dimension_semantics=("arbitrary",) * n_axes,
        vmem_limit_bytes=VMEM_LIMIT,
    )


def _full(shape):
    return pl.BlockSpec(shape, lambda *_: (0,) * len(shape), pipeline_mode=pl.Buffered(1))


def _lane_iota(rows):
    return lax.broadcasted_iota(jnp.int32, (rows, LANES), 1)


def _rms(x, gain, width):
    ss = jnp.sum(x * x, axis=-1, keepdims=True)
    return x * lax.rsqrt(ss * (1.0 / width) + EPS) * gain


def _log_sigmoid(z):
    return jnp.minimum(z, 0.0) - jnp.log(1.0 + jnp.exp(-jnp.abs(z)))


def _split3(x):
    a = x.astype(BF16)
    r = x - a.astype(F32)
    b = r.astype(BF16)
    c = (r - b.astype(F32)).astype(BF16)
    return a, b, c


def _dot(a, b):
    return jnp.dot(a, b, preferred_element_type=F32)


def _dot_nt(a, b):
    return lax.dot_general(a, b, (((1,), (1,)), ((), ())), preferred_element_type=F32)


def _in0_kernel(x_ref, pos_ref, g_ref, w_ref, qan_ref, wq_ref, kvan_ref, wkv_ref,
                qn_ref, kn_ref, invf_ref,
                qm_ref, km_ref, vtm_ref, sq_ref, sk_ref, vts_ref):
    x = x_ref[0]
    h = _rms(x, g_ref[...], D_MODEL).astype(BF16)
    proj = _dot(h, w_ref[...])
    cq = proj[:, 0:MLA_Q_RANK]
    ckv = proj[:, MLA_Q_RANK:MLA_Q_RANK + MLA_KV_RANK]
    o = MLA_Q_RANK + MLA_KV_RANK
    k_rope_group = proj[:, o:o + LANES]
    o += LANES
    sq = proj[:, o:o + SB_WIDTH]
    sk = proj[:, o + SB_WIDTH:o + 2 * SB_WIDTH]
    sv = proj[:, o + 2 * SB_WIDTH:o + 3 * SB_WIDTH]

    q_all = _dot(_rms(cq, qan_ref[...], MLA_Q_RANK).astype(BF16), wq_ref[...])
    kv_all = _dot(_rms(ckv, kvan_ref[...], MLA_KV_RANK).astype(BF16), wkv_ref[...])

    rows = x.shape[0]
    ang = pos_ref[0].astype(F32) * invf_ref[...]
    cos = jnp.cos(ang)
    sin = jnp.sin(ang)
    lane = _lane_iota(rows)
    half = MLA_ROPE // 2
    first = (lane >= MLA_NOPE) & (lane < MLA_NOPE + half)
    second = (lane >= MLA_NOPE + half) & (lane < MLA_QK)
    sin_first = jnp.where(first, -sin, 0.0)
    sin_second = jnp.where(second, sin, 0.0)

    def norm_rope(t, gain):
        y = _rms(t, gain, MLA_QK)
        return (y * cos + pltpu.roll(y, LANES - half, 1) * sin_first
                + pltpu.roll(y, half, 1) * sin_second)

    qn = qn_ref[...]
    kn = kn_ref[...]
    for hd in range(MLA_HEADS):
        sl = slice(hd * LANES, (hd + 1) * LANES)
        qm_ref[0, hd] = norm_rope(q_all[:, sl], qn).astype(BF16)
        km_ref[0, hd] = norm_rope(kv_all[:, sl] + k_rope_group, kn).astype(BF16)
    v = kv_all[:, MLA_HEADS * LANES:]
    vtm_ref[0, 0] = v.T.astype(BF16)
    sq_ref[0] = (sq * (SB_DIM ** -0.5 * LOG2E)).astype(BF16)
    sk_ref[0] = sk.astype(BF16)
    vts_ref[0, 0] = sv.T.astype(BF16)


def _in0_call(x, pos3, g, w, qan, wq, kvan, wkv, qn, kn, invf):
    b, s, _ = x.shape
    tm = ROW_TILE
    nt = s // tm
    n_proj = w.shape[1]
    head_spec = pl.BlockSpec((1, MLA_HEADS, tm, LANES), lambda bi, i: (bi, 0, i, 0))
    vt_spec = pl.BlockSpec((1, 1, SB_WIDTH, tm), lambda bi, i: (bi, i, 0, 0))
    row_spec = pl.BlockSpec((1, tm, SB_WIDTH), lambda bi, i: (bi, i, 0))
    return pl.pallas_call(
        _in0_kernel,
        grid=(b, nt),
        in_specs=[
            pl.BlockSpec((1, tm, D_MODEL), lambda bi, i: (bi, i, 0)),
            pl.BlockSpec((1, tm, 1), lambda bi, i: (bi, i, 0)),
            _full((1, D_MODEL)),
            _full((D_MODEL, n_proj)),
            _full((1, MLA_Q_RANK)),
            _full(wq.shape),
            _full((1, MLA_KV_RANK)),
            _full(wkv.shape),
            _full((1, LANES)),
            _full((1, LANES)),
            _full((1, LANES)),
        ],
        out_specs=[head_spec, head_spec, vt_spec, row_spec, row_spec, vt_spec],
        out_shape=[
            jax.ShapeDtypeStruct((b, MLA_HEADS, s, LANES), BF16),
            jax.ShapeDtypeStruct((b, MLA_HEADS, s, LANES), BF16),
            jax.ShapeDtypeStruct((b, nt, MLA_HEADS * MLA_V, tm), BF16),
            jax.ShapeDtypeStruct((b, s, SB_WIDTH), BF16),
            jax.ShapeDtypeStruct((b, s, SB_WIDTH), BF16),
            jax.ShapeDtypeStruct((b, nt, SB_WIDTH, tm), BF16),
        ],
        compiler_params=_compiler_params(2),
        name="l0_in",
    )(x, pos3, g, w, qan, wq, kvan, wkv, qn, kn, invf)


def _diag_spans(n_chains, chain, tk):
    for c in range(n_chains):
        end = (c + 1) * chain
        for d in range(-(-end // tk)):
            nk = min(tk, end - d * tk)
            yield c, d, nk, d * tk + nk > c * chain


def _softmax_attn_kernel(q_ref, k_ref, vt_ref, o_ref, m_scr, l_scr, acc_scr, *, head_dim):
    tq, tk, cw = Q_TILE, K_TILE, CHAIN
    n_c = tq // cw
    kt_per_q = tq // tk
    qi = pl.program_id(2)
    chains = [(hh, c) for hh in range(2) for c in range(n_c)]
    qs = [q_ref[0, hh, c * cw:(c + 1) * cw, :] for hh, c in chains]

    m_scr[...] = jnp.full(m_scr.shape, NEG_INF, F32)
    l_scr[...] = jnp.zeros(l_scr.shape, F32)
    acc_scr[...] = jnp.zeros(acc_scr.shape, F32)

    def update(idx, s, vt, visible):
        if visible is not None:
            s = jnp.where(visible, s, NEG_INF)
        m = m_scr[idx]
        m_new = jnp.maximum(m, jnp.max(s, axis=0, keepdims=True))
        p = jnp.exp2(s - m_new)
        alpha = jnp.exp2(m - m_new)
        m_scr[idx] = m_new
        l_scr[idx] = alpha * l_scr[idx] + jnp.sum(p, axis=0, keepdims=True)
        acc_scr[idx] = alpha * acc_scr[idx] + _dot(vt, p.astype(BF16))

    def full_step(j, _):
        ks = [k_ref[0, hh, pl.ds(pl.multiple_of(j * tk, tk), tk), :] for hh in range(2)]
        vts = [vt_ref[0, j, hh * head_dim:(hh + 1) * head_dim, :] for hh in range(2)]
        scores = [_dot_nt(ks[hh], q) for (hh, _), q in zip(chains, qs)]
        for idx, ((hh, _), s) in enumerate(zip(chains, scores)):
            update(idx, s, vts[hh], None)
        return 0

    lax.fori_loop(0, qi * kt_per_q, full_step, 0)

    spans = [(hh, c, d, nk, mask) for hh in range(2)
             for c, d, nk, mask in _diag_spans(n_c, cw, tk)]
    scores = []
    for hh, c, d, nk, _ in spans:
        start = pl.multiple_of(qi * tq + d * tk, tk)
        scores.append(_dot_nt(k_ref[0, hh, pl.ds(start, nk), :], qs[hh * n_c + c]))
    for (hh, c, d, nk, mask), s in zip(spans, scores):
        visible = None
        if mask:
            row = lax.broadcasted_iota(jnp.int32, (nk, cw), 0) + d * tk
            col = lax.broadcasted_iota(jnp.int32, (nk, cw), 1) + c * cw
            visible = row <= col
        vt = vt_ref[0, qi * kt_per_q + d, hh * head_dim:(hh + 1) * head_dim, 0:nk]
        update(hh * n_c + c, s, vt, visible)

    outs = [jnp.concatenate([acc_scr[hh * n_c + c] / l_scr[hh * n_c + c] for c in range(n_c)],
                            axis=1) for hh in range(2)]
    o_ref[0] = jnp.concatenate(outs, axis=0).T.astype(o_ref.dtype)


def _softmax_attn_call(q, k, vt, name):
    b, h, s, _ = q.shape
    tq = Q_TILE
    nkt = s // K_TILE
    head_dim = vt.shape[2] // h
    n_chains = 2 * tq // CHAIN
    return pl.pallas_call(
        functools.partial(_softmax_attn_kernel, head_dim=head_dim),
        grid=(b, h // 2, s // tq),
        in_specs=[
            pl.BlockSpec((1, 2, tq, LANES), lambda bi, p, i: (bi, p, i, 0)),
            pl.BlockSpec((1, 2, s, LANES), lambda bi, p, i: (bi, p, 0, 0)),
            pl.BlockSpec((1, nkt, 2 * head_dim, K_TILE), lambda bi, p, i: (bi, 0, p, 0)),
        ],
        out_specs=pl.BlockSpec((1, tq, 2 * head_dim), lambda bi, p, i: (bi, i, p)),
        out_shape=jax.ShapeDtypeStruct((b, s, h * head_dim), BF16),
        scratch_shapes=[
            pltpu.VMEM((n_chains, 1, CHAIN), F32),
            pltpu.VMEM((n_chains, 1, CHAIN), F32),
            pltpu.VMEM((n_chains, head_dim, CHAIN), F32),
        ],
        compiler_params=_compiler_params(3),
        name=name,
    )(q, k, vt)


def _sb_attn_kernel(q_ref, k_ref, vt_ref, o_ref, run_scr, acc_scr):
    tq, tk, cw, sub = Q_TILE, K_TILE, CHAIN, SB_SUB
    n_c = tq // cw
    kt_per_q = tq // tk
    qi = pl.program_id(2)
    s_row = lax.broadcasted_iota(jnp.int32, (sub, sub), 0)
    s_col = lax.broadcasted_iota(jnp.int32, (sub, sub), 1)
    later = (s_col > s_row).astype(BF16)
    later2 = jnp.concatenate([later, later], axis=1)
    lane = _lane_iota(cw)
    chains = [(hh, c) for hh in range(2) for c in range(n_c)]
    qs = []
    for hh, c in chains:
        q_pair = q_ref[0, c * cw:(c + 1) * cw, :]
        in_head = (lane >= hh * SB_DIM) & (lane < (hh + 1) * SB_DIM)
        qs.append(jnp.where(in_head, q_pair, jnp.zeros_like(q_pair)))

    run_scr[...] = jnp.zeros(run_scr.shape, F32)
    acc_scr[...] = jnp.zeros(acc_scr.shape, F32)

    def suffix_sums(z, visible):
        log_beta = jnp.minimum(z, 0.0) - jnp.log(1.0 + jnp.exp2(-jnp.abs(z))) * LOG2E
        log_1m = log_beta - z
        if visible is not None:
            log_1m = jnp.where(visible, log_1m, 0.0)
        hi = log_1m.astype(BF16)
        lo = (log_1m - hi.astype(F32)).astype(BF16)
        blocks = [slice(b * sub, (b + 1) * sub) for b in range(z.shape[0] // sub)]
        suffix = [_dot(later2, jnp.concatenate([hi[bl], lo[bl]], axis=0)) for bl in blocks]
        return log_beta, log_1m, blocks, suffix

    def accumulate(idx, stage, vt, visible):
        log_beta, log_1m, blocks, suffix = stage
        run = run_scr[idx]
        terms = []
        for bl, sfx in reversed(list(zip(blocks, suffix))):
            terms.append(log_beta[bl] + sfx + run)
            run = run + sfx[0:1, :] + log_1m[bl.start:bl.start + 1, :]
        a = jnp.exp2(jnp.concatenate(terms[::-1], axis=0))
        if visible is not None:
            a = jnp.where(visible, a, 0.0)
        run_scr[idx] = run
        acc_scr[idx] = acc_scr[idx] + _dot(vt, a.astype(BF16))

    def tile_update(idxs, ks, vts, masks):
        scores = [_dot_nt(k, qs[idx]) for idx, k in zip(idxs, ks)]
        stages = [suffix_sums(z, vis) for z, vis in zip(scores, masks)]
        for idx, stage, vt, vis in zip(idxs, stages, vts, masks):
            accumulate(idx, stage, vt, vis)

    for d in reversed(range(kt_per_q)):
        idxs, ks, vts, masks = [], [], [], []
        for hh in range(2):
            for c, dd, nk, mask in _diag_spans(n_c, cw, tk):
                if dd != d:
                    continue
                visible = None
                if mask:
                    row = lax.broadcasted_iota(jnp.int32, (nk, cw), 0) + d * tk
                    col = lax.broadcasted_iota(jnp.int32, (nk, cw), 1) + c * cw
                    visible = row < col
                start = pl.multiple_of(qi * tq + d * tk, tk)
                idxs.append(hh * n_c + c)
                ks.append(k_ref[0, pl.ds(start, nk), :])
                vts.append(vt_ref[0, qi * kt_per_q + d, hh * SB_DIM:(hh + 1) * SB_DIM, 0:nk])
                masks.append(visible)
        tile_update(idxs, ks, vts, masks)

    def full_step(i, _):
        j = qi * kt_per_q - 1 - i
        k = k_ref[0, pl.ds(pl.multiple_of(j * tk, tk), tk), :]
        vts = [vt_ref[0, j, hh * SB_DIM:(hh + 1) * SB_DIM, :] for hh, _ in chains]
        tile_update(list(range(len(chains))), [k] * len(chains), vts, [None] * len(chains))
        return 0

    lax.fori_loop(0, qi * kt_per_q, full_step, 0)
    outs = [jnp.concatenate([acc_scr[hh * n_c + c] for c in range(n_c)], axis=1)
            for hh in range(2)]
    o_ref[0] = jnp.concatenate(outs, axis=0).T.astype(o_ref.dtype)


def _sb_attn_call(q, k, vt):
    b, s, w = q.shape
    tq = Q_TILE
    nkt = s // K_TILE
    n_chains = 2 * tq // CHAIN
    return pl.pallas_call(
        _sb_attn_kernel,
        grid=(b, w // LANES, s // tq),
        in_specs=[
            pl.BlockSpec((1, tq, LANES), lambda bi, p, i: (bi, i, p)),
            pl.BlockSpec((1, s, LANES), lambda bi, p, i: (bi, 0, p)),
            pl.BlockSpec((1, nkt, LANES, K_TILE), lambda bi, p, i: (bi, 0, p, 0)),
        ],
        out_specs=pl.BlockSpec((1, tq, LANES), lambda bi, p, i: (bi, i, p)),
        out_shape=jax.ShapeDtypeStruct((b, s, w), BF16),
        scratch_shapes=[
            pltpu.VMEM((n_chains, 1, CHAIN), F32),
            pltpu.VMEM((n_chains, SB_DIM, CHAIN), F32),
        ],
        compiler_params=_compiler_params(3),
        name="l0_sb_attn",
    )(q, k, vt)


def _out_ffn_kernel(*refs, n_mix):
    x_ref = refs[0]
    o_refs = refs[1:1 + n_mix]
    wo_refs = refs[1 + n_mix:1 + 2 * n_mix]
    g_ref, wg_ref, wu_ref, wd_ref, y_ref = refs[1 + 2 * n_mix:]
    y = x_ref[...]
    for o_ref, wo_ref in zip(o_refs, wo_refs):
        y = y + _dot(o_ref[...], wo_ref[...])
    h = _rms(y, g_ref[...], D_MODEL).astype(BF16)
    gate = _dot(h, wg_ref[...])
    up = _dot(h, wu_ref[...])
    act = gate * (1.0 / (1.0 + jnp.exp(-gate))) * up
    y_ref[...] = y + _dot(act.astype(BF16), wd_ref[...])


def _out_ffn_call(x2, mixes, wos, g, wg, wu, wd, name):
    n, d = x2.shape
    tm = FFN_TILE
    n_mix = len(mixes)
    row = lambda width: pl.BlockSpec((tm, width), lambda i: (i, 0))
    return pl.pallas_call(
        functools.partial(_out_ffn_kernel, n_mix=n_mix),
        grid=(n // tm,),
        in_specs=([row(d)] + [row(m.shape[1]) for m in mixes]
                  + [_full(w.shape) for w in wos]
                  + [_full((1, d)), _full(wg.shape), _full(wu.shape), _full(wd.shape)]),
        out_specs=row(d),
        out_shape=jax.ShapeDtypeStruct((n, d), F32),
        compiler_params=_compiler_params(1),
        name=name,
    )(x2, *mixes, *wos, g, wg, wu, wd)


def _in1_kernel(x_ref, g_ref, w_ref, fb_ref, qn_ref, kn_ref,
                qa_ref, ka_ref, vt_ref, carry_ref):
    i = pl.program_id(1)

    @pl.when(i == 0)
    def _():
        carry_ref[...] = jnp.zeros_like(carry_ref)

    x = x_ref[0]
    rows = x.shape[0]
    h = _rms(x, g_ref[...], D_MODEL).astype(BF16)
    proj = _dot(h, w_ref[...])
    q = proj[:, 0:FOX_WIDTH]
    k = proj[:, FOX_WIDTH:2 * FOX_WIDTH]
    v = proj[:, 2 * FOX_WIDTH:3 * FOX_WIDTH]
    f_logit = proj[:, 3 * FOX_WIDTH:]

    log_f = _log_sigmoid(f_logit + fb_ref[...])
    r_i = lax.broadcasted_iota(jnp.int32, (rows, rows), 0)
    c_i = lax.broadcasted_iota(jnp.int32, (rows, rows), 1)
    lower = (c_i <= r_i).astype(BF16)
    f1, f2, f3 = _split3(log_f)
    decay = carry_ref[...] + _dot(lower, f1) + _dot(lower, f2) + _dot(lower, f3)
    carry_ref[...] = decay[rows - 1:rows, :]
    d1, d2, d3 = (d.astype(F32) for d in _split3(decay * LOG2E))

    lane = _lane_iota(rows)
    low_half = lane < FOX_DIM
    qn = qn_ref[...]
    kn = kn_ref[...]

    def head_norm(pair, gain, in_head):
        ss = jnp.sum(jnp.where(in_head, pair * pair, 0.0), axis=-1, keepdims=True)
        return pair * lax.rsqrt(ss * (1.0 / FOX_DIM) + EPS) * gain

    def lane_is(n):
        return lane == AUG0 + n

    for hd in range(FOX_HEADS):
        sl = slice((hd // 2) * LANES, (hd // 2 + 1) * LANES)
        in_head = low_half if hd % 2 == 0 else jnp.logical_not(low_half)
        qh = head_norm(q[:, sl], qn, in_head)
        kh = head_norm(k[:, sl], kn, in_head)
        if hd % 2 == 1:
            qh = pltpu.roll(qh, FOX_DIM, 1)
            kh = pltpu.roll(kh, FOX_DIM, 1)
        c1 = d1[:, hd:hd + 1]
        c2 = d2[:, hd:hd + 1]
        c3 = d3[:, hd:hd + 1]
        one = jnp.ones_like(c1)
        zero = jnp.zeros_like(c1)
        aug_q = jnp.where(lane_is(0), c1, jnp.where(lane_is(1), c2, jnp.where(
            lane_is(2), c3, jnp.where((lane >= AUG0 + 3) & (lane < AUG0 + 6), one, zero))))
        aug_k = jnp.where(lane_is(3), -c1, jnp.where(lane_is(4), -c2, jnp.where(
            lane_is(5), -c3, jnp.where((lane >= AUG0) & (lane < AUG0 + 3), one, zero))))
        qa_ref[0, hd] = jnp.where(low_half, qh, aug_q).astype(BF16)
        ka_ref[0, hd] = jnp.where(low_half, kh, aug_k).astype(BF16)
    vt_ref[0, 0] = v.T.astype(BF16)


def _in1_call(x, g, w, fb, qn, kn):
    b, s, _ = x.shape
    tm = ROW_TILE
    nt = s // tm
    head_spec = pl.BlockSpec((1, FOX_HEADS, tm, LANES), lambda bi, i: (bi, 0, i, 0))
    return pl.pallas_call(
        _in1_kernel,
        grid=(b, nt),
        in_specs=[
            pl.BlockSpec((1, tm, D_MODEL), lambda bi, i: (bi, i, 0)),
            _full((1, D_MODEL)),
            _full(w.shape),
            _full((1, LANES)),
            _full((1, LANES)),
            _full((1, LANES)),
        ],
        out_specs=[head_spec, head_spec,
                   pl.BlockSpec((1, 1, FOX_WIDTH, tm), lambda bi, i: (bi, i, 0, 0))],
        out_shape=[
            jax.ShapeDtypeStruct((b, FOX_HEADS, s, LANES), BF16),
            jax.ShapeDtypeStruct((b, FOX_HEADS, s, LANES), BF16),
            jax.ShapeDtypeStruct((b, nt, FOX_WIDTH, tm), BF16),
        ],
        scratch_shapes=[pltpu.VMEM((1, LANES), F32)],
        compiler_params=_compiler_params(2),
        name="l1_in",
    )(x, g, w, fb, qn, kn)


def _pad_lanes(v, width=LANES):
    return jnp.pad(v, (0, width - v.shape[0])).reshape(1, width)


def _prep_layer0(w_in, w_q_up, w_kv_up, q_norm, k_norm):
    o = MLA_Q_RANK + MLA_KV_RANK
    w_rope = w_in[:, o:o + MLA_ROPE]
    rope_group = jnp.pad(w_rope, ((0, 0), (MLA_NOPE, LANES - MLA_QK)))
    w = jnp.concatenate([w_in[:, :o], rope_group, w_in[:, o + MLA_ROPE:]], axis=1)
    wq = jnp.pad(w_q_up.reshape(MLA_Q_RANK, MLA_HEADS, MLA_QK),
                 ((0, 0), (0, 0), (0, LANES - MLA_QK))).reshape(MLA_Q_RANK, MLA_HEADS * LANES)
    kv = w_kv_up.reshape(MLA_KV_RANK, MLA_HEADS, MLA_NOPE + MLA_V)
    wk = jnp.pad(kv[:, :, :MLA_NOPE], ((0, 0), (0, 0), (0, LANES - MLA_NOPE)))
    wkv = jnp.concatenate([wk.reshape(MLA_KV_RANK, MLA_HEADS * LANES),
                           kv[:, :, MLA_NOPE:].reshape(MLA_KV_RANK, MLA_HEADS * MLA_V)], axis=1)
    qn = _pad_lanes(q_norm * (MLA_QK ** -0.5 * LOG2E))
    kn = _pad_lanes(k_norm)
    half = MLA_ROPE // 2
    inv_freq = ROPE_THETA ** (-jnp.arange(half, dtype=F32) / half)
    invf = jnp.zeros((LANES,), F32)
    invf = invf.at[MLA_NOPE:MLA_NOPE + half].set(inv_freq)
    invf = invf.at[MLA_NOPE + half:MLA_QK].set(inv_freq)
    return w.astype(BF16), wq.astype(BF16), wkv.astype(BF16), qn, kn, invf.reshape(1, LANES)


def kernel(x, positions, l0_attn_norm, l0_w_in, l0_mla_q_a_norm, l0_mla_w_q_up, l0_mla_kv_a_norm, l0_mla_w_kv_up, l0_mla_q_norm, l0_mla_k_norm, l0_w_o, l0_ffn_norm, l0_w_gate, l0_w_up, l0_w_down, l1_attn_norm, l1_w_in, l1_fox_f_bias, l1_fox_q_norm, l1_fox_k_norm, l1_w_o, l1_ffn_norm, l1_w_gate, l1_w_up, l1_w_down):
    b, s, d = x.shape
    assert d == D_MODEL and s % Q_TILE == 0 and K_TILE == ROW_TILE and Q_TILE % K_TILE == 0
    assert K_TILE % CHAIN == 0 and CHAIN % SB_SUB == 0
    row = lambda v: v.reshape(1, -1)

    w0, wq, wkv, qn0, kn0, invf = _prep_layer0(
        l0_w_in, l0_mla_w_q_up, l0_mla_w_kv_up, l0_mla_q_norm, l0_mla_k_norm)
    qm, km, vtm, sq, sk, vts = _in0_call(
        x, positions.reshape(b, s, 1), row(l0_attn_norm), w0, row(l0_mla_q_a_norm), wq,
        row(l0_mla_kv_a_norm), wkv, qn0, kn0, invf)
    o_mla = _softmax_attn_call(qm, km, vtm, "l0_mla_attn")
    o_sb = _sb_attn_call(sq, sk, vts)
    n_mla = MLA_HEADS * MLA_V
    x1 = _out_ffn_call(
        x.reshape(b * s, d),
        [o_mla.reshape(b * s, n_mla), o_sb.reshape(b * s, SB_WIDTH)],
        [l0_w_o[:n_mla].astype(BF16), l0_w_o[n_mla:].astype(BF16)],
        row(l0_ffn_norm), l0_w_gate.astype(BF16), l0_w_up.astype(BF16),
        l0_w_down.astype(BF16), "l0_out_ffn")

    w1 = jnp.pad(l1_w_in, ((0, 0), (0, LANES - FOX_HEADS))).astype(BF16)
    gain2 = lambda gvec: jnp.concatenate([gvec, gvec]).reshape(1, LANES)
    qa, ka, vtf = _in1_call(
        x1.reshape(b, s, d), row(l1_attn_norm), w1, _pad_lanes(l1_fox_f_bias),
        gain2(l1_fox_q_norm * (FOX_DIM ** -0.5 * LOG2E)), gain2(l1_fox_k_norm))
    o_fox = _softmax_attn_call(qa, ka, vtf, "l1_fox_attn")
    out = _out_ffn_call(
        x1, [o_fox.reshape(b * s, FOX_WIDTH)], [l1_w_o.astype(BF16)],
        row(l1_ffn_norm), l1_w_gate.astype(BF16), l1_w_up.astype(BF16),
        l1_w_down.astype(BF16), "l1_out_ffn")
    return out.reshape(b, s, d)
```

```python
import functools

import jax
import jax.numpy as jnp
from jax import lax
from jax.experimental import pallas as pl
from jax.experimental.pallas import tpu as pltpu

F32 = jnp.float32
BF16 = jnp.bfloat16

D_MODEL = 1024
EPS = 1e-6
NEG_INF = -1e30
ROPE_THETA = 10000.0

MLA_HEADS = 8
MLA_Q_RANK = 256
MLA_KV_RANK = 128
MLA_NOPE = 64
MLA_ROPE = 32
MLA_QK = MLA_NOPE + MLA_ROPE
MLA_V = 64
SB_HEADS = 8
SB_DIM = 64
SB_WIDTH = SB_HEADS * SB_DIM
FOX_HEADS = 16
FOX_DIM = 64
FOX_WIDTH = FOX_HEADS * FOX_DIM

LANES = 128
V7X_VMEM_BYTES = 64 * 1024 * 1024
VMEM_LIMIT = V7X_VMEM_BYTES * 7 // 8

ROW_TILE = 512
K_TILE = 512
Q_TILE = 1024
CHAIN = 256
FFN_TILE = 256
ONES_ROWS = 16
SB_SUB = 128
LOG2E = 1.4426950408889634

AUG_TERMS = 3
IN_SUB = 256


def _compiler_params(n_axes):
    return pltpu.CompilerParams(
        dimension_semantics=("arbitrary",) * n_axes,
        vmem_limit_bytes=VMEM_LIMIT,
    )


def _full(shape):
    return pl.BlockSpec(shape, lambda *_: (0,) * len(shape), pipeline_mode=pl.Buffered(1))


def _lane_iota(rows):
    return lax.broadcasted_iota(jnp.int32, (rows, LANES), 1)


def _rms(x, gain, width):
    ss = jnp.sum(x * x, axis=-1, keepdims=True)
    return x * lax.rsqrt(ss * (1.0 / width) + EPS) * gain


def _log_sigmoid(z):
    return jnp.minimum(z, 0.0) - jnp.log(1.0 + jnp.exp(-jnp.abs(z)))


def _split3(x):
    a = x.astype(BF16)
    r = x - a.astype(F32)
    b = r.astype(BF16)
    c = (r - b.astype(F32)).astype(BF16)
    return a, b, c


def _dot(a, b):
    return jnp.dot(a, b, preferred_element_type=F32)


def _dot_nt(a, b):
    return lax.dot_general(a, b, (((1,), (1,)), ((), ())), preferred_element_type=F32)


def _in0_kernel(x_ref, pos_ref, g_ref, w_ref, wsvt_ref, qan_ref, wq_ref, kvan_ref, wk_ref,
                wvt_ref, gq_ref, gqp_ref, gk_ref, gkp_ref, invf_ref,
                qm_ref, km_ref, vtm_ref, sq_ref, sk_ref, vts_ref):
    g = g_ref[...]
    for r in range(ROW_TILE // IN_SUB):
        rows = slice(r * IN_SUB, (r + 1) * IN_SUB)
        h = _rms(x_ref[0, rows, :], g, D_MODEL).astype(BF16)
        proj = _dot(h, w_ref[...])
        vts_ref[0, 0, :, rows] = _dot_nt(wsvt_ref[...], h).astype(BF16)
        o = MLA_Q_RANK + MLA_KV_RANK
        cq = proj[:, 0:MLA_Q_RANK]
        ckv = proj[:, MLA_Q_RANK:o]
        k_rope = proj[:, o:o + LANES]
        k_rope_partner = proj[:, o + LANES:o + 2 * LANES]
        o += 2 * LANES
        sq_ref[0, rows, :] = (proj[:, o:o + SB_WIDTH] * (SB_DIM ** -0.5 * LOG2E)).astype(BF16)
        sk_ref[0, rows, :] = proj[:, o + SB_WIDTH:o + 2 * SB_WIDTH].astype(BF16)

        qn = _rms(cq, qan_ref[...], MLA_Q_RANK).astype(BF16)
        q_all = _dot(qn, wq_ref[...])
        kvn = _rms(ckv, kvan_ref[...], MLA_KV_RANK).astype(BF16)
        k_all = _dot(kvn, wk_ref[...])
        vtm_ref[0, 0, :, rows] = _dot_nt(wvt_ref[...], kvn).astype(BF16)

        ang = pos_ref[0, rows, :].astype(F32) * invf_ref[...]
        cos = jnp.cos(ang)
        sin = jnp.sin(ang)
        gq_cos = gq_ref[...] * cos
        gq_sin = gqp_ref[...] * sin
        gk_cos = gk_ref[...] * cos
        k_partner_term = k_rope_partner * (gkp_ref[...] * sin)

        def inv_rms(t):
            ss = jnp.sum(t * t, axis=-1, keepdims=True)
            return lax.rsqrt(ss * (1.0 / MLA_QK) + EPS)

        for hd in range(MLA_HEADS):
            sl = slice(hd * LANES, (hd + 1) * LANES)
            t = q_all[:, sl]
            t_partner = q_all[:, MLA_HEADS * LANES + hd * LANES:MLA_HEADS * LANES + (hd + 1) * LANES]
            qm_ref[0, hd, rows, :] = (
                (t * gq_cos + t_partner * gq_sin) * inv_rms(t)).astype(BF16)
            tk = k_all[:, sl] + k_rope
            km_ref[0, hd, rows, :] = (
                (tk * gk_cos + k_partner_term) * inv_rms(tk)).astype(BF16)


def _in0_call(x, pos3, g, w, wsvt, qan, wq, kvan, wk, wvt, gq, gqp, gk, gkp, invf):
    b, s, _ = x.shape
    tm = ROW_TILE
    nt = s // tm
    head_spec = pl.BlockSpec((1, MLA_HEADS, tm, LANES), lambda bi, i: (bi, 0, i, 0))
    vt_spec = pl.BlockSpec((1, 1, SB_WIDTH, tm), lambda bi, i: (bi, i, 0, 0))
    row_spec = pl.BlockSpec((1, tm, SB_WIDTH), lambda bi, i: (bi, i, 0))
    lane_vec = _full((1, LANES))
    return pl.pallas_call(
        _in0_kernel,
        grid=(b, nt),
        in_specs=[
            pl.BlockSpec((1, tm, D_MODEL), lambda bi, i: (bi, i, 0)),
            pl.BlockSpec((1, tm, 1), lambda bi, i: (bi, i, 0)),
            _full((1, D_MODEL)),
            _full(w.shape),
            _full(wsvt.shape),
            _full((1, MLA_Q_RANK)),
            _full(wq.shape),
            _full((1, MLA_KV_RANK)),
            _full(wk.shape),
            _full(wvt.shape),
            lane_vec, lane_vec, lane_vec, lane_vec, lane_vec,
        ],
        out_specs=[head_spec, head_spec, vt_spec, row_spec, row_spec, vt_spec],
        out_shape=[
            jax.ShapeDtypeStruct((b, MLA_HEADS, s, LANES), BF16),
            jax.ShapeDtypeStruct((b, MLA_HEADS, s, LANES), BF16),
            jax.ShapeDtypeStruct((b, nt, MLA_HEADS * MLA_V, tm), BF16),
            jax.ShapeDtypeStruct((b, s, SB_WIDTH), BF16),
            jax.ShapeDtypeStruct((b, s, SB_WIDTH), BF16),
            jax.ShapeDtypeStruct((b, nt, SB_WIDTH, tm), BF16),
        ],
        compiler_params=_compiler_params(2),
        name="l0_in",
    )(x, pos3, g, w, wsvt, qan, wq, kvan, wk, wvt, gq, gqp, gk, gkp, invf)


def _diag_spans(n_chains, chain, tk):
    for c in range(n_chains):
        end = (c + 1) * chain
        for d in range(-(-end // tk)):
            nk = min(tk, end - d * tk)
            yield c, d, nk, d * tk + nk > c * chain


def _softmax_attn_kernel(q_ref, k_ref, vt_ref, o_ref, m_scr, acc_scr, *, head_dim):
    tq, tk, cw = Q_TILE, K_TILE, CHAIN
    n_c = tq // cw
    kt_per_q = tq // tk
    qi = pl.program_id(2)
    chains = [(hh, c) for hh in range(2) for c in range(n_c)]
    qs = [q_ref[0, hh, c * cw:(c + 1) * cw, :] for hh, c in chains]

    m_scr[...] = jnp.full(m_scr.shape, NEG_INF, F32)
    acc_scr[...] = jnp.zeros(acc_scr.shape, F32)

    def with_ones(vt):
        return jnp.concatenate([vt, jnp.ones((ONES_ROWS, vt.shape[1]), vt.dtype)], axis=0)

    def update(idx, s, vt, visible):
        if visible is not None:
            s = jnp.where(visible, s, NEG_INF)
        m = m_scr[idx]
        m_new = jnp.maximum(m, jnp.max(s, axis=0, keepdims=True))
        p = jnp.exp2(s - m_new)
        m_scr[idx] = m_new
        acc_scr[idx] = jnp.exp2(m - m_new) * acc_scr[idx] + _dot(vt, p.astype(BF16))

    def full_step(j, _):
        ks = [k_ref[0, hh, pl.ds(pl.multiple_of(j * tk, tk), tk), :] for hh in range(2)]
        vts = [with_ones(vt_ref[0, j, hh * head_dim:(hh + 1) * head_dim, :]) for hh in range(2)]
        scores = [_dot_nt(ks[hh], q) for (hh, _), q in zip(chains, qs)]
        for idx, ((hh, _), s) in enumerate(zip(chains, scores)):
            update(idx, s, vts[hh], None)
        return 0

    lax.fori_loop(0, qi * kt_per_q, full_step, 0)

    spans = [(hh, c, d, nk, mask) for hh in range(2)
             for c, d, nk, mask in _diag_spans(n_c, cw, tk)]
    scores = []
    for hh, c, d, nk, _ in spans:
        start = pl.multiple_of(qi * tq + d * tk, tk)
        scores.append(_dot_nt(k_ref[0, hh, pl.ds(start, nk), :], qs[hh * n_c + c]))
    for (hh, c, d, nk, mask), s in zip(spans, scores):
        visible = None
        if mask:
            row = lax.broadcasted_iota(jnp.int32, (nk, cw), 0) + d * tk
            col = lax.broadcasted_iota(jnp.int32, (nk, cw), 1) + c * cw
            visible = row <= col
        vt = vt_ref[0, qi * kt_per_q + d, hh * head_dim:(hh + 1) * head_dim, 0:nk]
        update(hh * n_c + c, s, with_ones(vt), visible)

    def normalized(idx):
        acc = acc_scr[idx]
        return acc[:head_dim] / acc[head_dim:head_dim + 1]

    outs = [jnp.concatenate([normalized(hh * n_c + c) for c in range(n_c)], axis=1)
            for hh in range(2)]
    o_ref[0] = jnp.concatenate(outs, axis=0).T.astype(o_ref.dtype)


def _softmax_attn_call(q, k, vt, name):
    b, h, s, _ = q.shape
    tq = Q_TILE
    nkt = s // K_TILE
    head_dim = vt.shape[2] // h
    n_chains = 2 * tq // CHAIN
    return pl.pallas_call(
        functools.partial(_softmax_attn_kernel, head_dim=head_dim),
        grid=(b, h // 2, s // tq),
        in_specs=[
            pl.BlockSpec((1, 2, tq, LANES), lambda bi, p, i: (bi, p, i, 0)),
            pl.BlockSpec((1, 2, s, LANES), lambda bi, p, i: (bi, p, 0, 0)),
            pl.BlockSpec((1, nkt, 2 * head_dim, K_TILE), lambda bi, p, i: (bi, 0, p, 0)),
        ],
        out_specs=pl.BlockSpec((1, tq, 2 * head_dim), lambda bi, p, i: (bi, i, p)),
        out_shape=jax.ShapeDtypeStruct((b, s, h * head_dim), BF16),
        scratch_shapes=[
            pltpu.VMEM((n_chains, 1, CHAIN), F32),
            pltpu.VMEM((n_chains, head_dim + ONES_ROWS, CHAIN), F32),
        ],
        compiler_params=_compiler_params(3),
        name=name,
    )(q, k, vt)


def _sb_attn_kernel(q_ref, k_ref, vt_ref, o_ref, run_scr, acc_scr):
    tq, tk, cw, sub = Q_TILE, K_TILE, CHAIN, SB_SUB
    n_c = tq // cw
    kt_per_q = tq // tk
    qi = pl.program_id(2)
    s_row = lax.broadcasted_iota(jnp.int32, (sub, sub), 0)
    s_col = lax.broadcasted_iota(jnp.int32, (sub, sub), 1)
    later = (s_col > s_row).astype(BF16)
    later2 = jnp.concatenate([later, later], axis=1)
    lane = _lane_iota(cw)
    chains = [(hh, c) for hh in range(2) for c in range(n_c)]
    qs = []
    for hh, c in chains:
        q_pair = q_ref[0, c * cw:(c + 1) * cw, :]
        in_head = (lane >= hh * SB_DIM) & (lane < (hh + 1) * SB_DIM)
        qs.append(jnp.where(in_head, q_pair, jnp.zeros_like(q_pair)))

    run_scr[...] = jnp.zeros(run_scr.shape, F32)
    acc_scr[...] = jnp.zeros(acc_scr.shape, F32)

    def suffix_sums(z, visible):
        log_beta = jnp.minimum(z, 0.0) - jnp.log(1.0 + jnp.exp2(-jnp.abs(z))) * LOG2E
        log_1m = log_beta - z
        if visible is not None:
            log_1m = jnp.where(visible, log_1m, 0.0)
        hi = log_1m.astype(BF16)
        lo = (log_1m - hi.astype(F32)).astype(BF16)
        blocks = [slice(b * sub, (b + 1) * sub) for b in range(z.shape[0] // sub)]
        suffix = [_dot(later2, jnp.concatenate([hi[bl], lo[bl]], axis=0)) for bl in blocks]
        return log_beta, log_1m, blocks, suffix

    def accumulate(idx, stage, vt, visible):
        log_beta, log_1m, blocks, suffix = stage
        run = run_scr[idx]
        terms = []
        for bl, sfx in reversed(list(zip(blocks, suffix))):
            terms.append(log_beta[bl] + sfx + run)
            run = run + sfx[0:1, :] + log_1m[bl.start:bl.start + 1, :]
        a = jnp.exp2(jnp.concatenate(terms[::-1], axis=0))
        if visible is not None:
            a = jnp.where(visible, a, 0.0)
        run_scr[idx] = run
        acc_scr[idx] = acc_scr[idx] + _dot(vt, a.astype(BF16))

    def tile_update(idxs, ks, vts, masks):
        scores = [_dot_nt(k, qs[idx]) for idx, k in zip(idxs, ks)]
        stages = [suffix_sums(z, vis) for z, vis in zip(scores, masks)]
        for idx, stage, vt, vis in zip(idxs, stages, vts, masks):
            accumulate(idx, stage, vt, vis)

    for d in reversed(range(kt_per_q)):
        idxs, ks, vts, masks = [], [], [], []
        for hh in range(2):
            for c, dd, nk, mask in _diag_spans(n_c, cw, tk):
                if dd != d:
                    continue
                visible = None
                if mask:
                    row = lax.broadcasted_iota(jnp.int32, (nk, cw), 0) + d * tk
                    col = lax.broadcasted_iota(jnp.int32, (nk, cw), 1) + c * cw
                    visible = row < col
                start = pl.multiple_of(qi * tq + d * tk, tk)
                idxs.append(hh * n_c + c)
                ks.append(k_ref[0, pl.ds(start, nk), :])
                vts.append(vt_ref[0, qi * kt_per_q + d, hh * SB_DIM:(hh + 1) * SB_DIM, 0:nk])
                masks.append(visible)
        tile_update(idxs, ks, vts, masks)

    def full_step(i, _):
        j = qi * kt_per_q - 1 - i
        k = k_ref[0, pl.ds(pl.multiple_of(j * tk, tk), tk), :]
        vts = [vt_ref[0, j, hh * SB_DIM:(hh + 1) * SB_DIM, :] for hh, _ in chains]
        tile_update(list(range(len(chains))), [k] * len(chains), vts, [None] * len(chains))
        return 0

    lax.fori_loop(0, qi * kt_per_q, full_step, 0)
    outs = [jnp.concatenate([acc_scr[hh * n_c + c] for c in range(n_c)], axis=1)
            for hh in range(2)]
    o_ref[0] = jnp.concatenate(outs, axis=0).T.astype(o_ref.dtype)


def _sb_attn_call(q, k, vt):
    b, s, w = q.shape
    tq = Q_TILE
    nkt = s // K_TILE
    n_chains = 2 * tq // CHAIN
    return pl.pallas_call(
        _sb_attn_kernel,
        grid=(b, w // LANES, s // tq),
        in_specs=[
            pl.BlockSpec((1, tq, LANES), lambda bi, p, i: (bi, i, p)),
            pl.BlockSpec((1, s, LANES), lambda bi, p, i: (bi, 0, p)),
            pl.BlockSpec((1, nkt, LANES, K_TILE), lambda bi, p, i: (bi, 0, p, 0)),
        ],
        out_specs=pl.BlockSpec((1, tq, LANES), lambda bi, p, i: (bi, i, p)),
        out_shape=jax.ShapeDtypeStruct((b, s, w), BF16),
        scratch_shapes=[
            pltpu.VMEM((n_chains, 1, CHAIN), F32),
            pltpu.VMEM((n_chains, SB_DIM, CHAIN), F32),
        ],
        compiler_params=_compiler_params(3),
        name="l0_sb_attn",
    )(q, k, vt)


def _out_ffn_kernel(*refs, n_mix):
    x_ref = refs[0]
    o_refs = refs[1:1 + n_mix]
    wo_refs = refs[1 + n_mix:1 + 2 * n_mix]
    g_ref, wg_ref, wu_ref, wd_ref, y_ref = refs[1 + 2 * n_mix:]
    y = x_ref[...]
    for o_ref, wo_ref in zip(o_refs, wo_refs):
        y = y + _dot(o_ref[...], wo_ref[...])
    h = _rms(y, g_ref[...], D_MODEL).astype(BF16)
    gate = _dot(h, wg_ref[...])
    up = _dot(h, wu_ref[...])
    act = gate * (1.0 / (1.0 + jnp.exp(-gate))) * up
    y_ref[...] = y + _dot(act.astype(BF16), wd_ref[...])


def _out_ffn_call(x2, mixes, wos, g, wg, wu, wd, name):
    n, d = x2.shape
    tm = FFN_TILE
    n_mix = len(mixes)
    row = lambda width: pl.BlockSpec((tm, width), lambda i: (i, 0))
    return pl.pallas_call(
        functools.partial(_out_ffn_kernel, n_mix=n_mix),
        grid=(n // tm,),
        in_specs=([row(d)] + [row(m.shape[1]) for m in mixes]
                  + [_full(w.shape) for w in wos]
                  + [_full((1, d)), _full(wg.shape), _full(wu.shape), _full(wd.shape)]),
        out_specs=row(d),
        out_shape=jax.ShapeDtypeStruct((n, d), F32),
        compiler_params=_compiler_params(1),
        name=name,
    )(x2, *mixes, *wos, g, wg, wu, wd)


def _in1_kernel(x_ref, g_ref, w_ref, wvt_ref, fb_ref, gq_ref, gk_ref, layout_ref,
                qa_ref, ka_ref, vt_ref, carry_ref):
    i = pl.program_id(1)

    @pl.when(i == 0)
    def _():
        carry_ref[...] = jnp.zeros_like(carry_ref)

    n_terms = AUG_TERMS
    used = n_terms * FOX_HEADS
    rows = IN_SUB
    lane = _lane_iota(rows)
    half_lane = lane & (FOX_DIM - 1)
    low_half = lane < FOX_DIM
    term = jnp.broadcast_to(layout_ref[...], (rows, LANES))
    decay_lane = term < n_terms
    r_i = lax.broadcasted_iota(jnp.int32, (rows, rows), 0)
    c_i = lax.broadcasted_iota(jnp.int32, (rows, rows), 1)
    lower = (c_i <= r_i).astype(BF16)
    g = g_ref[...]
    gq = gq_ref[...]
    gk = gk_ref[...]

    for r in range(ROW_TILE // IN_SUB):
        sl_rows = slice(r * rows, (r + 1) * rows)
        h = _rms(x_ref[0, sl_rows, :], g, D_MODEL).astype(BF16)
        proj = _dot(h, w_ref[...])
        vt_ref[0, 0, :, sl_rows] = _dot_nt(wvt_ref[...], h).astype(BF16)
        q = proj[:, 0:FOX_WIDTH]
        k = proj[:, FOX_WIDTH:2 * FOX_WIDTH]
        f_logit = proj[:, 2 * FOX_WIDTH:]

        log_f = jnp.where(decay_lane, _log_sigmoid(f_logit + fb_ref[...]), 0.0)
        f1, f2, f3 = _split3(log_f)
        decay = carry_ref[...] + _dot(lower, f1) + _dot(lower, f2) + _dot(lower, f3)
        carry_ref[...] = decay[rows - 1:rows, :]
        d1, d2, d3 = (d.astype(F32) for d in _split3(decay * LOG2E))
        terms = jnp.where(term == 0, d1, jnp.where(term == 1, d2, d3))
        aug_q = jnp.where(decay_lane, terms, jnp.where(term == n_terms, 1.0, 0.0))
        neg_terms = -terms

        for pair in range(FOX_HEADS // 2):
            sl = slice(pair * LANES, (pair + 1) * LANES)
            qp = q[:, sl]
            kp = k[:, sl]
            qsq = qp * qp
            ksq = kp * kp
            for parity in range(2):
                hd = 2 * pair + parity
                in_head = low_half if parity == 0 else jnp.logical_not(low_half)

                def inv_rms(sq):
                    ss = jnp.sum(jnp.where(in_head, sq, 0.0), axis=-1, keepdims=True)
                    return lax.rsqrt(ss * (1.0 / FOX_DIM) + EPS)

                qa_ref[0, hd, sl_rows, :] = jnp.where(
                    in_head, qp * gq * inv_rms(qsq), aug_q).astype(BF16)
                own = (half_lane >= n_terms * hd) & (half_lane < n_terms * (hd + 1))
                moved = pltpu.roll(neg_terms, used - n_terms * hd, 1)
                aug_k = jnp.where(own, 1.0, jnp.where(term == n_terms, moved, 0.0))
                ka_ref[0, hd, sl_rows, :] = jnp.where(
                    in_head, kp * gk * inv_rms(ksq), aug_k).astype(BF16)


def _in1_call(x, g, w, wvt, fb, gq, gk, layout):
    b, s, _ = x.shape
    tm = ROW_TILE
    nt = s // tm
    head_spec = pl.BlockSpec((1, FOX_HEADS, tm, LANES), lambda bi, i: (bi, 0, i, 0))
    lane_vec = _full((1, LANES))
    return pl.pallas_call(
        _in1_kernel,
        grid=(b, nt),
        in_specs=[
            pl.BlockSpec((1, tm, D_MODEL), lambda bi, i: (bi, i, 0)),
            _full((1, D_MODEL)),
            _full(w.shape),
            _full(wvt.shape),
            lane_vec, lane_vec, lane_vec, lane_vec,
        ],
        out_specs=[head_spec, head_spec,
                   pl.BlockSpec((1, 1, FOX_WIDTH, tm), lambda bi, i: (bi, i, 0, 0))],
        out_shape=[
            jax.ShapeDtypeStruct((b, FOX_HEADS, s, LANES), BF16),
            jax.ShapeDtypeStruct((b, FOX_HEADS, s, LANES), BF16),
            jax.ShapeDtypeStruct((b, nt, FOX_WIDTH, tm), BF16),
        ],
        scratch_shapes=[pltpu.VMEM((1, LANES), F32)],
        compiler_params=_compiler_params(2),
        name="l1_in",
    )(x, g, w, wvt, fb, gq, gk, layout)


def _pad_lanes(v, width=LANES):
    return jnp.pad(v, (0, width - v.shape[0])).reshape(1, width)


def _rope_partner(cols):
    half = MLA_ROPE // 2
    return jnp.concatenate([-cols[..., half:], cols[..., :half]], axis=-1)


def _prep_layer0(w_in, w_q_up, w_kv_up, q_norm, k_norm):
    o = MLA_Q_RANK + MLA_KV_RANK
    pad_rope = lambda c: jnp.pad(c, ((0, 0),) * (c.ndim - 1) + ((MLA_NOPE, LANES - MLA_QK),))
    w_rope = w_in[:, o:o + MLA_ROPE]
    sb = w_in[:, o + MLA_ROPE:]
    w = jnp.concatenate([w_in[:, :o], pad_rope(w_rope), pad_rope(_rope_partner(w_rope)),
                         sb[:, :2 * SB_WIDTH]], axis=1)
    wsvt = sb[:, 2 * SB_WIDTH:].T
    wq3 = w_q_up.reshape(MLA_Q_RANK, MLA_HEADS, MLA_QK)
    wq = jnp.pad(wq3, ((0, 0), (0, 0), (0, LANES - MLA_QK)))
    wq_partner = pad_rope(_rope_partner(wq3[:, :, MLA_NOPE:]))
    wq = jnp.concatenate([wq.reshape(MLA_Q_RANK, -1), wq_partner.reshape(MLA_Q_RANK, -1)], axis=1)
    kv = w_kv_up.reshape(MLA_KV_RANK, MLA_HEADS, MLA_NOPE + MLA_V)
    wk = jnp.pad(kv[:, :, :MLA_NOPE], ((0, 0), (0, 0), (0, LANES - MLA_NOPE)))
    wk = wk.reshape(MLA_KV_RANK, MLA_HEADS * LANES)
    wvt = kv[:, :, MLA_NOPE:].reshape(MLA_KV_RANK, MLA_HEADS * MLA_V).T
    gq = q_norm * (MLA_QK ** -0.5 * LOG2E)
    half = MLA_ROPE // 2
    swap_halves = lambda r: jnp.concatenate([r[half:], r[:half]])
    gqp = pad_rope(swap_halves(gq[MLA_NOPE:]))
    gkp = pad_rope(swap_halves(k_norm[MLA_NOPE:]))
    inv_freq = ROPE_THETA ** (-jnp.arange(half, dtype=F32) / half)
    invf = pad_rope(jnp.concatenate([inv_freq, inv_freq]))
    lanes = lambda v: v.reshape(1, LANES)
    return (w.astype(BF16), wsvt.astype(BF16), wq.astype(BF16), wk.astype(BF16), wvt.astype(BF16),
            _pad_lanes(gq), lanes(gqp), _pad_lanes(k_norm), lanes(gkp), lanes(invf))


def _prep_layer1(w_in, f_bias, q_norm, k_norm):
    wf = w_in[:, 3 * FOX_WIDTH:]
    rep = jnp.repeat(wf, AUG_TERMS, axis=1)
    pad = jnp.zeros((wf.shape[0], FOX_DIM - AUG_TERMS * FOX_HEADS), wf.dtype)
    f_group = jnp.concatenate([rep, pad, rep, pad], axis=1)
    w = jnp.concatenate([w_in[:, :2 * FOX_WIDTH], f_group], axis=1)
    wvt = w_in[:, 2 * FOX_WIDTH:3 * FOX_WIDTH].T
    fb = jnp.repeat(f_bias, AUG_TERMS)
    fpad = jnp.zeros((FOX_DIM - AUG_TERMS * FOX_HEADS,), f_bias.dtype)
    fb = jnp.concatenate([fb, fpad, fb, fpad]).reshape(1, LANES)
    gain2 = lambda gvec: jnp.concatenate([gvec, gvec]).reshape(1, LANES)
    used = AUG_TERMS * FOX_HEADS
    half_lane = jnp.arange(LANES, dtype=jnp.int32) % FOX_DIM
    layout = jnp.where(half_lane < used, half_lane % AUG_TERMS,
                       jnp.where(half_lane < used + AUG_TERMS, AUG_TERMS, AUG_TERMS + 1))
    return (w.astype(BF16), wvt.astype(BF16), fb,
            gain2(q_norm * (FOX_DIM ** -0.5 * LOG2E)), gain2(k_norm), layout.reshape(1, LANES))


def kernel(x, positions, l0_attn_norm, l0_w_in, l0_mla_q_a_norm, l0_mla_w_q_up, l0_mla_kv_a_norm, l0_mla_w_kv_up, l0_mla_q_norm, l0_mla_k_norm, l0_w_o, l0_ffn_norm, l0_w_gate, l0_w_up, l0_w_down, l1_attn_norm, l1_w_in, l1_fox_f_bias, l1_fox_q_norm, l1_fox_k_norm, l1_w_o, l1_ffn_norm, l1_w_gate, l1_w_up, l1_w_down):
    b, s, d = x.shape
    assert d == D_MODEL and s % Q_TILE == 0 and K_TILE == ROW_TILE and Q_TILE % K_TILE == 0
    assert K_TILE % CHAIN == 0 and CHAIN % SB_SUB == 0
    row = lambda v: v.reshape(1, -1)

    w0, wsvt, wq, wk, wvt0, gq0, gqp0, gk0, gkp0, invf = _prep_layer0(
        l0_w_in, l0_mla_w_q_up, l0_mla_w_kv_up, l0_mla_q_norm, l0_mla_k_norm)
    qm, km, vtm, sq, sk, vts = _in0_call(
        x, positions.reshape(b, s, 1), row(l0_attn_norm), w0, wsvt, row(l0_mla_q_a_norm), wq,
        row(l0_mla_kv_a_norm), wk, wvt0, gq0, gqp0, gk0, gkp0, invf)
    o_mla = _softmax_attn_call(qm, km, vtm, "l0_mla_attn")
    o_sb = _sb_attn_call(sq, sk, vts)
    n_mla = MLA_HEADS * MLA_V
    x1 = _out_ffn_call(
        x.reshape(b * s, d),
        [o_mla.reshape(b * s, n_mla), o_sb.reshape(b * s, SB_WIDTH)],
        [l0_w_o[:n_mla].astype(BF16), l0_w_o[n_mla:].astype(BF16)],
        row(l0_ffn_norm), l0_w_gate.astype(BF16), l0_w_up.astype(BF16),
        l0_w_down.astype(BF16), "l0_out_ffn")

    w1, wvt1, fb, gq1, gk1, layout = _prep_layer1(
        l1_w_in, l1_fox_f_bias, l1_fox_q_norm, l1_fox_k_norm)
    qa, ka, vtf = _in1_call(
        x1.reshape(b, s, d), row(l1_attn_norm), w1, wvt1, fb, gq1, gk1, layout)
    o_fox = _softmax_attn_call(qa, ka, vtf, "l1_fox_attn")
    out = _out_ffn_call(
        x1, [o_fox.reshape(b * s, FOX_WIDTH)], [l1_w_o.astype(BF16)],
        row(l1_ffn_norm), l1_w_gate.astype(BF16), l1_w_up.astype(BF16),
        l1_w_down.astype(BF16), "l1_out_ffn")
    return out.reshape(b, s, d)
```

```python
import functools

import jax
import jax.numpy as jnp
from jax import lax
from jax.experimental import pallas as pl
from jax.experimental.pallas import tpu as pltpu

F32 = jnp.float32
BF16 = jnp.bfloat16

D_MODEL = 1024
EPS = 1e-6
NEG_INF = -1e30
ROPE_THETA = 10000.0

MLA_HEADS = 8
MLA_Q_RANK = 256
MLA_KV_RANK = 128
MLA_NOPE = 64
MLA_ROPE = 32
MLA_QK = MLA_NOPE + MLA_ROPE
MLA_V = 64
SB_HEADS = 8
SB_DIM = 64
SB_WIDTH = SB_HEADS * SB_DIM
FOX_HEADS = 16
FOX_DIM = 64
FOX_WIDTH = FOX_HEADS * FOX_DIM

LANES = 128
V7X_VMEM_BYTES = 64 * 1024 * 1024
VMEM_LIMIT = V7X_VMEM_BYTES * 7 // 8

ROW_TILE = 512
K_TILE = 512
Q_TILE = 1024
CHAIN = 256
FFN_TILE = 256
PREFETCH = 2
LOOKAHEAD = 3
ONES_ROWS = 16
SB_SUB = 128
LOG2E = 1.4426950408889634

AUG_TERMS = 3
IN_SUB = 256


def _compiler_params(n_axes):
    return pltpu.CompilerParams(
        dimension_semantics=("arbitrary",) * n_axes,
        vmem_limit_bytes=VMEM_LIMIT,
    )


def _full(shape):
    return pl.BlockSpec(shape, lambda *_: (0,) * len(shape), pipeline_mode=pl.Buffered(1))


def _lane_iota(rows):
    return lax.broadcasted_iota(jnp.int32, (rows, LANES), 1)


def _rms(x, gain, width):
    ss = jnp.sum(x * x, axis=-1, keepdims=True)
    return x * lax.rsqrt(ss * (1.0 / width) + EPS) * gain


def _log_sigmoid(z):
    return jnp.minimum(z, 0.0) - jnp.log(1.0 + jnp.exp(-jnp.abs(z)))


def _split3(x):
    a = x.astype(BF16)
    r = x - a.astype(F32)
    b = r.astype(BF16)
    c = (r - b.astype(F32)).astype(BF16)
    return a, b, c


def _dot(a, b):
    return jnp.dot(a, b, preferred_element_type=F32)


def _dot_nt(a, b):
    return lax.dot_general(a, b, (((1,), (1,)), ((), ())), preferred_element_type=F32)


def _in0_kernel(x_ref, pos_ref, g_ref, w_ref, wsvt_ref, qan_ref, wq_ref, kvan_ref, wk_ref,
                wvt_ref, gq_ref, gqp_ref, gk_ref, gkp_ref, invf_ref,
                qm_ref, km_ref, vtm_ref, sq_ref, sk_ref, vts_ref):
    g = g_ref[...]
    for r in range(ROW_TILE // IN_SUB):
        rows = slice(r * IN_SUB, (r + 1) * IN_SUB)
        h = _rms(x_ref[0, rows, :], g, D_MODEL).astype(BF16)
        proj = _dot(h, w_ref[...])
        vts_ref[0, 0, :, rows] = _dot_nt(wsvt_ref[...], h).astype(BF16)
        o = MLA_Q_RANK + MLA_KV_RANK
        cq = proj[:, 0:MLA_Q_RANK]
        ckv = proj[:, MLA_Q_RANK:o]
        k_rope = proj[:, o:o + LANES]
        k_rope_partner = proj[:, o + LANES:o + 2 * LANES]
        o += 2 * LANES
        sq_ref[0, rows, :] = (proj[:, o:o + SB_WIDTH] * (SB_DIM ** -0.5 * LOG2E)).astype(BF16)
        sk_ref[0, rows, :] = proj[:, o + SB_WIDTH:o + 2 * SB_WIDTH].astype(BF16)

        qn = _rms(cq, qan_ref[...], MLA_Q_RANK).astype(BF16)
        q_all = _dot(qn, wq_ref[...])
        kvn = _rms(ckv, kvan_ref[...], MLA_KV_RANK).astype(BF16)
        k_all = _dot(kvn, wk_ref[...])
        vtm_ref[0, 0, :, rows] = _dot_nt(wvt_ref[...], kvn).astype(BF16)

        ang = pos_ref[0, rows, :].astype(F32) * invf_ref[...]
        cos = jnp.cos(ang)
        sin = jnp.sin(ang)
        gq_cos = gq_ref[...] * cos
        gq_sin = gqp_ref[...] * sin
        gk_cos = gk_ref[...] * cos
        k_partner_term = k_rope_partner * (gkp_ref[...] * sin)

        def inv_rms(t):
            ss = jnp.sum(t * t, axis=-1, keepdims=True)
            return lax.rsqrt(ss * (1.0 / MLA_QK) + EPS)

        for hd in range(MLA_HEADS):
            sl = slice(hd * LANES, (hd + 1) * LANES)
            t = q_all[:, sl]
            t_partner = q_all[:, MLA_HEADS * LANES + hd * LANES:MLA_HEADS * LANES + (hd + 1) * LANES]
            qm_ref[0, hd, rows, :] = (
                (t * gq_cos + t_partner * gq_sin) * inv_rms(t)).astype(BF16)
            tk = k_all[:, sl] + k_rope
            km_ref[0, hd, rows, :] = (
                (tk * gk_cos + k_partner_term) * inv_rms(tk)).astype(BF16)


def _in0_call(x, pos3, g, w, wsvt, qan, wq, kvan, wk, wvt, gq, gqp, gk, gkp, invf):
    b, s, _ = x.shape
    tm = ROW_TILE
    nt = s // tm
    head_spec = pl.BlockSpec((1, MLA_HEADS, tm, LANES), lambda bi, i: (bi, 0, i, 0))
    vt_spec = pl.BlockSpec((1, 1, SB_WIDTH, tm), lambda bi, i: (bi, i, 0, 0))
    row_spec = pl.BlockSpec((1, tm, SB_WIDTH), lambda bi, i: (bi, i, 0))
    lane_vec = _full((1, LANES))
    return pl.pallas_call(
        _in0_kernel,
        grid=(b, nt),
        in_specs=[
            pl.BlockSpec((1, tm, D_MODEL), lambda bi, i: (bi, i, 0)),
            pl.BlockSpec((1, tm, 1), lambda bi, i: (bi, i, 0)),
            _full((1, D_MODEL)),
            _full(w.shape),
            _full(wsvt.shape),
            _full((1, MLA_Q_RANK)),
            _full(wq.shape),
            _full((1, MLA_KV_RANK)),
            _full(wk.shape),
            _full(wvt.shape),
            lane_vec, lane_vec, lane_vec, lane_vec, lane_vec,
        ],
        out_specs=[head_spec, head_spec, vt_spec, row_spec, row_spec, vt_spec],
        out_shape=[
            jax.ShapeDtypeStruct((b, MLA_HEADS, s, LANES), BF16),
            jax.ShapeDtypeStruct((b, MLA_HEADS, s, LANES), BF16),
            jax.ShapeDtypeStruct((b, nt, MLA_HEADS * MLA_V, tm), BF16),
            jax.ShapeDtypeStruct((b, s, SB_WIDTH), BF16),
            jax.ShapeDtypeStruct((b, s, SB_WIDTH), BF16),
            jax.ShapeDtypeStruct((b, nt, SB_WIDTH, tm), BF16),
        ],
        compiler_params=_compiler_params(2),
        name="l0_in",
    )(x, pos3, g, w, wsvt, qan, wq, kvan, wk, wvt, gq, gqp, gk, gkp, invf)


def _diag_spans(n_chains, chain, tk):
    for c in range(n_chains):
        end = (c + 1) * chain
        for d in range(-(-end // tk)):
            nk = min(tk, end - d * tk)
            yield c, d, nk, d * tk + nk > c * chain


def _softmax_attn_kernel(q_ref, k_ref, vt_ref, o_ref, m_scr, acc_scr, pre_scr, *, head_dim):
    tq, tk, cw = Q_TILE, K_TILE, CHAIN
    n_c = tq // cw
    kt_per_q = tq // tk
    qi = pl.program_id(2)
    chains = [(hh, c) for hh in range(2) for c in range(n_c)]
    qs = [q_ref[0, hh, c * cw:(c + 1) * cw, :] for hh, c in chains]

    m_scr[...] = jnp.full(m_scr.shape, NEG_INF, F32)
    acc_scr[...] = jnp.zeros(acc_scr.shape, F32)

    def with_ones(vt):
        return jnp.concatenate([vt, jnp.ones((ONES_ROWS, vt.shape[1]), vt.dtype)], axis=0)

    def update(idx, s, vt, visible):
        if visible is not None:
            s = jnp.where(visible, s, NEG_INF)
        m = m_scr[idx]
        m_new = jnp.maximum(m, jnp.max(s, axis=0, keepdims=True))
        p = jnp.exp2(s - m_new)
        m_scr[idx] = m_new
        acc_scr[idx] = jnp.exp2(m - m_new) * acc_scr[idx] + _dot(vt, p.astype(BF16))

    def scores_of(idx, tile, nk=tk):
        hh = chains[idx][0]
        return _dot_nt(k_ref[0, hh, pl.ds(pl.multiple_of(tile * tk, tk), nk), :], qs[idx])

    n = len(chains)
    for a in range(PREFETCH):
        pre_scr[a] = scores_of(a, 0)

    def full_step(j, _):
        vts = [with_ones(vt_ref[0, j, hh * head_dim:(hh + 1) * head_dim, :]) for hh in range(2)]
        pending = {a: pre_scr[a] for a in range(PREFETCH)}
        for w in range(PREFETCH, LOOKAHEAD):
            pending[w] = scores_of(w, j)
        for v in range(n):
            w = v + LOOKAHEAD
            if w < n:
                pending[w] = scores_of(w, j)
            elif w < n + PREFETCH:
                pre_scr[w - n] = scores_of(w - n, j + 1)
            update(v, pending.pop(v), vts[chains[v][0]], None)
        return 0

    n_full = qi * kt_per_q
    lax.fori_loop(0, n_full, full_step, 0)

    spans = [(hh, c, d, nk, mask) for hh in range(2)
             for c, d, nk, mask in _diag_spans(n_c, cw, tk)]
    def span_scores(span):
        hh, c, d, nk, _ = span
        idx = hh * n_c + c
        if idx < PREFETCH and d == 0:
            return pre_scr[idx, 0:nk, :]
        return scores_of(idx, n_full + d, nk)

    pending = {w: span_scores(spans[w]) for w in range(LOOKAHEAD)}
    for i, (hh, c, d, nk, mask) in enumerate(spans):
        if i + LOOKAHEAD < len(spans):
            pending[i + LOOKAHEAD] = span_scores(spans[i + LOOKAHEAD])
        visible = None
        if mask:
            row = lax.broadcasted_iota(jnp.int32, (nk, cw), 0) + d * tk
            col = lax.broadcasted_iota(jnp.int32, (nk, cw), 1) + c * cw
            visible = row <= col
        vt = vt_ref[0, n_full + d, hh * head_dim:(hh + 1) * head_dim, 0:nk]
        update(hh * n_c + c, pending.pop(i), with_ones(vt), visible)

    def normalized(idx):
        acc = acc_scr[idx]
        return acc[:head_dim] / acc[head_dim:head_dim + 1]

    outs = [jnp.concatenate([normalized(hh * n_c + c) for c in range(n_c)], axis=1)
            for hh in range(2)]
    o_ref[0] = jnp.concatenate(outs, axis=0).T.astype(o_ref.dtype)


def _softmax_attn_call(q, k, vt, name):
    b, h, s, _ = q.shape
    tq = Q_TILE
    nkt = s // K_TILE
    head_dim = vt.shape[2] // h
    n_chains = 2 * tq // CHAIN
    return pl.pallas_call(
        functools.partial(_softmax_attn_kernel, head_dim=head_dim),
        grid=(b, h // 2, s // tq),
        in_specs=[
            pl.BlockSpec((1, 2, tq, LANES), lambda bi, p, i: (bi, p, i, 0)),
            pl.BlockSpec((1, 2, s, LANES), lambda bi, p, i: (bi, p, 0, 0)),
            pl.BlockSpec((1, nkt, 2 * head_dim, K_TILE), lambda bi, p, i: (bi, 0, p, 0)),
        ],
        out_specs=pl.BlockSpec((1, tq, 2 * head_dim), lambda bi, p, i: (bi, i, p)),
        out_shape=jax.ShapeDtypeStruct((b, s, h * head_dim), BF16),
        scratch_shapes=[
            pltpu.VMEM((n_chains, 1, CHAIN), F32),
            pltpu.VMEM((n_chains, head_dim + ONES_ROWS, CHAIN), F32),
            pltpu.VMEM((PREFETCH, K_TILE, CHAIN), F32),
        ],
        compiler_params=_compiler_params(3),
        name=name,
    )(q, k, vt)


def _sb_attn_kernel(q_ref, k_ref, vt_ref, o_ref, run_scr, acc_scr, pre_scr):
    tq, tk, cw, sub = Q_TILE, K_TILE, CHAIN, SB_SUB
    n_c = tq // cw
    kt_per_q = tq // tk
    qi = pl.program_id(2)
    s_row = lax.broadcasted_iota(jnp.int32, (sub, sub), 0)
    s_col = lax.broadcasted_iota(jnp.int32, (sub, sub), 1)
    neg_later = -((s_col > s_row).astype(BF16))
    lane = _lane_iota(cw)
    chains = [(hh, c) for hh in range(2) for c in range(n_c)]
    qs = []
    for hh, c in chains:
        q_pair = q_ref[0, c * cw:(c + 1) * cw, :]
        in_head = (lane >= hh * SB_DIM) & (lane < (hh + 1) * SB_DIM)
        qs.append(jnp.where(in_head, q_pair, jnp.zeros_like(q_pair)))

    run_scr[...] = jnp.zeros(run_scr.shape, F32)
    acc_scr[...] = jnp.zeros(acc_scr.shape, F32)

    sign_bit = jnp.uint32(0x80000000)

    def suffix_sums(z, visible):
        neg_abs = lax.bitcast_convert_type(lax.bitcast_convert_type(z, jnp.uint32) | sign_bit, F32)
        cost = jnp.maximum(z, 0.0) + jnp.log(1.0 + jnp.exp2(neg_abs)) * LOG2E
        log_beta = z - cost
        if visible is not None:
            cost = jnp.where(visible, cost, 0.0)
        cost16 = cost.astype(BF16)
        blocks = [slice(b * sub, (b + 1) * sub) for b in range(z.shape[0] // sub)]
        suffix = [_dot(neg_later, cost16[bl]) for bl in blocks]
        return log_beta, cost, blocks, suffix

    def accumulate(idx, stage, vt, visible):
        log_beta, cost, blocks, suffix = stage
        run = run_scr[idx]
        terms = []
        for bl, sfx in reversed(list(zip(blocks, suffix))):
            terms.append(log_beta[bl] + sfx + run)
            run = run + sfx[0:1, :] - cost[bl.start:bl.start + 1, :]
        a = jnp.exp2(jnp.concatenate(terms[::-1], axis=0))
        if visible is not None:
            a = jnp.where(visible, a, 0.0)
        run_scr[idx] = run
        acc_scr[idx] = acc_scr[idx] + _dot(vt, a.astype(BF16))

    def tile_update(idxs, score_fns, vts, masks, tail_fns=()):
        n = len(idxs)
        fns = list(score_fns) + list(tail_fns)
        pending, stages = {}, {}
        for w in range(min(LOOKAHEAD, len(fns))):
            pending[w] = fns[w]()
        for v in range(n + 1):
            if v + LOOKAHEAD < len(fns):
                pending[v + LOOKAHEAD] = fns[v + LOOKAHEAD]()
            if v < n:
                stages[v] = suffix_sums(pending.pop(v), masks[v])
            if v >= 1:
                accumulate(idxs[v - 1], stages.pop(v - 1), vts[v - 1], masks[v - 1])

    def scores_fn(idx, start, nk):
        return lambda: _dot_nt(k_ref[0, pl.ds(pl.multiple_of(start, tk), nk), :], qs[idx])

    n_full = qi * kt_per_q
    for d in reversed(range(kt_per_q)):
        idxs, fns, vts, masks = [], [], [], []
        for hh in range(2):
            for c, dd, nk, mask in _diag_spans(n_c, cw, tk):
                if dd != d:
                    continue
                visible = None
                if mask:
                    row = lax.broadcasted_iota(jnp.int32, (nk, cw), 0) + d * tk
                    col = lax.broadcasted_iota(jnp.int32, (nk, cw), 1) + c * cw
                    visible = row < col
                idxs.append(hh * n_c + c)
                fns.append(scores_fn(hh * n_c + c, (n_full + d) * tk, nk))
                vts.append(vt_ref[0, n_full + d, hh * SB_DIM:(hh + 1) * SB_DIM, 0:nk])
                masks.append(visible)
        tile_update(idxs, fns, vts, masks)

    all_idx = list(range(len(chains)))
    for a in range(PREFETCH):
        pre_scr[a] = scores_fn(a, jnp.maximum(n_full - 1, 0) * tk, tk)()

    def store_prefetch(a, start):
        def fn():
            pre_scr[a] = scores_fn(a, start, tk)()
        return fn

    def full_step(i, _):
        j = n_full - 1 - i
        vts = [vt_ref[0, j, hh * SB_DIM:(hh + 1) * SB_DIM, :] for hh, _ in chains]
        fns = [(lambda a=a: pre_scr[a]) if a < PREFETCH else scores_fn(a, j * tk, tk)
               for a in all_idx]
        nxt = [store_prefetch(a, jnp.maximum(j - 1, 0) * tk) for a in range(PREFETCH)]
        tile_update(all_idx, fns, vts, [None] * len(chains), nxt)
        return 0

    lax.fori_loop(0, n_full, full_step, 0)
    outs = [jnp.concatenate([acc_scr[hh * n_c + c] for c in range(n_c)], axis=1)
            for hh in range(2)]
    o_ref[0] = jnp.concatenate(outs, axis=0).T.astype(o_ref.dtype)


def _sb_attn_call(q, k, vt):
    b, s, w = q.shape
    tq = Q_TILE
    nkt = s // K_TILE
    n_chains = 2 * tq // CHAIN
    return pl.pallas_call(
        _sb_attn_kernel,
        grid=(b, w // LANES, s // tq),
        in_specs=[
            pl.BlockSpec((1, tq, LANES), lambda bi, p, i: (bi, i, p)),
            pl.BlockSpec((1, s, LANES), lambda bi, p, i: (bi, 0, p)),
            pl.BlockSpec((1, nkt, LANES, K_TILE), lambda bi, p, i: (bi, 0, p, 0)),
        ],
        out_specs=pl.BlockSpec((1, tq, LANES), lambda bi, p, i: (bi, i, p)),
        out_shape=jax.ShapeDtypeStruct((b, s, w), BF16),
        scratch_shapes=[
            pltpu.VMEM((n_chains, 1, CHAIN), F32),
            pltpu.VMEM((n_chains, SB_DIM, CHAIN), F32),
            pltpu.VMEM((PREFETCH, K_TILE, CHAIN), F32),
        ],
        compiler_params=_compiler_params(3),
        name="l0_sb_attn",
    )(q, k, vt)


def _out_ffn_kernel(*refs, n_mix):
    x_ref = refs[0]
    o_refs = refs[1:1 + n_mix]
    wo_refs = refs[1 + n_mix:1 + 2 * n_mix]
    g_ref, wg_ref, wu_ref, wd_ref, y_ref = refs[1 + 2 * n_mix:]
    y = x_ref[...]
    for o_ref, wo_ref in zip(o_refs, wo_refs):
        y = y + _dot(o_ref[...], wo_ref[...])
    h = _rms(y, g_ref[...], D_MODEL).astype(BF16)
    gate = _dot(h, wg_ref[...])
    up = _dot(h, wu_ref[...])
    act = gate * (1.0 / (1.0 + jnp.exp(-gate))) * up
    y_ref[...] = y + _dot(act.astype(BF16), wd_ref[...])


def _out_ffn_call(x2, mixes, wos, g, wg, wu, wd, name):
    n, d = x2.shape
    tm = FFN_TILE
    n_mix = len(mixes)
    row = lambda width: pl.BlockSpec((tm, width), lambda i: (i, 0))
    return pl.pallas_call(
        functools.partial(_out_ffn_kernel, n_mix=n_mix),
        grid=(n // tm,),
        in_specs=([row(d)] + [row(m.shape[1]) for m in mixes]
                  + [_full(w.shape) for w in wos]
                  + [_full((1, d)), _full(wg.shape), _full(wu.shape), _full(wd.shape)]),
        out_specs=row(d),
        out_shape=jax.ShapeDtypeStruct((n, d), F32),
        compiler_params=_compiler_params(1),
        name=name,
    )(x2, *mixes, *wos, g, wg, wu, wd)


def _in1_kernel(x_ref, g_ref, w_ref, wvt_ref, fb_ref, gq_ref, gk_ref, layout_ref,
                qa_ref, ka_ref, vt_ref, carry_ref):
    i = pl.program_id(1)

    @pl.when(i == 0)
    def _():
        carry_ref[...] = jnp.zeros_like(carry_ref)

    n_terms = AUG_TERMS
    used = n_terms * FOX_HEADS
    rows = IN_SUB
    lane = _lane_iota(rows)
    half_lane = lane & (FOX_DIM - 1)
    low_half = lane < FOX_DIM
    term = jnp.broadcast_to(layout_ref[...], (rows, LANES))
    decay_lane = term < n_terms
    r_i = lax.broadcasted_iota(jnp.int32, (rows, rows), 0)
    c_i = lax.broadcasted_iota(jnp.int32, (rows, rows), 1)
    lower = (c_i <= r_i).astype(BF16)
    g = g_ref[...]
    gq = gq_ref[...]
    gk = gk_ref[...]

    for r in range(ROW_TILE // IN_SUB):
        sl_rows = slice(r * rows, (r + 1) * rows)
        h = _rms(x_ref[0, sl_rows, :], g, D_MODEL).astype(BF16)
        proj = _dot(h, w_ref[...])
        vt_ref[0, 0, :, sl_rows] = _dot_nt(wvt_ref[...], h).astype(BF16)
        q = proj[:, 0:FOX_WIDTH]
        k = proj[:, FOX_WIDTH:2 * FOX_WIDTH]
        f_logit = proj[:, 2 * FOX_WIDTH:]

        log_f = jnp.where(decay_lane, _log_sigmoid(f_logit + fb_ref[...]), 0.0)
        f1, f2, f3 = _split3(log_f)
        decay = carry_ref[...] + _dot(lower, f1) + _dot(lower, f2) + _dot(lower, f3)
        carry_ref[...] = decay[rows - 1:rows, :]
        d1, d2, d3 = (d.astype(F32) for d in _split3(decay * LOG2E))
        terms = jnp.where(term == 0, d1, jnp.where(term == 1, d2, d3))
        aug_q = jnp.where(decay_lane, terms, jnp.where(term == n_terms, 1.0, 0.0))
        neg_terms = -terms

        for pair in range(FOX_HEADS // 2):
            sl = slice(pair * LANES, (pair + 1) * LANES)
            qp = q[:, sl]
            kp = k[:, sl]
            qsq = qp * qp
            ksq = kp * kp
            for parity in range(2):
                hd = 2 * pair + parity
                in_head = low_half if parity == 0 else jnp.logical_not(low_half)

                def inv_rms(sq):
                    ss = jnp.sum(jnp.where(in_head, sq, 0.0), axis=-1, keepdims=True)
                    return lax.rsqrt(ss * (1.0 / FOX_DIM) + EPS)

                qa_ref[0, hd, sl_rows, :] = jnp.where(
                    in_head, qp * gq * inv_rms(qsq), aug_q).astype(BF16)
                own = (half_lane >= n_terms * hd) & (half_lane < n_terms * (hd + 1))
                moved = pltpu.roll(neg_terms, used - n_terms * hd, 1)
                aug_k = jnp.where(own, 1.0, jnp.where(term == n_terms, moved, 0.0))
                ka_ref[0, hd, sl_rows, :] = jnp.where(
                    in_head, kp * gk * inv_rms(ksq), aug_k).astype(BF16)


def _in1_call(x, g, w, wvt, fb, gq, gk, layout):
    b, s, _ = x.shape
    tm = ROW_TILE
    nt = s // tm
    head_spec = pl.BlockSpec((1, FOX_HEADS, tm, LANES), lambda bi, i: (bi, 0, i, 0))
    lane_vec = _full((1, LANES))
    return pl.pallas_call(
        _in1_kernel,
        grid=(b, nt),
        in_specs=[
            pl.BlockSpec((1, tm, D_MODEL), lambda bi, i: (bi, i, 0)),
            _full((1, D_MODEL)),
            _full(w.shape),
            _full(wvt.shape),
            lane_vec, lane_vec, lane_vec, lane_vec,
        ],
        out_specs=[head_spec, head_spec,
                   pl.BlockSpec((1, 1, FOX_WIDTH, tm), lambda bi, i: (bi, i, 0, 0))],
        out_shape=[
            jax.ShapeDtypeStruct((b, FOX_HEADS, s, LANES), BF16),
            jax.ShapeDtypeStruct((b, FOX_HEADS, s, LANES), BF16),
            jax.ShapeDtypeStruct((b, nt, FOX_WIDTH, tm), BF16),
        ],
        scratch_shapes=[pltpu.VMEM((1, LANES), F32)],
        compiler_params=_compiler_params(2),
        name="l1_in",
    )(x, g, w, wvt, fb, gq, gk, layout)


def _pad_lanes(v, width=LANES):
    return jnp.pad(v, (0, width - v.shape[0])).reshape(1, width)


def _rope_partner(cols):
    half = MLA_ROPE // 2
    return jnp.concatenate([-cols[..., half:], cols[..., :half]], axis=-1)


def _prep_layer0(w_in, w_q_up, w_kv_up, q_norm, k_norm):
    o = MLA_Q_RANK + MLA_KV_RANK
    pad_rope = lambda c: jnp.pad(c, ((0, 0),) * (c.ndim - 1) + ((MLA_NOPE, LANES - MLA_QK),))
    w_rope = w_in[:, o:o + MLA_ROPE]
    sb = w_in[:, o + MLA_ROPE:]
    w = jnp.concatenate([w_in[:, :o], pad_rope(w_rope), pad_rope(_rope_partner(w_rope)),
                         sb[:, :2 * SB_WIDTH]], axis=1)
    wsvt = sb[:, 2 * SB_WIDTH:].T
    wq3 = w_q_up.reshape(MLA_Q_RANK, MLA_HEADS, MLA_QK)
    wq = jnp.pad(wq3, ((0, 0), (0, 0), (0, LANES - MLA_QK)))
    wq_partner = pad_rope(_rope_partner(wq3[:, :, MLA_NOPE:]))
    wq = jnp.concatenate([wq.reshape(MLA_Q_RANK, -1), wq_partner.reshape(MLA_Q_RANK, -1)], axis=1)
    kv = w_kv_up.reshape(MLA_KV_RANK, MLA_HEADS, MLA_NOPE + MLA_V)
    wk = jnp.pad(kv[:, :, :MLA_NOPE], ((0, 0), (0, 0), (0, LANES - MLA_NOPE)))
    wk = wk.reshape(MLA_KV_RANK, MLA_HEADS * LANES)
    wvt = kv[:, :, MLA_NOPE:].reshape(MLA_KV_RANK, MLA_HEADS * MLA_V).T
    gq = q_norm * (MLA_QK ** -0.5 * LOG2E)
    half = MLA_ROPE // 2
    swap_halves = lambda r: jnp.concatenate([r[half:], r[:half]])
    gqp = pad_rope(swap_halves(gq[MLA_NOPE:]))
    gkp = pad_rope(swap_halves(k_norm[MLA_NOPE:]))
    inv_freq = ROPE_THETA ** (-jnp.arange(half, dtype=F32) / half)
    invf = pad_rope(jnp.concatenate([inv_freq, inv_freq]))
    lanes = lambda v: v.reshape(1, LANES)
    return (w.astype(BF16), wsvt.astype(BF16), wq.astype(BF16), wk.astype(BF16), wvt.astype(BF16),
            _pad_lanes(gq), lanes(gqp), _pad_lanes(k_norm), lanes(gkp), lanes(invf))


def _prep_layer1(w_in, f_bias, q_norm, k_norm):
    wf = w_in[:, 3 * FOX_WIDTH:]
    rep = jnp.repeat(wf, AUG_TERMS, axis=1)
    pad = jnp.zeros((wf.shape[0], FOX_DIM - AUG_TERMS * FOX_HEADS), wf.dtype)
    f_group = jnp.concatenate([rep, pad, rep, pad], axis=1)
    w = jnp.concatenate([w_in[:, :2 * FOX_WIDTH], f_group], axis=1)
    wvt = w_in[:, 2 * FOX_WIDTH:3 * FOX_WIDTH].T
    fb = jnp.repeat(f_bias, AUG_TERMS)
    fpad = jnp.zeros((FOX_DIM - AUG_TERMS * FOX_HEADS,), f_bias.dtype)
    fb = jnp.concatenate([fb, fpad, fb, fpad]).reshape(1, LANES)
    gain2 = lambda gvec: jnp.concatenate([gvec, gvec]).reshape(1, LANES)
    used = AUG_TERMS * FOX_HEADS
    half_lane = jnp.arange(LANES, dtype=jnp.int32) % FOX_DIM
    layout = jnp.where(half_lane < used, half_lane % AUG_TERMS,
                       jnp.where(half_lane < used + AUG_TERMS, AUG_TERMS, AUG_TERMS + 1))
    return (w.astype(BF16), wvt.astype(BF16), fb,
            gain2(q_norm * (FOX_DIM ** -0.5 * LOG2E)), gain2(k_norm), layout.reshape(1, LANES))


def kernel(x, positions, l0_attn_norm, l0_w_in, l0_mla_q_a_norm, l0_mla_w_q_up, l0_mla_kv_a_norm, l0_mla_w_kv_up, l0_mla_q_norm, l0_mla_k_norm, l0_w_o, l0_ffn_norm, l0_w_gate, l0_w_up, l0_w_down, l1_attn_norm, l1_w_in, l1_fox_f_bias, l1_fox_q_norm, l1_fox_k_norm, l1_w_o, l1_ffn_norm, l1_w_gate, l1_w_up, l1_w_down):
    b, s, d = x.shape
    assert d == D_MODEL and s % Q_TILE == 0 and K_TILE == ROW_TILE and Q_TILE % K_TILE == 0
    assert K_TILE % CHAIN == 0 and CHAIN % SB_SUB == 0
    row = lambda v: v.reshape(1, -1)

    w0, wsvt, wq, wk, wvt0, gq0, gqp0, gk0, gkp0, invf = _prep_layer0(
        l0_w_in, l0_mla_w_q_up, l0_mla_w_kv_up, l0_mla_q_norm, l0_mla_k_norm)
    qm, km, vtm, sq, sk, vts = _in0_call(
        x, positions.reshape(b, s, 1), row(l0_attn_norm), w0, wsvt, row(l0_mla_q_a_norm), wq,
        row(l0_mla_kv_a_norm), wk, wvt0, gq0, gqp0, gk0, gkp0, invf)
    o_mla = _softmax_attn_call(qm, km, vtm, "l0_mla_attn")
    o_sb = _sb_attn_call(sq, sk, vts)
    n_mla = MLA_HEADS * MLA_V
    x1 = _out_ffn_call(
        x.reshape(b * s, d),
        [o_mla.reshape(b * s, n_mla), o_sb.reshape(b * s, SB_WIDTH)],
        [l0_w_o[:n_mla].astype(BF16), l0_w_o[n_mla:].astype(BF16)],
        row(l0_ffn_norm), l0_w_gate.astype(BF16), l0_w_up.astype(BF16),
        l0_w_down.astype(BF16), "l0_out_ffn")

    w1, wvt1, fb, gq1, gk1, layout = _prep_layer1(
        l1_w_in, l1_fox_f_bias, l1_fox_q_norm, l1_fox_k_norm)
    qa, ka, vtf = _in1_call(
        x1.reshape(b, s, d), row(l1_attn_norm), w1, wvt1, fb, gq1, gk1, layout)
    o_fox = _softmax_attn_call(qa, ka, vtf, "l1_fox_attn")
    out = _out_ffn_call(
        x1, [o_fox.reshape(b * s, FOX_WIDTH)], [l1_w_o.astype(BF16)],
        row(l1_ffn_norm), l1_w_gate.astype(BF16), l1_w_up.astype(BF16),
        l1_w_down.astype(BF16), "l1_out_ffn")
    return out.reshape(b, s, d)
```

```python
import functools

import jax
import jax.numpy as jnp
from jax import lax
from jax.experimental import pallas as pl
from jax.experimental.pallas import tpu as pltpu

F32 = jnp.float32
BF16 = jnp.bfloat16

D_MODEL = 1024
EPS = 1e-6
NEG_INF = -1e30
ROPE_THETA = 10000.0

MLA_HEADS = 8
MLA_Q_RANK = 256
MLA_KV_RANK = 128
MLA_NOPE = 64
MLA_ROPE = 32
MLA_QK = MLA_NOPE + MLA_ROPE
MLA_V = 64
SB_HEADS = 8
SB_DIM = 64
SB_WIDTH = SB_HEADS * SB_DIM
FOX_HEADS = 16
FOX_DIM = 64
FOX_WIDTH = FOX_HEADS * FOX_DIM

LANES = 128
V7X_VMEM_BYTES = 64 * 1024 * 1024
VMEM_LIMIT = V7X_VMEM_BYTES * 7 // 8

ROW_TILE = 512
K_TILE = 512
Q_TILE = 1024
CHAIN = 256
FFN_TILE = 512
PREFETCH = 2
LOOKAHEAD = 3
ONES_ROWS = 16
SB_SUB = 128
LOG2E = 1.4426950408889634
RUN_FLOOR = -160.0

AUG_TERMS = 3
IN_SUB = 256


def _compiler_params(n_axes):
    return pltpu.CompilerParams(
        dimension_semantics=("arbitrary",) * n_axes,
        vmem_limit_bytes=VMEM_LIMIT,
    )


def _full(shape):
    return pl.BlockSpec(shape, lambda *_: (0,) * len(shape), pipeline_mode=pl.Buffered(1))


def _lane_iota(rows):
    return lax.broadcasted_iota(jnp.int32, (rows, LANES), 1)


def _rms(x, gain, width):
    ss = jnp.sum(x * x, axis=-1, keepdims=True)
    return x * lax.rsqrt(ss * (1.0 / width) + EPS) * gain


def _log_sigmoid(z):
    return jnp.minimum(z, 0.0) - jnp.log(1.0 + jnp.exp(-jnp.abs(z)))


def _split3(x):
    a = x.astype(BF16)
    r = x - a.astype(F32)
    b = r.astype(BF16)
    c = (r - b.astype(F32)).astype(BF16)
    return a, b, c


def _dot(a, b):
    return jnp.dot(a, b, preferred_element_type=F32)


def _dot_nt(a, b):
    return lax.dot_general(a, b, (((1,), (1,)), ((), ())), preferred_element_type=F32)


def _in0_kernel(x_ref, pos_ref, g_ref, w_ref, wsvt_ref, qan_ref, wq_ref, kvan_ref, wk_ref,
                wvt_ref, gq_ref, gqp_ref, gk_ref, gkp_ref, invf_ref,
                qm_ref, km_ref, vtm_ref, sq_ref, sk_ref, vts_ref):
    g = g_ref[...]
    for r in range(ROW_TILE // IN_SUB):
        rows = slice(r * IN_SUB, (r + 1) * IN_SUB)
        h = _rms(x_ref[0, rows, :], g, D_MODEL).astype(BF16)
        proj = _dot(h, w_ref[...])
        vts_ref[0, 0, :, rows] = _dot_nt(wsvt_ref[...], h).astype(BF16)
        o = MLA_Q_RANK + MLA_KV_RANK
        cq = proj[:, 0:MLA_Q_RANK]
        ckv = proj[:, MLA_Q_RANK:o]
        k_rope = proj[:, o:o + LANES]
        k_rope_partner = proj[:, o + LANES:o + 2 * LANES]
        o += 2 * LANES
        sq_ref[0, rows, :] = (proj[:, o:o + SB_WIDTH] * (SB_DIM ** -0.5 * LOG2E)).astype(BF16)
        sk_ref[0, rows, :] = proj[:, o + SB_WIDTH:o + 2 * SB_WIDTH].astype(BF16)

        qn = _rms(cq, qan_ref[...], MLA_Q_RANK).astype(BF16)
        q_all = _dot(qn, wq_ref[...])
        kvn = _rms(ckv, kvan_ref[...], MLA_KV_RANK).astype(BF16)
        k_all = _dot(kvn, wk_ref[...])
        vtm_ref[0, 0, :, rows] = _dot_nt(wvt_ref[...], kvn).astype(BF16)

        ang = pos_ref[0, rows, :].astype(F32) * invf_ref[...]
        cos = jnp.cos(ang)
        sin = jnp.sin(ang)
        gq_cos = gq_ref[...] * cos
        gq_sin = gqp_ref[...] * sin
        gk_cos = gk_ref[...] * cos
        k_partner_term = k_rope_partner * (gkp_ref[...] * sin)

        def inv_rms(t):
            ss = jnp.sum(t * t, axis=-1, keepdims=True)
            return lax.rsqrt(ss * (1.0 / MLA_QK) + EPS)

        for hd in range(MLA_HEADS):
            sl = slice(hd * LANES, (hd + 1) * LANES)
            t = q_all[:, sl]
            t_partner = q_all[:, MLA_HEADS * LANES + hd * LANES:MLA_HEADS * LANES + (hd + 1) * LANES]
            qm_ref[0, hd, rows, :] = (
                (t * gq_cos + t_partner * gq_sin) * inv_rms(t)).astype(BF16)
            tk = k_all[:, sl] + k_rope
            km_ref[0, hd, rows, :] = (
                (tk * gk_cos + k_partner_term) * inv_rms(tk)).astype(BF16)


def _in0_call(x, pos3, g, w, wsvt, qan, wq, kvan, wk, wvt, gq, gqp, gk, gkp, invf):
    b, s, _ = x.shape
    tm = ROW_TILE
    nt = s // tm
    head_spec = pl.BlockSpec((1, MLA_HEADS, tm, LANES), lambda bi, i: (bi, 0, i, 0))
    vt_spec = pl.BlockSpec((1, 1, SB_WIDTH, tm), lambda bi, i: (bi, i, 0, 0))
    row_spec = pl.BlockSpec((1, tm, SB_WIDTH), lambda bi, i: (bi, i, 0))
    lane_vec = _full((1, LANES))
    return pl.pallas_call(
        _in0_kernel,
        grid=(b, nt),
        in_specs=[
            pl.BlockSpec((1, tm, D_MODEL), lambda bi, i: (bi, i, 0)),
            pl.BlockSpec((1, tm, 1), lambda bi, i: (bi, i, 0)),
            _full((1, D_MODEL)),
            _full(w.shape),
            _full(wsvt.shape),
            _full((1, MLA_Q_RANK)),
            _full(wq.shape),
            _full((1, MLA_KV_RANK)),
            _full(wk.shape),
            _full(wvt.shape),
            lane_vec, lane_vec, lane_vec, lane_vec, lane_vec,
        ],
        out_specs=[head_spec, head_spec, vt_spec, row_spec, row_spec, vt_spec],
        out_shape=[
            jax.ShapeDtypeStruct((b, MLA_HEADS, s, LANES), BF16),
            jax.ShapeDtypeStruct((b, MLA_HEADS, s, LANES), BF16),
            jax.ShapeDtypeStruct((b, nt, MLA_HEADS * MLA_V, tm), BF16),
            jax.ShapeDtypeStruct((b, s, SB_WIDTH), BF16),
            jax.ShapeDtypeStruct((b, s, SB_WIDTH), BF16),
            jax.ShapeDtypeStruct((b, nt, SB_WIDTH, tm), BF16),
        ],
        compiler_params=_compiler_params(2),
        name="l0_in",
    )(x, pos3, g, w, wsvt, qan, wq, kvan, wk, wvt, gq, gqp, gk, gkp, invf)


def _diag_spans(n_chains, chain, tk):
    for c in range(n_chains):
        end = (c + 1) * chain
        for d in range(-(-end // tk)):
            nk = min(tk, end - d * tk)
            yield c, d, nk, d * tk + nk > c * chain


def _softmax_attn_kernel(q_ref, k_ref, vt_ref, o_ref, m_scr, acc_scr, pre_scr, *, head_dim):
    tq, tk, cw = Q_TILE, K_TILE, CHAIN
    n_c = tq // cw
    kt_per_q = tq // tk
    qi = pl.program_id(2)
    chains = [(hh, c) for hh in range(2) for c in range(n_c)]
    qs = [q_ref[0, hh, c * cw:(c + 1) * cw, :] for hh, c in chains]

    m_scr[...] = jnp.full(m_scr.shape, NEG_INF, F32)
    acc_scr[...] = jnp.zeros(acc_scr.shape, F32)

    def with_ones(vt):
        return jnp.concatenate([vt, jnp.ones((ONES_ROWS, vt.shape[1]), vt.dtype)], axis=0)

    def update(idx, s, vt, visible):
        if visible is not None:
            s = jnp.where(visible, s, NEG_INF)
        m = m_scr[idx]
        m_new = jnp.maximum(m, jnp.max(s, axis=0, keepdims=True))
        p = jnp.exp2(s - m_new)
        m_scr[idx] = m_new
        acc_scr[idx] = jnp.exp2(m - m_new) * acc_scr[idx] + _dot(vt, p.astype(BF16))

    def scores_of(idx, tile, nk=tk):
        hh = chains[idx][0]
        return _dot_nt(k_ref[0, hh, pl.ds(pl.multiple_of(tile * tk, tk), nk), :], qs[idx])

    def run_pipeline(items, tail_fns):
        fns = [score_fn for score_fn, _ in items] + list(tail_fns)
        pending = {w: fns[w]() for w in range(min(LOOKAHEAD, len(fns)))}
        for v, (_, consume) in enumerate(items):
            if v + LOOKAHEAD < len(fns):
                pending[v + LOOKAHEAD] = fns[v + LOOKAHEAD]()
            consume(pending.pop(v))

    def prefetch_fns(tile):
        def store(a):
            def fn():
                pre_scr[a] = scores_of(a, tile)
            return fn
        return [store(a) for a in range(PREFETCH)]

    n_full = qi * kt_per_q
    items = []
    for hh in range(2):
        for c, d, nk, mask in _diag_spans(n_c, cw, tk):
            idx = hh * n_c + c
            visible = None
            if mask:
                row = lax.broadcasted_iota(jnp.int32, (nk, cw), 0) + d * tk
                col = lax.broadcasted_iota(jnp.int32, (nk, cw), 1) + c * cw
                visible = row <= col

            def score_fn(idx=idx, d=d, nk=nk):
                return scores_of(idx, n_full + d, nk)

            def consume(s, idx=idx, hh=hh, d=d, nk=nk, visible=visible):
                vt = vt_ref[0, n_full + d, hh * head_dim:(hh + 1) * head_dim, 0:nk]
                update(idx, s, with_ones(vt), visible)

            items.append((score_fn, consume))
    run_pipeline(items, prefetch_fns(0))

    def full_step(j, _):
        vts = [with_ones(vt_ref[0, j, hh * head_dim:(hh + 1) * head_dim, :]) for hh in range(2)]
        items = []
        for idx, (hh, _) in enumerate(chains):
            if idx < PREFETCH:
                score_fn = lambda idx=idx: pre_scr[idx]
            else:
                score_fn = lambda idx=idx: scores_of(idx, j)
            items.append((score_fn, lambda s, idx=idx, hh=hh: update(idx, s, vts[hh], None)))
        run_pipeline(items, prefetch_fns(jnp.minimum(j + 1, n_full - 1)))
        return 0

    lax.fori_loop(0, n_full, full_step, 0)

    def normalized(idx):
        acc = acc_scr[idx]
        return acc[:head_dim] / acc[head_dim:head_dim + 1]

    outs = [jnp.concatenate([normalized(hh * n_c + c) for c in range(n_c)], axis=1)
            for hh in range(2)]
    o_ref[0] = jnp.concatenate(outs, axis=0).T.astype(o_ref.dtype)


def _softmax_attn_call(q, k, vt, name):
    b, h, s, _ = q.shape
    tq = Q_TILE
    nkt = s // K_TILE
    head_dim = vt.shape[2] // h
    n_chains = 2 * tq // CHAIN
    return pl.pallas_call(
        functools.partial(_softmax_attn_kernel, head_dim=head_dim),
        grid=(b, h // 2, s // tq),
        in_specs=[
            pl.BlockSpec((1, 2, tq, LANES), lambda bi, p, i: (bi, p, i, 0)),
            pl.BlockSpec((1, 2, s, LANES), lambda bi, p, i: (bi, p, 0, 0)),
            pl.BlockSpec((1, nkt, 2 * head_dim, K_TILE), lambda bi, p, i: (bi, 0, p, 0)),
        ],
        out_specs=pl.BlockSpec((1, tq, 2 * head_dim), lambda bi, p, i: (bi, i, p)),
        out_shape=jax.ShapeDtypeStruct((b, s, h * head_dim), BF16),
        scratch_shapes=[
            pltpu.VMEM((n_chains, 1, CHAIN), F32),
            pltpu.VMEM((n_chains, head_dim + ONES_ROWS, CHAIN), F32),
            pltpu.VMEM((PREFETCH, K_TILE, CHAIN), F32),
        ],
        compiler_params=_compiler_params(3),
        name=name,
    )(q, k, vt)


def _sb_attn_kernel(q_ref, k_ref, vt_ref, o_ref, run_scr, acc_scr, pre_scr):
    tq, tk, cw, sub = Q_TILE, K_TILE, CHAIN, SB_SUB
    n_c = tq // cw
    kt_per_q = tq // tk
    qi = pl.program_id(2)
    s_row = lax.broadcasted_iota(jnp.int32, (sub, sub), 0)
    s_col = lax.broadcasted_iota(jnp.int32, (sub, sub), 1)
    neg_later = -((s_col > s_row).astype(BF16))
    lane = _lane_iota(cw)
    chains = [(hh, c) for hh in range(2) for c in range(n_c)]
    qs = []
    for hh, c in chains:
        q_pair = q_ref[0, c * cw:(c + 1) * cw, :]
        in_head = (lane >= hh * SB_DIM) & (lane < (hh + 1) * SB_DIM)
        qs.append(jnp.where(in_head, q_pair, jnp.zeros_like(q_pair)))

    run_scr[...] = jnp.zeros(run_scr.shape, F32)
    acc_scr[...] = jnp.zeros(acc_scr.shape, F32)

    def suffix_sums(z, visible):
        cost = jnp.maximum(z, 0.0) + jnp.log(1.0 + jnp.exp2(-jnp.abs(z))) * LOG2E
        log_beta = z - cost
        if visible is not None:
            cost = jnp.where(visible, cost, 0.0)
        cost16 = cost.astype(BF16)
        blocks = [slice(b * sub, (b + 1) * sub) for b in range(z.shape[0] // sub)]
        suffix = [_dot(neg_later, cost16[bl]) for bl in blocks]
        return log_beta, cost, blocks, suffix

    def accumulate(idx, stage, vt, visible):
        log_beta, cost, blocks, suffix = stage
        run = run_scr[idx]
        terms = []
        for bl, sfx in reversed(list(zip(blocks, suffix))):
            terms.append(log_beta[bl] + sfx + run)
            run = run + sfx[0:1, :] - cost[bl.start:bl.start + 1, :]
        a = jnp.exp2(jnp.concatenate(terms[::-1], axis=0))
        if visible is not None:
            a = jnp.where(visible, a, 0.0)
        run_scr[idx] = run
        acc_scr[idx] = acc_scr[idx] + _dot(vt, a.astype(BF16))

    def tile_update(idxs, score_fns, vts, masks, tail_fns=()):
        n = len(idxs)
        fns = list(score_fns) + list(tail_fns)
        pending, stages = {}, {}
        for w in range(min(LOOKAHEAD, len(fns))):
            pending[w] = fns[w]()
        for v in range(n + 1):
            if v + LOOKAHEAD < len(fns):
                pending[v + LOOKAHEAD] = fns[v + LOOKAHEAD]()
            if v < n:
                stages[v] = suffix_sums(pending.pop(v), masks[v])
            if v >= 1:
                accumulate(idxs[v - 1], stages.pop(v - 1), vts[v - 1], masks[v - 1])

    def scores_fn(idx, start, nk):
        return lambda: _dot_nt(k_ref[0, pl.ds(pl.multiple_of(start, tk), nk), :], qs[idx])

    n_full = qi * kt_per_q
    for d in reversed(range(kt_per_q)):
        idxs, fns, vts, masks = [], [], [], []
        for hh in range(2):
            for c, dd, nk, mask in _diag_spans(n_c, cw, tk):
                if dd != d:
                    continue
                visible = None
                if mask:
                    row = lax.broadcasted_iota(jnp.int32, (nk, cw), 0) + d * tk
                    col = lax.broadcasted_iota(jnp.int32, (nk, cw), 1) + c * cw
                    visible = row < col
                idxs.append(hh * n_c + c)
                fns.append(scores_fn(hh * n_c + c, (n_full + d) * tk, nk))
                vts.append(vt_ref[0, n_full + d, hh * SB_DIM:(hh + 1) * SB_DIM, 0:nk])
                masks.append(visible)
        tile_update(idxs, fns, vts, masks)

    all_idx = list(range(len(chains)))

    def still_alive():
        return jnp.max(run_scr[...]) > RUN_FLOOR

    alive = jnp.logical_and(n_full > 0, still_alive())

    @pl.when(alive)
    def _():
        for a in range(PREFETCH):
            pre_scr[a] = scores_fn(a, jnp.maximum(n_full - 1, 0) * tk, tk)()

    def store_prefetch(a, start):
        def fn():
            pre_scr[a] = scores_fn(a, start, tk)()
        return fn

    def full_step(i, _):
        j = n_full - 1 - i
        vts = [vt_ref[0, j, hh * SB_DIM:(hh + 1) * SB_DIM, :] for hh, _ in chains]
        fns = [(lambda a=a: pre_scr[a]) if a < PREFETCH else scores_fn(a, j * tk, tk)
               for a in all_idx]
        nxt = [store_prefetch(a, jnp.maximum(j - 1, 0) * tk) for a in range(PREFETCH)]
        tile_update(all_idx, fns, vts, [None] * len(chains), nxt)

    def step_and_test(carry):
        i, _ = carry
        full_step(i, None)
        return i + 1, still_alive()

    lax.while_loop(lambda carry: jnp.logical_and(carry[0] < n_full, carry[1]),
                   step_and_test, (jnp.int32(0), alive))
    outs = [jnp.concatenate([acc_scr[hh * n_c + c] for c in range(n_c)], axis=1)
            for hh in range(2)]
    o_ref[0] = jnp.concatenate(outs, axis=0).T.astype(o_ref.dtype)


def _sb_attn_call(q, k, vt):
    b, s, w = q.shape
    tq = Q_TILE
    nkt = s // K_TILE
    n_chains = 2 * tq // CHAIN
    return pl.pallas_call(
        _sb_attn_kernel,
        grid=(b, w // LANES, s // tq),
        in_specs=[
            pl.BlockSpec((1, tq, LANES), lambda bi, p, i: (bi, i, p)),
            pl.BlockSpec((1, s, LANES), lambda bi, p, i: (bi, 0, p)),
            pl.BlockSpec((1, nkt, LANES, K_TILE), lambda bi, p, i: (bi, 0, p, 0)),
        ],
        out_specs=pl.BlockSpec((1, tq, LANES), lambda bi, p, i: (bi, i, p)),
        out_shape=jax.ShapeDtypeStruct((b, s, w), BF16),
        scratch_shapes=[
            pltpu.VMEM((n_chains, 1, CHAIN), F32),
            pltpu.VMEM((n_chains, SB_DIM, CHAIN), F32),
            pltpu.VMEM((PREFETCH, K_TILE, CHAIN), F32),
        ],
        compiler_params=_compiler_params(3),
        name="l0_sb_attn",
    )(q, k, vt)


def _out_ffn_kernel(*refs, n_mix):
    x_ref = refs[0]
    o_refs = refs[1:1 + n_mix]
    wo_refs = refs[1 + n_mix:1 + 2 * n_mix]
    g_ref, wg_ref, wu_ref, wd_ref, y_ref = refs[1 + 2 * n_mix:]
    y = x_ref[...]
    for o_ref, wo_ref in zip(o_refs, wo_refs):
        y = y + _dot(o_ref[...], wo_ref[...])
    h = _rms(y, g_ref[...], D_MODEL).astype(BF16)
    gate = _dot(h, wg_ref[...])
    up = _dot(h, wu_ref[...])
    act = gate * (1.0 / (1.0 + jnp.exp(-gate))) * up
    y_ref[...] = y + _dot(act.astype(BF16), wd_ref[...])


def _out_ffn_call(x2, mixes, wos, g, wg, wu, wd, name):
    n, d = x2.shape
    tm = FFN_TILE
    n_mix = len(mixes)
    row = lambda width: pl.BlockSpec((tm, width), lambda i: (i, 0))
    return pl.pallas_call(
        functools.partial(_out_ffn_kernel, n_mix=n_mix),
        grid=(n // tm,),
        in_specs=([row(d)] + [row(m.shape[1]) for m in mixes]
                  + [_full(w.shape) for w in wos]
                  + [_full((1, d)), _full(wg.shape), _full(wu.shape), _full(wd.shape)]),
        out_specs=row(d),
        out_shape=jax.ShapeDtypeStruct((n, d), F32),
        compiler_params=_compiler_params(1),
        name=name,
    )(x2, *mixes, *wos, g, wg, wu, wd)


def _in1_kernel(x_ref, g_ref, w_ref, wvt_ref, fb_ref, gq_ref, gk_ref, layout_ref,
                qa_ref, ka_ref, vt_ref, carry_ref):
    i = pl.program_id(1)

    @pl.when(i == 0)
    def _():
        carry_ref[...] = jnp.zeros_like(carry_ref)

    n_terms = AUG_TERMS
    used = n_terms * FOX_HEADS
    rows = IN_SUB
    lane = _lane_iota(rows)
    half_lane = lane & (FOX_DIM - 1)
    low_half = lane < FOX_DIM
    term = jnp.broadcast_to(layout_ref[...], (rows, LANES))
    decay_lane = term < n_terms
    r_i = lax.broadcasted_iota(jnp.int32, (rows, rows), 0)
    c_i = lax.broadcasted_iota(jnp.int32, (rows, rows), 1)
    lower = (c_i <= r_i).astype(BF16)
    g = g_ref[...]
    gq = gq_ref[...]
    gk = gk_ref[...]

    for r in range(ROW_TILE // IN_SUB):
        sl_rows = slice(r * rows, (r + 1) * rows)
        h = _rms(x_ref[0, sl_rows, :], g, D_MODEL).astype(BF16)
        proj = _dot(h, w_ref[...])
        vt_ref[0, 0, :, sl_rows] = _dot_nt(wvt_ref[...], h).astype(BF16)
        q = proj[:, 0:FOX_WIDTH]
        k = proj[:, FOX_WIDTH:2 * FOX_WIDTH]
        f_logit = proj[:, 2 * FOX_WIDTH:]

        log_f = jnp.where(decay_lane, _log_sigmoid(f_logit + fb_ref[...]), 0.0)
        f1, f2, f3 = _split3(log_f)
        decay = carry_ref[...] + _dot(lower, f1) + _dot(lower, f2) + _dot(lower, f3)
        carry_ref[...] = decay[rows - 1:rows, :]
        d1, d2, d3 = (d.astype(F32) for d in _split3(decay * LOG2E))
        terms = jnp.where(term == 0, d1, jnp.where(term == 1, d2, d3))
        aug_q = jnp.where(decay_lane, terms, jnp.where(term == n_terms, 1.0, 0.0))
        neg_terms = -terms

        for pair in range(FOX_HEADS // 2):
            sl = slice(pair * LANES, (pair + 1) * LANES)
            qp = q[:, sl]
            kp = k[:, sl]
            qsq = qp * qp
            ksq = kp * kp
            for parity in range(2):
                hd = 2 * pair + parity
                in_head = low_half if parity == 0 else jnp.logical_not(low_half)

                def inv_rms(sq):
                    ss = jnp.sum(jnp.where(in_head, sq, 0.0), axis=-1, keepdims=True)
                    return lax.rsqrt(ss * (1.0 / FOX_DIM) + EPS)

                qa_ref[0, hd, sl_rows, :] = jnp.where(
                    in_head, qp * gq * inv_rms(qsq), aug_q).astype(BF16)
                own = (half_lane >= n_terms * hd) & (half_lane < n_terms * (hd + 1))
                moved = pltpu.roll(neg_terms, used - n_terms * hd, 1)
                aug_k = jnp.where(own, 1.0, jnp.where(term == n_terms, moved, 0.0))
                ka_ref[0, hd, sl_rows, :] = jnp.where(
                    in_head, kp * gk * inv_rms(ksq), aug_k).astype(BF16)


def _in1_call(x, g, w, wvt, fb, gq, gk, layout):
    b, s, _ = x.shape
    tm = ROW_TILE
    nt = s // tm
    head_spec = pl.BlockSpec((1, FOX_HEADS, tm, LANES), lambda bi, i: (bi, 0, i, 0))
    lane_vec = _full((1, LANES))
    return pl.pallas_call(
        _in1_kernel,
        grid=(b, nt),
        in_specs=[
            pl.BlockSpec((1, tm, D_MODEL), lambda bi, i: (bi, i, 0)),
            _full((1, D_MODEL)),
            _full(w.shape),
            _full(wvt.shape),
            lane_vec, lane_vec, lane_vec, lane_vec,
        ],
        out_specs=[head_spec, head_spec,
                   pl.BlockSpec((1, 1, FOX_WIDTH, tm), lambda bi, i: (bi, i, 0, 0))],
        out_shape=[
            jax.ShapeDtypeStruct((b, FOX_HEADS, s, LANES), BF16),
            jax.ShapeDtypeStruct((b, FOX_HEADS, s, LANES), BF16),
            jax.ShapeDtypeStruct((b, nt, FOX_WIDTH, tm), BF16),
        ],
        scratch_shapes=[pltpu.VMEM((1, LANES), F32)],
        compiler_params=_compiler_params(2),
        name="l1_in",
    )(x, g, w, wvt, fb, gq, gk, layout)


def _pad_lanes(v, width=LANES):
    return jnp.pad(v, (0, width - v.shape[0])).reshape(1, width)


def _rope_partner(cols):
    half = MLA_ROPE // 2
    return jnp.concatenate([-cols[..., half:], cols[..., :half]], axis=-1)


def _prep_layer0(w_in, w_q_up, w_kv_up, q_norm, k_norm):
    o = MLA_Q_RANK + MLA_KV_RANK
    pad_rope = lambda c: jnp.pad(c, ((0, 0),) * (c.ndim - 1) + ((MLA_NOPE, LANES - MLA_QK),))
    w_rope = w_in[:, o:o + MLA_ROPE]
    sb = w_in[:, o + MLA_ROPE:]
    w = jnp.concatenate([w_in[:, :o], pad_rope(w_rope), pad_rope(_rope_partner(w_rope)),
                         sb[:, :2 * SB_WIDTH]], axis=1)
    wsvt = sb[:, 2 * SB_WIDTH:].T
    wq3 = w_q_up.reshape(MLA_Q_RANK, MLA_HEADS, MLA_QK)
    wq = jnp.pad(wq3, ((0, 0), (0, 0), (0, LANES - MLA_QK)))
    wq_partner = pad_rope(_rope_partner(wq3[:, :, MLA_NOPE:]))
    wq = jnp.concatenate([wq.reshape(MLA_Q_RANK, -1), wq_partner.reshape(MLA_Q_RANK, -1)], axis=1)
    kv = w_kv_up.reshape(MLA_KV_RANK, MLA_HEADS, MLA_NOPE + MLA_V)
    wk = jnp.pad(kv[:, :, :MLA_NOPE], ((0, 0), (0, 0), (0, LANES - MLA_NOPE)))
    wk = wk.reshape(MLA_KV_RANK, MLA_HEADS * LANES)
    wvt = kv[:, :, MLA_NOPE:].reshape(MLA_KV_RANK, MLA_HEADS * MLA_V).T
    gq = q_norm * (MLA_QK ** -0.5 * LOG2E)
    half = MLA_ROPE // 2
    swap_halves = lambda r: jnp.concatenate([r[half:], r[:half]])
    gqp = pad_rope(swap_halves(gq[MLA_NOPE:]))
    gkp = pad_rope(swap_halves(k_norm[MLA_NOPE:]))
    inv_freq = ROPE_THETA ** (-jnp.arange(half, dtype=F32) / half)
    invf = pad_rope(jnp.concatenate([inv_freq, inv_freq]))
    lanes = lambda v: v.reshape(1, LANES)
    return (w.astype(BF16), wsvt.astype(BF16), wq.astype(BF16), wk.astype(BF16), wvt.astype(BF16),
            _pad_lanes(gq), lanes(gqp), _pad_lanes(k_norm), lanes(gkp), lanes(invf))


def _prep_layer1(w_in, f_bias, q_norm, k_norm):
    wf = w_in[:, 3 * FOX_WIDTH:]
    rep = jnp.repeat(wf, AUG_TERMS, axis=1)
    pad = jnp.zeros((wf.shape[0], FOX_DIM - AUG_TERMS * FOX_HEADS), wf.dtype)
    f_group = jnp.concatenate([rep, pad, rep, pad], axis=1)
    w = jnp.concatenate([w_in[:, :2 * FOX_WIDTH], f_group], axis=1)
    wvt = w_in[:, 2 * FOX_WIDTH:3 * FOX_WIDTH].T
    fb = jnp.repeat(f_bias, AUG_TERMS)
    fpad = jnp.zeros((FOX_DIM - AUG_TERMS * FOX_HEADS,), f_bias.dtype)
    fb = jnp.concatenate([fb, fpad, fb, fpad]).reshape(1, LANES)
    gain2 = lambda gvec: jnp.concatenate([gvec, gvec]).reshape(1, LANES)
    used = AUG_TERMS * FOX_HEADS
    half_lane = jnp.arange(LANES, dtype=jnp.int32) % FOX_DIM
    layout = jnp.where(half_lane < used, half_lane % AUG_TERMS,
                       jnp.where(half_lane < used + AUG_TERMS, AUG_TERMS, AUG_TERMS + 1))
    return (w.astype(BF16), wvt.astype(BF16), fb,
            gain2(q_norm * (FOX_DIM ** -0.5 * LOG2E)), gain2(k_norm), layout.reshape(1, LANES))


def kernel(x, positions, l0_attn_norm, l0_w_in, l0_mla_q_a_norm, l0_mla_w_q_up, l0_mla_kv_a_norm, l0_mla_w_kv_up, l0_mla_q_norm, l0_mla_k_norm, l0_w_o, l0_ffn_norm, l0_w_gate, l0_w_up, l0_w_down, l1_attn_norm, l1_w_in, l1_fox_f_bias, l1_fox_q_norm, l1_fox_k_norm, l1_w_o, l1_ffn_norm, l1_w_gate, l1_w_up, l1_w_down):
    b, s, d = x.shape
    assert d == D_MODEL and s % Q_TILE == 0 and K_TILE == ROW_TILE and Q_TILE % K_TILE == 0
    assert K_TILE % CHAIN == 0 and CHAIN % SB_SUB == 0
    row = lambda v: v.reshape(1, -1)

    w0, wsvt, wq, wk, wvt0, gq0, gqp0, gk0, gkp0, invf = _prep_layer0(
        l0_w_in, l0_mla_w_q_up, l0_mla_w_kv_up, l0_mla_q_norm, l0_mla_k_norm)
    qm, km, vtm, sq, sk, vts = _in0_call(
        x, positions.reshape(b, s, 1), row(l0_attn_norm), w0, wsvt, row(l0_mla_q_a_norm), wq,
        row(l0_mla_kv_a_norm), wk, wvt0, gq0, gqp0, gk0, gkp0, invf)
    o_mla = _softmax_attn_call(qm, km, vtm, "l0_mla_attn")
    o_sb = _sb_attn_call(sq, sk, vts)
    n_mla = MLA_HEADS * MLA_V
    x1 = _out_ffn_call(
        x.reshape(b * s, d),
        [o_mla.reshape(b * s, n_mla), o_sb.reshape(b * s, SB_WIDTH)],
        [l0_w_o[:n_mla].astype(BF16), l0_w_o[n_mla:].astype(BF16)],
        row(l0_ffn_norm), l0_w_gate.astype(BF16), l0_w_up.astype(BF16),
        l0_w_down.astype(BF16), "l0_out_ffn")

    w1, wvt1, fb, gq1, gk1, layout = _prep_layer1(
        l1_w_in, l1_fox_f_bias, l1_fox_q_norm, l1_fox_k_norm)
    qa, ka, vtf = _in1_call(
        x1.reshape(b, s, d), row(l1_attn_norm), w1, wvt1, fb, gq1, gk1, layout)
    o_fox = _softmax_attn_call(qa, ka, vtf, "l1_fox_attn")
    out = _out_ffn_call(
        x1, [o_fox.reshape(b * s, FOX_WIDTH)], [l1_w_o.astype(BF16)],
        row(l1_ffn_norm), l1_w_gate.astype(BF16), l1_w_up.astype(BF16),
        l1_w_down.astype(BF16), "l1_out_ffn")
    return out.reshape(b, s, d)
```

```python
import functools

import jax
import jax.numpy as jnp
from jax import lax
from jax.experimental import pallas as pl
from jax.experimental.pallas import tpu as pltpu

F32 = jnp.float32
BF16 = jnp.bfloat16

D_MODEL = 1024
EPS = 1e-6
NEG_INF = -1e30
ROPE_THETA = 10000.0

MLA_HEADS = 8
MLA_Q_RANK = 256
MLA_KV_RANK = 128
MLA_NOPE = 64
MLA_ROPE = 32
MLA_QK = MLA_NOPE + MLA_ROPE
MLA_V = 64
SB_HEADS = 8
SB_DIM = 64
SB_WIDTH = SB_HEADS * SB_DIM
FOX_HEADS = 16
FOX_DIM = 64
FOX_WIDTH = FOX_HEADS * FOX_DIM

LANES = 128
V7X_VMEM_BYTES = 64 * 1024 * 1024
VMEM_LIMIT = V7X_VMEM_BYTES * 7 // 8

ROW_TILE = 512
K_TILE = 512
Q_TILE = 1024
CHAIN = 256
FFN_TILE = 512
PREFETCH = 2
LOOKAHEAD = 3
DIAG_LOOKAHEAD = 4
ONES_ROWS = 16
SB_SUB = 128
LOG2E = 1.4426950408889634
RUN_FLOOR = -160.0

AUG_TERMS = 3
IN_SUB = 256


def _compiler_params(n_axes):
    return pltpu.CompilerParams(
        dimension_semantics=("arbitrary",) * n_axes,
        vmem_limit_bytes=VMEM_LIMIT,
    )


def _full(shape):
    return pl.BlockSpec(shape, lambda *_: (0,) * len(shape), pipeline_mode=pl.Buffered(1))


def _lane_iota(rows):
    return lax.broadcasted_iota(jnp.int32, (rows, LANES), 1)


def _rms(x, gain, width):
    ss = jnp.sum(x * x, axis=-1, keepdims=True)
    return x * lax.rsqrt(ss * (1.0 / width) + EPS) * gain


def _log_sigmoid(z):
    return jnp.minimum(z, 0.0) - jnp.log(1.0 + jnp.exp(-jnp.abs(z)))


def _split3(x):
    a = x.astype(BF16)
    r = x - a.astype(F32)
    b = r.astype(BF16)
    c = (r - b.astype(F32)).astype(BF16)
    return a, b, c


def _dot(a, b):
    return jnp.dot(a, b, preferred_element_type=F32)


def _dot_nt(a, b):
    return lax.dot_general(a, b, (((1,), (1,)), ((), ())), preferred_element_type=F32)


def _in0_kernel(x_ref, pos_ref, g_ref, w_ref, wsvt_ref, qan_ref, wq_ref, kvan_ref, wk_ref,
                wvt_ref, gq_ref, gqp_ref, gk_ref, gkp_ref, invf_ref,
                qm_ref, km_ref, vtm_ref, sq_ref, sk_ref, vts_ref):
    g = g_ref[...]
    deferred = []
    for r in range(ROW_TILE // IN_SUB):
        rows = slice(r * IN_SUB, (r + 1) * IN_SUB)
        h = _rms(x_ref[0, rows, :], g, D_MODEL).astype(BF16)
        proj = _dot(h, w_ref[...])
        o = MLA_Q_RANK + MLA_KV_RANK
        cq = proj[:, 0:MLA_Q_RANK]
        ckv = proj[:, MLA_Q_RANK:o]
        k_rope = proj[:, o:o + LANES]
        k_rope_partner = proj[:, o + LANES:o + 2 * LANES]
        o += 2 * LANES
        sq_ref[0, rows, :] = (proj[:, o:o + SB_WIDTH] * (SB_DIM ** -0.5 * LOG2E)).astype(BF16)
        sk_ref[0, rows, :] = proj[:, o + SB_WIDTH:o + 2 * SB_WIDTH].astype(BF16)

        qn = _rms(cq, qan_ref[...], MLA_Q_RANK).astype(BF16)
        q_all = _dot(qn, wq_ref[...])
        kvn = _rms(ckv, kvan_ref[...], MLA_KV_RANK).astype(BF16)
        k_all = _dot(kvn, wk_ref[...])
        deferred.append((rows, h, kvn))

        ang = pos_ref[0, rows, :].astype(F32) * invf_ref[...]
        cos = jnp.cos(ang)
        sin = jnp.sin(ang)
        gq_cos = gq_ref[...] * cos
        gq_sin = gqp_ref[...] * sin
        gk_cos = gk_ref[...] * cos
        k_partner_term = k_rope_partner * (gkp_ref[...] * sin)

        def inv_rms(t):
            ss = jnp.sum(t * t, axis=-1, keepdims=True)
            return lax.rsqrt(ss * (1.0 / MLA_QK) + EPS)

        for hd in range(MLA_HEADS):
            sl = slice(hd * LANES, (hd + 1) * LANES)
            t = q_all[:, sl]
            t_partner = q_all[:, MLA_HEADS * LANES + hd * LANES:MLA_HEADS * LANES + (hd + 1) * LANES]
            qm_ref[0, hd, rows, :] = (
                (t * gq_cos + t_partner * gq_sin) * inv_rms(t)).astype(BF16)
            tk = k_all[:, sl] + k_rope
            km_ref[0, hd, rows, :] = (
                (tk * gk_cos + k_partner_term) * inv_rms(tk)).astype(BF16)

    for rows, h, kvn in deferred:
        vts_ref[0, 0, :, rows] = _dot_nt(wsvt_ref[...], h).astype(BF16)
        vtm_ref[0, 0, :, rows] = _dot_nt(wvt_ref[...], kvn).astype(BF16)


def _in0_call(x, pos3, g, w, wsvt, qan, wq, kvan, wk, wvt, gq, gqp, gk, gkp, invf):
    b, s, _ = x.shape
    tm = ROW_TILE
    nt = s // tm
    head_spec = pl.BlockSpec((1, MLA_HEADS, tm, LANES), lambda bi, i: (bi, 0, i, 0))
    vt_spec = pl.BlockSpec((1, 1, SB_WIDTH, tm), lambda bi, i: (bi, i, 0, 0))
    row_spec = pl.BlockSpec((1, tm, SB_WIDTH), lambda bi, i: (bi, i, 0))
    lane_vec = _full((1, LANES))
    return pl.pallas_call(
        _in0_kernel,
        grid=(b, nt),
        in_specs=[
            pl.BlockSpec((1, tm, D_MODEL), lambda bi, i: (bi, i, 0)),
            pl.BlockSpec((1, tm, 1), lambda bi, i: (bi, i, 0)),
            _full((1, D_MODEL)),
            _full(w.shape),
            _full(wsvt.shape),
            _full((1, MLA_Q_RANK)),
            _full(wq.shape),
            _full((1, MLA_KV_RANK)),
            _full(wk.shape),
            _full(wvt.shape),
            lane_vec, lane_vec, lane_vec, lane_vec, lane_vec,
        ],
        out_specs=[head_spec, head_spec, vt_spec, row_spec, row_spec, vt_spec],
        out_shape=[
            jax.ShapeDtypeStruct((b, MLA_HEADS, s, LANES), BF16),
            jax.ShapeDtypeStruct((b, MLA_HEADS, s, LANES), BF16),
            jax.ShapeDtypeStruct((b, nt, MLA_HEADS * MLA_V, tm), BF16),
            jax.ShapeDtypeStruct((b, s, SB_WIDTH), BF16),
            jax.ShapeDtypeStruct((b, s, SB_WIDTH), BF16),
            jax.ShapeDtypeStruct((b, nt, SB_WIDTH, tm), BF16),
        ],
        compiler_params=_compiler_params(2),
        name="l0_in",
    )(x, pos3, g, w, wsvt, qan, wq, kvan, wk, wvt, gq, gqp, gk, gkp, invf)


def _diag_spans(n_chains, chain, tk):
    for c in range(n_chains):
        end = (c + 1) * chain
        for d in range(-(-end // tk)):
            nk = min(tk, end - d * tk)
            yield c, d, nk, d * tk + nk > c * chain


def _softmax_attn_kernel(q_ref, k_ref, vt_ref, o_ref, m_scr, acc_scr, pre_scr, *, head_dim):
    tq, tk, cw = Q_TILE, K_TILE, CHAIN
    n_c = tq // cw
    kt_per_q = tq // tk
    qi = pl.program_id(2)
    chains = [(hh, c) for hh in range(2) for c in range(n_c)]
    qs = [q_ref[0, hh, c * cw:(c + 1) * cw, :] for hh, c in chains]

    m_scr[...] = jnp.full(m_scr.shape, NEG_INF, F32)
    acc_scr[...] = jnp.zeros(acc_scr.shape, F32)

    def with_ones(vt):
        return jnp.concatenate([vt, jnp.ones((ONES_ROWS, vt.shape[1]), vt.dtype)], axis=0)

    def update(idx, s, vt, visible):
        if visible is not None:
            s = jnp.where(visible, s, NEG_INF)
        m = m_scr[idx]
        m_new = jnp.maximum(m, jnp.max(s, axis=0, keepdims=True))
        p = jnp.exp2(s - m_new)
        m_scr[idx] = m_new
        acc_scr[idx] = jnp.exp2(m - m_new) * acc_scr[idx] + _dot(vt, p.astype(BF16))

    def scores_of(idx, tile, nk=tk):
        hh = chains[idx][0]
        return _dot_nt(k_ref[0, hh, pl.ds(pl.multiple_of(tile * tk, tk), nk), :], qs[idx])

    def run_pipeline(items, tail_fns, ahead):
        fns = [score_fn for score_fn, _ in items] + list(tail_fns)
        pending = {w: fns[w]() for w in range(min(ahead, len(fns)))}
        for v, (_, consume) in enumerate(items):
            if v + ahead < len(fns):
                pending[v + ahead] = fns[v + ahead]()
            consume(pending.pop(v))

    def prefetch_fns(tile):
        def store(a):
            def fn():
                pre_scr[a] = scores_of(a, tile)
            return fn
        return [store(a) for a in range(PREFETCH)]

    n_full = qi * kt_per_q
    items = []
    for hh in range(2):
        for c, d, nk, mask in _diag_spans(n_c, cw, tk):
            idx = hh * n_c + c
            visible = None
            if mask:
                row = lax.broadcasted_iota(jnp.int32, (nk, cw), 0) + d * tk
                col = lax.broadcasted_iota(jnp.int32, (nk, cw), 1) + c * cw
                visible = row <= col

            def score_fn(idx=idx, d=d, nk=nk):
                return scores_of(idx, n_full + d, nk)

            def consume(s, idx=idx, hh=hh, d=d, nk=nk, visible=visible):
                vt = vt_ref[0, n_full + d, hh * head_dim:(hh + 1) * head_dim, 0:nk]
                update(idx, s, with_ones(vt), visible)

            items.append((score_fn, consume))
    run_pipeline(items, prefetch_fns(0), DIAG_LOOKAHEAD)

    def full_step(i, _):
        items = []
        for u in range(kt_per_q):
            j = i * kt_per_q + u
            for idx, (hh, _) in enumerate(chains):
                if u == 0 and idx < PREFETCH:
                    score_fn = lambda idx=idx: pre_scr[idx]
                else:
                    score_fn = lambda idx=idx, j=j: scores_of(idx, j)

                def consume(s, idx=idx, hh=hh, j=j):
                    vt = vt_ref[0, j, hh * head_dim:(hh + 1) * head_dim, :]
                    update(idx, s, with_ones(vt), None)

                items.append((score_fn, consume))
        run_pipeline(items, prefetch_fns(jnp.minimum(i + 1, qi - 1) * kt_per_q), LOOKAHEAD)
        return 0

    lax.fori_loop(0, qi, full_step, 0)

    def normalized(idx):
        acc = acc_scr[idx]
        return acc[:head_dim] / acc[head_dim:head_dim + 1]

    outs = [jnp.concatenate([normalized(hh * n_c + c) for c in range(n_c)], axis=1)
            for hh in range(2)]
    o_ref[0] = jnp.concatenate(outs, axis=0).T.astype(o_ref.dtype)


def _softmax_attn_call(q, k, vt, name):
    b, h, s, _ = q.shape
    tq = Q_TILE
    nkt = s // K_TILE
    head_dim = vt.shape[2] // h
    n_chains = 2 * tq // CHAIN
    return pl.pallas_call(
        functools.partial(_softmax_attn_kernel, head_dim=head_dim),
        grid=(b, h // 2, s // tq),
        in_specs=[
            pl.BlockSpec((1, 2, tq, LANES), lambda bi, p, i: (bi, p, i, 0)),
            pl.BlockSpec((1, 2, s, LANES), lambda bi, p, i: (bi, p, 0, 0)),
            pl.BlockSpec((1, nkt, 2 * head_dim, K_TILE), lambda bi, p, i: (bi, 0, p, 0)),
        ],
        out_specs=pl.BlockSpec((1, tq, 2 * head_dim), lambda bi, p, i: (bi, i, p)),
        out_shape=jax.ShapeDtypeStruct((b, s, h * head_dim), BF16),
        scratch_shapes=[
            pltpu.VMEM((n_chains, 1, CHAIN), F32),
            pltpu.VMEM((n_chains, head_dim + ONES_ROWS, CHAIN), F32),
            pltpu.VMEM((PREFETCH, K_TILE, CHAIN), F32),
        ],
        compiler_params=_compiler_params(3),
        name=name,
    )(q, k, vt)


def _sb_attn_kernel(q_ref, k_ref, vt_ref, o_ref, run_scr, acc_scr, pre_scr):
    tq, tk, cw, sub = Q_TILE, K_TILE, CHAIN, SB_SUB
    n_c = tq // cw
    kt_per_q = tq // tk
    qi = pl.program_id(2)
    s_row = lax.broadcasted_iota(jnp.int32, (sub, sub), 0)
    s_col = lax.broadcasted_iota(jnp.int32, (sub, sub), 1)
    neg_later = -((s_col > s_row).astype(BF16))
    lane = _lane_iota(cw)
    chains = [(hh, c) for hh in range(2) for c in range(n_c)]
    qs = []
    for hh, c in chains:
        q_pair = q_ref[0, c * cw:(c + 1) * cw, :]
        in_head = (lane >= hh * SB_DIM) & (lane < (hh + 1) * SB_DIM)
        qs.append(jnp.where(in_head, q_pair, jnp.zeros_like(q_pair)))

    run_scr[...] = jnp.zeros(run_scr.shape, F32)
    acc_scr[...] = jnp.zeros(acc_scr.shape, F32)

    def suffix_sums(z, visible):
        cost = jnp.maximum(z, 0.0) + jnp.log(1.0 + jnp.exp2(-jnp.abs(z))) * LOG2E
        log_beta = z - cost
        if visible is not None:
            cost = jnp.where(visible, cost, 0.0)
        cost16 = cost.astype(BF16)
        blocks = [slice(b * sub, (b + 1) * sub) for b in range(z.shape[0] // sub)]
        suffix = [_dot(neg_later, cost16[bl]) for bl in blocks]
        return log_beta, cost, blocks, suffix

    def accumulate(idx, stage, vt, visible):
        log_beta, cost, blocks, suffix = stage
        run = run_scr[idx]
        terms = []
        for bl, sfx in reversed(list(zip(blocks, suffix))):
            terms.append(log_beta[bl] + sfx + run)
            run = run + sfx[0:1, :] - cost[bl.start:bl.start + 1, :]
        a = jnp.exp2(jnp.concatenate(terms[::-1], axis=0))
        if visible is not None:
            a = jnp.where(visible, a, 0.0)
        run_scr[idx] = run
        acc_scr[idx] = acc_scr[idx] + _dot(vt, a.astype(BF16))

    def tile_update(idxs, score_fns, vts, masks, tail_fns=()):
        n = len(idxs)
        fns = list(score_fns) + list(tail_fns)
        pending, stages = {}, {}
        for w in range(min(LOOKAHEAD, len(fns))):
            pending[w] = fns[w]()
        for v in range(n + 1):
            if v + LOOKAHEAD < len(fns):
                pending[v + LOOKAHEAD] = fns[v + LOOKAHEAD]()
            if v < n:
                stages[v] = suffix_sums(pending.pop(v), masks[v])
            if v >= 1:
                accumulate(idxs[v - 1], stages.pop(v - 1), vts[v - 1], masks[v - 1])

    def scores_fn(idx, start, nk):
        return lambda: _dot_nt(k_ref[0, pl.ds(pl.multiple_of(start, tk), nk), :], qs[idx])

    n_full = qi * kt_per_q
    def still_alive(idxs):
        return functools.reduce(
            jnp.logical_or, [jnp.max(run_scr[idx]) > RUN_FLOOR for idx in idxs])

    def diag_items(masked):
        idxs, fns, vts, masks = [], [], [], []
        for hh in range(2):
            for c, d, nk, mask in _diag_spans(n_c, cw, tk):
                if mask != masked:
                    continue
                visible = None
                if mask:
                    row = lax.broadcasted_iota(jnp.int32, (nk, cw), 0) + d * tk
                    col = lax.broadcasted_iota(jnp.int32, (nk, cw), 1) + c * cw
                    visible = row < col
                idxs.append(hh * n_c + c)
                fns.append(scores_fn(hh * n_c + c, (n_full + d) * tk, nk))
                vts.append(vt_ref[0, n_full + d, hh * SB_DIM:(hh + 1) * SB_DIM, 0:nk])
                masks.append(visible)
        return idxs, fns, vts, masks

    tile_update(*diag_items(True))
    late = diag_items(False)

    @pl.when(still_alive(sorted(set(late[0]))))
    def _():
        tile_update(*late)

    all_idx = list(range(len(chains)))
    alive = jnp.logical_and(n_full > 0, still_alive(all_idx))

    @pl.when(alive)
    def _():
        for a in range(PREFETCH):
            pre_scr[a] = scores_fn(a, jnp.maximum(n_full - 1, 0) * tk, tk)()

    def store_prefetch(a, start):
        def fn():
            pre_scr[a] = scores_fn(a, start, tk)()
        return fn

    def full_step(i, _):
        j = n_full - 1 - i
        vts = [vt_ref[0, j, hh * SB_DIM:(hh + 1) * SB_DIM, :] for hh, _ in chains]
        fns = [(lambda a=a: pre_scr[a]) if a < PREFETCH else scores_fn(a, j * tk, tk)
               for a in all_idx]
        nxt = [store_prefetch(a, jnp.maximum(j - 1, 0) * tk) for a in range(PREFETCH)]
        tile_update(all_idx, fns, vts, [None] * len(chains), nxt)

    def step_and_test(carry):
        i, _ = carry
        full_step(i, None)
        return i + 1, still_alive(all_idx)

    lax.while_loop(lambda carry: jnp.logical_and(carry[0] < n_full, carry[1]),
                   step_and_test, (jnp.int32(0), alive))
    outs = [jnp.concatenate([acc_scr[hh * n_c + c] for c in range(n_c)], axis=1)
            for hh in range(2)]
    o_ref[0] = jnp.concatenate(outs, axis=0).T.astype(o_ref.dtype)


def _sb_attn_call(q, k, vt):
    b, s, w = q.shape
    tq = Q_TILE
    nkt = s // K_TILE
    n_chains = 2 * tq // CHAIN
    return pl.pallas_call(
        _sb_attn_kernel,
        grid=(b, w // LANES, s // tq),
        in_specs=[
            pl.BlockSpec((1, tq, LANES), lambda bi, p, i: (bi, i, p)),
            pl.BlockSpec((1, s, LANES), lambda bi, p, i: (bi, 0, p)),
            pl.BlockSpec((1, nkt, LANES, K_TILE), lambda bi, p, i: (bi, 0, p, 0)),
        ],
        out_specs=pl.BlockSpec((1, tq, LANES), lambda bi, p, i: (bi, i, p)),
        out_shape=jax.ShapeDtypeStruct((b, s, w), BF16),
        scratch_shapes=[
            pltpu.VMEM((n_chains, 1, CHAIN), F32),
            pltpu.VMEM((n_chains, SB_DIM, CHAIN), F32),
            pltpu.VMEM((PREFETCH, K_TILE, CHAIN), F32),
        ],
        compiler_params=_compiler_params(3),
        name="l0_sb_attn",
    )(q, k, vt)


def _out_ffn_kernel(*refs, n_mix):
    x_ref = refs[0]
    o_refs = refs[1:1 + n_mix]
    wo_refs = refs[1 + n_mix:1 + 2 * n_mix]
    g_ref, wg_ref, wu_ref, wd_ref, y_ref = refs[1 + 2 * n_mix:]
    y = x_ref[...]
    for o_ref, wo_ref in zip(o_refs, wo_refs):
        y = y + _dot(o_ref[...], wo_ref[...])
    h = _rms(y, g_ref[...], D_MODEL).astype(BF16)
    gate = _dot(h, wg_ref[...])
    up = _dot(h, wu_ref[...])
    act = gate * (1.0 / (1.0 + jnp.exp(-gate))) * up
    y_ref[...] = y + _dot(act.astype(BF16), wd_ref[...])


def _out_ffn_call(x2, mixes, wos, g, wg, wu, wd, name):
    n, d = x2.shape
    tm = FFN_TILE
    n_mix = len(mixes)
    row = lambda width: pl.BlockSpec((tm, width), lambda i: (i, 0))
    return pl.pallas_call(
        functools.partial(_out_ffn_kernel, n_mix=n_mix),
        grid=(n // tm,),
        in_specs=([row(d)] + [row(m.shape[1]) for m in mixes]
                  + [_full(w.shape) for w in wos]
                  + [_full((1, d)), _full(wg.shape), _full(wu.shape), _full(wd.shape)]),
        out_specs=row(d),
        out_shape=jax.ShapeDtypeStruct((n, d), F32),
        compiler_params=_compiler_params(1),
        name=name,
    )(x2, *mixes, *wos, g, wg, wu, wd)


def _in1_kernel(x_ref, g_ref, w_ref, wvt_ref, fb_ref, gq_ref, gk_ref, layout_ref,
                qa_ref, ka_ref, vt_ref, carry_ref):
    i = pl.program_id(1)

    @pl.when(i == 0)
    def _():
        carry_ref[...] = jnp.zeros_like(carry_ref)

    n_terms = AUG_TERMS
    used = n_terms * FOX_HEADS
    rows = IN_SUB
    lane = _lane_iota(1)
    half_lane = lane & (FOX_DIM - 1)
    low_half = lane < FOX_DIM
    term = layout_ref[...]
    decay_lane = term < n_terms
    ones_lane = term == n_terms
    own_rows = [jnp.where((half_lane >= n_terms * hd) & (half_lane < n_terms * (hd + 1)), 1.0, 0.0)
                for hd in range(FOX_HEADS)]
    r_i = lax.broadcasted_iota(jnp.int32, (rows, rows), 0)
    c_i = lax.broadcasted_iota(jnp.int32, (rows, rows), 1)
    lower = (c_i <= r_i).astype(BF16)
    g = g_ref[...]
    gq = gq_ref[...]
    gk = gk_ref[...]

    deferred = []
    for r in range(ROW_TILE // IN_SUB):
        sl_rows = slice(r * rows, (r + 1) * rows)
        h = _rms(x_ref[0, sl_rows, :], g, D_MODEL).astype(BF16)
        proj = _dot(h, w_ref[...])
        deferred.append((sl_rows, h))
        q = proj[:, 0:FOX_WIDTH]
        k = proj[:, FOX_WIDTH:2 * FOX_WIDTH]
        f_logit = proj[:, 2 * FOX_WIDTH:]

        log_f = jnp.where(decay_lane, _log_sigmoid(f_logit + fb_ref[...]), 0.0)
        f1, f2, f3 = _split3(log_f)
        decay = carry_ref[...] + _dot(lower, f1) + _dot(lower, f2) + _dot(lower, f3)
        carry_ref[...] = decay[rows - 1:rows, :]
        d1, d2, d3 = (d.astype(F32) for d in _split3(decay * LOG2E))
        terms = jnp.where(term == 0, d1, jnp.where(term == 1, d2, d3))
        aug_q = jnp.where(decay_lane, terms, jnp.where(ones_lane, 1.0, 0.0))
        neg_terms = -terms

        for pair in range(FOX_HEADS // 2):
            sl = slice(pair * LANES, (pair + 1) * LANES)
            qp = q[:, sl]
            kp = k[:, sl]
            qsq = qp * qp
            ksq = kp * kp
            q_gain = qp * gq
            k_gain = kp * gk
            for parity in range(2):
                hd = 2 * pair + parity
                in_head = low_half if parity == 0 else jnp.logical_not(low_half)

                def inv_rms(sq):
                    ss = jnp.sum(jnp.where(in_head, sq, 0.0), axis=-1, keepdims=True)
                    return lax.rsqrt(ss * (1.0 / FOX_DIM) + EPS)

                qa_ref[0, hd, sl_rows, :] = jnp.where(
                    in_head, q_gain * inv_rms(qsq), aug_q).astype(BF16)
                moved = pltpu.roll(neg_terms, used - n_terms * hd, 1)
                aug_k = jnp.where(ones_lane, moved, own_rows[hd])
                ka_ref[0, hd, sl_rows, :] = jnp.where(
                    in_head, k_gain * inv_rms(ksq), aug_k).astype(BF16)

    for sl_rows, h in deferred:
        vt_ref[0, 0, :, sl_rows] = _dot_nt(wvt_ref[...], h).astype(BF16)


def _in1_call(x, g, w, wvt, fb, gq, gk, layout):
    b, s, _ = x.shape
    tm = ROW_TILE
    nt = s // tm
    head_spec = pl.BlockSpec((1, FOX_HEADS, tm, LANES), lambda bi, i: (bi, 0, i, 0))
    lane_vec = _full((1, LANES))
    return pl.pallas_call(
        _in1_kernel,
        grid=(b, nt),
        in_specs=[
            pl.BlockSpec((1, tm, D_MODEL), lambda bi, i: (bi, i, 0)),
            _full((1, D_MODEL)),
            _full(w.shape),
            _full(wvt.shape),
            lane_vec, lane_vec, lane_vec, lane_vec,
        ],
        out_specs=[head_spec, head_spec,
                   pl.BlockSpec((1, 1, FOX_WIDTH, tm), lambda bi, i: (bi, i, 0, 0))],
        out_shape=[
            jax.ShapeDtypeStruct((b, FOX_HEADS, s, LANES), BF16),
            jax.ShapeDtypeStruct((b, FOX_HEADS, s, LANES), BF16),
            jax.ShapeDtypeStruct((b, nt, FOX_WIDTH, tm), BF16),
        ],
        scratch_shapes=[pltpu.VMEM((1, LANES), F32)],
        compiler_params=_compiler_params(2),
        name="l1_in",
    )(x, g, w, wvt, fb, gq, gk, layout)


def _pad_lanes(v, width=LANES):
    return jnp.pad(v, (0, width - v.shape[0])).reshape(1, width)


def _rope_partner(cols):
    half = MLA_ROPE // 2
    return jnp.concatenate([-cols[..., half:], cols[..., :half]], axis=-1)


def _prep_layer0(w_in, w_q_up, w_kv_up, q_norm, k_norm):
    o = MLA_Q_RANK + MLA_KV_RANK
    pad_rope = lambda c: jnp.pad(c, ((0, 0),) * (c.ndim - 1) + ((MLA_NOPE, LANES - MLA_QK),))
    w_rope = w_in[:, o:o + MLA_ROPE]
    sb = w_in[:, o + MLA_ROPE:]
    w = jnp.concatenate([w_in[:, :o], pad_rope(w_rope), pad_rope(_rope_partner(w_rope)),
                         sb[:, :2 * SB_WIDTH]], axis=1)
    wsvt = sb[:, 2 * SB_WIDTH:].T
    wq3 = w_q_up.reshape(MLA_Q_RANK, MLA_HEADS, MLA_QK)
    wq = jnp.pad(wq3, ((0, 0), (0, 0), (0, LANES - MLA_QK)))
    wq_partner = pad_rope(_rope_partner(wq3[:, :, MLA_NOPE:]))
    wq = jnp.concatenate([wq.reshape(MLA_Q_RANK, -1), wq_partner.reshape(MLA_Q_RANK, -1)], axis=1)
    kv = w_kv_up.reshape(MLA_KV_RANK, MLA_HEADS, MLA_NOPE + MLA_V)
    wk = jnp.pad(kv[:, :, :MLA_NOPE], ((0, 0), (0, 0), (0, LANES - MLA_NOPE)))
    wk = wk.reshape(MLA_KV_RANK, MLA_HEADS * LANES)
    wvt = kv[:, :, MLA_NOPE:].reshape(MLA_KV_RANK, MLA_HEADS * MLA_V).T
    gq = q_norm * (MLA_QK ** -0.5 * LOG2E)
    half = MLA_ROPE // 2
    swap_halves = lambda r: jnp.concatenate([r[half:], r[:half]])
    gqp = pad_rope(swap_halves(gq[MLA_NOPE:]))
    gkp = pad_rope(swap_halves(k_norm[MLA_NOPE:]))
    inv_freq = ROPE_THETA ** (-jnp.arange(half, dtype=F32) / half)
    invf = pad_rope(jnp.concatenate([inv_freq, inv_freq]))
    lanes = lambda v: v.reshape(1, LANES)
    return (w.astype(BF16), wsvt.astype(BF16), wq.astype(BF16), wk.astype(BF16), wvt.astype(BF16),
            _pad_lanes(gq), lanes(gqp), _pad_lanes(k_norm), lanes(gkp), lanes(invf))


def _prep_layer1(w_in, f_bias, q_norm, k_norm):
    wf = w_in[:, 3 * FOX_WIDTH:]
    rep = jnp.repeat(wf, AUG_TERMS, axis=1)
    pad = jnp.zeros((wf.shape[0], FOX_DIM - AUG_TERMS * FOX_HEADS), wf.dtype)
    f_group = jnp.concatenate([rep, pad, rep, pad], axis=1)
    w = jnp.concatenate([w_in[:, :2 * FOX_WIDTH], f_group], axis=1)
    wvt = w_in[:, 2 * FOX_WIDTH:3 * FOX_WIDTH].T
    fb = jnp.repeat(f_bias, AUG_TERMS)
    fpad = jnp.zeros((FOX_DIM - AUG_TERMS * FOX_HEADS,), f_bias.dtype)
    fb = jnp.concatenate([fb, fpad, fb, fpad]).reshape(1, LANES)
    gain2 = lambda gvec: jnp.concatenate([gvec, gvec]).reshape(1, LANES)
    used = AUG_TERMS * FOX_HEADS
    half_lane = jnp.arange(LANES, dtype=jnp.int32) % FOX_DIM
    layout = jnp.where(half_lane < used, half_lane % AUG_TERMS,
                       jnp.where(half_lane < used + AUG_TERMS, AUG_TERMS, AUG_TERMS + 1))
    return (w.astype(BF16), wvt.astype(BF16), fb,
            gain2(q_norm * (FOX_DIM ** -0.5 * LOG2E)), gain2(k_norm), layout.reshape(1, LANES))


def kernel(x, positions, l0_attn_norm, l0_w_in, l0_mla_q_a_norm, l0_mla_w_q_up, l0_mla_kv_a_norm, l0_mla_w_kv_up, l0_mla_q_norm, l0_mla_k_norm, l0_w_o, l0_ffn_norm, l0_w_gate, l0_w_up, l0_w_down, l1_attn_norm, l1_w_in, l1_fox_f_bias, l1_fox_q_norm, l1_fox_k_norm, l1_w_o, l1_ffn_norm, l1_w_gate, l1_w_up, l1_w_down):
    b, s, d = x.shape
    assert d == D_MODEL and s % Q_TILE == 0 and K_TILE == ROW_TILE and Q_TILE % K_TILE == 0
    assert K_TILE % CHAIN == 0 and CHAIN % SB_SUB == 0
    row = lambda v: v.reshape(1, -1)

    w0, wsvt, wq, wk, wvt0, gq0, gqp0, gk0, gkp0, invf = _prep_layer0(
        l0_w_in, l0_mla_w_q_up, l0_mla_w_kv_up, l0_mla_q_norm, l0_mla_k_norm)
    qm, km, vtm, sq, sk, vts = _in0_call(
        x, positions.reshape(b, s, 1), row(l0_attn_norm), w0, wsvt, row(l0_mla_q_a_norm), wq,
        row(l0_mla_kv_a_norm), wk, wvt0, gq0, gqp0, gk0, gkp0, invf)
    o_mla = _softmax_attn_call(qm, km, vtm, "l0_mla_attn")
    o_sb = _sb_attn_call(sq, sk, vts)
    n_mla = MLA_HEADS * MLA_V
    x1 = _out_ffn_call(
        x.reshape(b * s, d),
        [o_mla.reshape(b * s, n_mla), o_sb.reshape(b * s, SB_WIDTH)],
        [l0_w_o[:n_mla].astype(BF16), l0_w_o[n_mla:].astype(BF16)],
        row(l0_ffn_norm), l0_w_gate.astype(BF16), l0_w_up.astype(BF16),
        l0_w_down.astype(BF16), "l0_out_ffn")

    w1, wvt1, fb, gq1, gk1, layout = _prep_layer1(
        l1_w_in, l1_fox_f_bias, l1_fox_q_norm, l1_fox_k_norm)
    qa, ka, vtf = _in1_call(
        x1.reshape(b, s, d), row(l1_attn_norm), w1, wvt1, fb, gq1, gk1, layout)
    o_fox = _softmax_attn_call(qa, ka, vtf, "l1_fox_attn")
    out = _out_ffn_call(
        x1, [o_fox.reshape(b * s, FOX_WIDTH)], [l1_w_o.astype(BF16)],
        row(l1_ffn_norm), l1_w_gate.astype(BF16), l1_w_up.astype(BF16),
        l1_w_down.astype(BF16), "l1_out_ffn")
    return out.reshape(b, s, d)
```

```python
import functools

import jax
import jax.numpy as jnp
from jax import lax
from jax.experimental import pallas as pl
from jax.experimental.pallas import tpu as pltpu

F32 = jnp.float32
BF16 = jnp.bfloat16

D_MODEL = 1024
EPS = 1e-6
NEG_INF = -1e30
ROPE_THETA = 10000.0

MLA_HEADS = 8
MLA_Q_RANK = 256
MLA_KV_RANK = 128
MLA_NOPE = 64
MLA_ROPE = 32
MLA_QK = MLA_NOPE + MLA_ROPE
MLA_V = 64
SB_HEADS = 8
SB_DIM = 64
SB_WIDTH = SB_HEADS * SB_DIM
FOX_HEADS = 16
FOX_DIM = 64
FOX_WIDTH = FOX_HEADS * FOX_DIM

LANES = 128
V7X_VMEM_BYTES = 64 * 1024 * 1024
VMEM_LIMIT = V7X_VMEM_BYTES * 7 // 8

ROW_TILE = 512
K_TILE = 512
Q_TILE = 1024
CHAIN = 256
FFN_TILE = 512
PREFETCH = 2
LOOKAHEAD = 3
DIAG_LOOKAHEAD = 4
ONES_ROWS = 16
SB_SUB = 128
LOG2E = 1.4426950408889634
RUN_FLOOR = -160.0

AUG_TERMS = 3
IN_SUB = 256


def _compiler_params(n_axes):
    return pltpu.CompilerParams(
        dimension_semantics=("arbitrary",) * n_axes,
        vmem_limit_bytes=VMEM_LIMIT,
    )


def _full(shape):
    return pl.BlockSpec(shape, lambda *_: (0,) * len(shape), pipeline_mode=pl.Buffered(1))


def _lane_iota(rows):
    return lax.broadcasted_iota(jnp.int32, (rows, LANES), 1)


def _rms(x, gain, width):
    ss = jnp.sum(x * x, axis=-1, keepdims=True)
    return x * lax.rsqrt(ss * (1.0 / width) + EPS) * gain


def _log_sigmoid(z):
    return jnp.minimum(z, 0.0) - jnp.log(1.0 + jnp.exp(-jnp.abs(z)))


def _split3(x):
    a = x.astype(BF16)
    r = x - a.astype(F32)
    b = r.astype(BF16)
    c = (r - b.astype(F32)).astype(BF16)
    return a, b, c


def _dot(a, b):
    return jnp.dot(a, b, preferred_element_type=F32)


def _dot_nt(a, b):
    return lax.dot_general(a, b, (((1,), (1,)), ((), ())), preferred_element_type=F32)


def _in0_kernel(x_ref, pos_ref, g_ref, w_ref, wsvt_ref, qan_ref, wq_ref, kvan_ref, wk_ref,
                wvt_ref, gq_ref, gqp_ref, gk_ref, gkp_ref, invf_ref,
                qm_ref, km_ref, vtm_ref, sq_ref, sk_ref, vts_ref):
    g = g_ref[...]
    deferred = []
    for r in range(ROW_TILE // IN_SUB):
        rows = slice(r * IN_SUB, (r + 1) * IN_SUB)
        h = _rms(x_ref[0, rows, :], g, D_MODEL).astype(BF16)
        proj = _dot(h, w_ref[...])
        o = MLA_Q_RANK + MLA_KV_RANK
        cq = proj[:, 0:MLA_Q_RANK]
        ckv = proj[:, MLA_Q_RANK:o]
        k_rope = proj[:, o:o + LANES]
        k_rope_partner = proj[:, o + LANES:o + 2 * LANES]
        o += 2 * LANES
        sq_ref[0, rows, :] = (proj[:, o:o + SB_WIDTH] * (SB_DIM ** -0.5 * LOG2E)).astype(BF16)
        sk_ref[0, rows, :] = proj[:, o + SB_WIDTH:o + 2 * SB_WIDTH].astype(BF16)

        qn = _rms(cq, qan_ref[...], MLA_Q_RANK).astype(BF16)
        q_all = _dot(qn, wq_ref[...])
        kvn = _rms(ckv, kvan_ref[...], MLA_KV_RANK).astype(BF16)
        k_all = _dot(kvn, wk_ref[...])
        deferred.append((rows, h, kvn))

        ang = pos_ref[0, rows, :].astype(F32) * invf_ref[...]
        cos = jnp.cos(ang)
        sin = jnp.sin(ang)
        gq_cos = gq_ref[...] * cos
        gq_sin = gqp_ref[...] * sin
        gk_cos = gk_ref[...] * cos
        k_partner_term = k_rope_partner * (gkp_ref[...] * sin)

        def inv_rms(t):
            ss = jnp.sum(t * t, axis=-1, keepdims=True)
            return lax.rsqrt(ss * (1.0 / MLA_QK) + EPS)

        for hd in range(MLA_HEADS):
            sl = slice(hd * LANES, (hd + 1) * LANES)
            t = q_all[:, sl]
            t_partner = q_all[:, MLA_HEADS * LANES + hd * LANES:MLA_HEADS * LANES + (hd + 1) * LANES]
            qm_ref[0, hd, rows, :] = (
                (t * gq_cos + t_partner * gq_sin) * inv_rms(t)).astype(BF16)
            tk = k_all[:, sl] + k_rope
            km_ref[0, hd, rows, :] = (
                (tk * gk_cos + k_partner_term) * inv_rms(tk)).astype(BF16)

    for rows, h, kvn in deferred:
        vts_ref[0, 0, :, rows] = _dot_nt(wsvt_ref[...], h).astype(BF16)
        vtm_ref[0, 0, :, rows] = _dot_nt(wvt_ref[...], kvn).astype(BF16)


def _in0_call(x, pos3, g, w, wsvt, qan, wq, kvan, wk, wvt, gq, gqp, gk, gkp, invf):
    b, s, _ = x.shape
    tm = ROW_TILE
    nt = s // tm
    head_spec = pl.BlockSpec((1, MLA_HEADS, tm, LANES), lambda bi, i: (bi, 0, i, 0))
    vt_spec = pl.BlockSpec((1, 1, SB_WIDTH, tm), lambda bi, i: (bi, i, 0, 0))
    row_spec = pl.BlockSpec((1, tm, SB_WIDTH), lambda bi, i: (bi, i, 0))
    lane_vec = _full((1, LANES))
    return pl.pallas_call(
        _in0_kernel,
        grid=(b, nt),
        in_specs=[
            pl.BlockSpec((1, tm, D_MODEL), lambda bi, i: (bi, i, 0)),
            pl.BlockSpec((1, tm, 1), lambda bi, i: (bi, i, 0)),
            _full((1, D_MODEL)),
            _full(w.shape),
            _full(wsvt.shape),
            _full((1, MLA_Q_RANK)),
            _full(wq.shape),
            _full((1, MLA_KV_RANK)),
            _full(wk.shape),
            _full(wvt.shape),
            lane_vec, lane_vec, lane_vec, lane_vec, lane_vec,
        ],
        out_specs=[head_spec, head_spec, vt_spec, row_spec, row_spec, vt_spec],
        out_shape=[
            jax.ShapeDtypeStruct((b, MLA_HEADS, s, LANES), BF16),
            jax.ShapeDtypeStruct((b, MLA_HEADS, s, LANES), BF16),
            jax.ShapeDtypeStruct((b, nt, MLA_HEADS * MLA_V, tm), BF16),
            jax.ShapeDtypeStruct((b, s, SB_WIDTH), BF16),
            jax.ShapeDtypeStruct((b, s, SB_WIDTH), BF16),
            jax.ShapeDtypeStruct((b, nt, SB_WIDTH, tm), BF16),
        ],
        compiler_params=_compiler_params(2),
        name="l0_in",
    )(x, pos3, g, w, wsvt, qan, wq, kvan, wk, wvt, gq, gqp, gk, gkp, invf)


def _diag_spans(n_chains, chain, tk):
    for c in range(n_chains):
        end = (c + 1) * chain
        for d in range(-(-end // tk)):
            nk = min(tk, end - d * tk)
            yield c, d, nk, d * tk + nk > c * chain


def _softmax_attn_kernel(q_ref, k_ref, vt_ref, o_ref, m_scr, acc_scr, pre_scr, *, head_dim):
    tq, tk, cw = Q_TILE, K_TILE, CHAIN
    n_c = tq // cw
    kt_per_q = tq // tk
    qi = pl.program_id(2)
    chains = [(hh, c) for hh in range(2) for c in range(n_c)]
    qs = [q_ref[0, hh, c * cw:(c + 1) * cw, :] for hh, c in chains]

    m_scr[...] = jnp.full(m_scr.shape, NEG_INF, F32)
    acc_scr[...] = jnp.zeros(acc_scr.shape, F32)

    def with_ones(vt):
        return jnp.concatenate([vt, jnp.ones((ONES_ROWS, vt.shape[1]), vt.dtype)], axis=0)

    def update(idx, s, vt, visible):
        if visible is not None:
            s = jnp.where(visible, s, NEG_INF)
        m = m_scr[idx]
        m_new = jnp.maximum(m, jnp.max(s, axis=0, keepdims=True))
        p = jnp.exp2(s - m_new)
        m_scr[idx] = m_new
        acc_scr[idx] = jnp.exp2(m - m_new) * acc_scr[idx] + _dot(vt, p.astype(BF16))

    def scores_of(idx, tile, nk=tk):
        hh = chains[idx][0]
        return _dot_nt(k_ref[0, hh, pl.ds(pl.multiple_of(tile * tk, tk), nk), :], qs[idx])

    def run_pipeline(items, tail_fns, ahead):
        fns = [score_fn for score_fn, _ in items] + list(tail_fns)
        pending = {w: fns[w]() for w in range(min(ahead, len(fns)))}
        for v, (_, consume) in enumerate(items):
            if v + ahead < len(fns):
                pending[v + ahead] = fns[v + ahead]()
            consume(pending.pop(v))

    def prefetch_fns(tile):
        def store(a):
            def fn():
                pre_scr[a] = scores_of(a, tile)
            return fn
        return [store(a) for a in range(PREFETCH)]

    n_full = qi * kt_per_q
    items = []
    for hh in range(2):
        for c, d, nk, mask in _diag_spans(n_c, cw, tk):
            idx = hh * n_c + c
            visible = None
            if mask:
                row = lax.broadcasted_iota(jnp.int32, (nk, cw), 0) + d * tk
                col = lax.broadcasted_iota(jnp.int32, (nk, cw), 1) + c * cw
                visible = row <= col

            def score_fn(idx=idx, d=d, nk=nk):
                return scores_of(idx, n_full + d, nk)

            def consume(s, idx=idx, hh=hh, d=d, nk=nk, visible=visible):
                vt = vt_ref[0, n_full + d, hh * head_dim:(hh + 1) * head_dim, 0:nk]
                update(idx, s, with_ones(vt), visible)

            items.append((score_fn, consume))
    run_pipeline(items, prefetch_fns(0), DIAG_LOOKAHEAD)

    def full_step(i, _):
        items = []
        for u in range(kt_per_q):
            j = i * kt_per_q + u
            for idx, (hh, _) in enumerate(chains):
                if u == 0 and idx < PREFETCH:
                    score_fn = lambda idx=idx: pre_scr[idx]
                else:
                    score_fn = lambda idx=idx, j=j: scores_of(idx, j)

                def consume(s, idx=idx, hh=hh, j=j):
                    vt = vt_ref[0, j, hh * head_dim:(hh + 1) * head_dim, :]
                    update(idx, s, with_ones(vt), None)

                items.append((score_fn, consume))
        run_pipeline(items, prefetch_fns(jnp.minimum(i + 1, qi - 1) * kt_per_q), LOOKAHEAD)
        return 0

    lax.fori_loop(0, qi, full_step, 0)

    def normalized(idx):
        acc = acc_scr[idx]
        return acc[:head_dim] / acc[head_dim:head_dim + 1]

    outs = [jnp.concatenate([normalized(hh * n_c + c) for c in range(n_c)], axis=1)
            for hh in range(2)]
    o_ref[0] = jnp.concatenate(outs, axis=0).T.astype(o_ref.dtype)


def _softmax_attn_call(q, k, vt, name):
    b, h, s, _ = q.shape
    tq = Q_TILE
    nkt = s // K_TILE
    head_dim = vt.shape[2] // h
    n_chains = 2 * tq // CHAIN
    return pl.pallas_call(
        functools.partial(_softmax_attn_kernel, head_dim=head_dim),
        grid=(b, h // 2, s // tq),
        in_specs=[
            pl.BlockSpec((1, 2, tq, LANES), lambda bi, p, i: (bi, p, i, 0)),
            pl.BlockSpec((1, 2, s, LANES), lambda bi, p, i: (bi, p, 0, 0)),
            pl.BlockSpec((1, nkt, 2 * head_dim, K_TILE), lambda bi, p, i: (bi, 0, p, 0)),
        ],
        out_specs=pl.BlockSpec((1, tq, 2 * head_dim), lambda bi, p, i: (bi, i, p)),
        out_shape=jax.ShapeDtypeStruct((b, s, h * head_dim), BF16),
        scratch_shapes=[
            pltpu.VMEM((n_chains, 1, CHAIN), F32),
            pltpu.VMEM((n_chains, head_dim + ONES_ROWS, CHAIN), F32),
            pltpu.VMEM((PREFETCH, K_TILE, CHAIN), F32),
        ],
        compiler_params=_compiler_params(3),
        name=name,
    )(q, k, vt)


def _sb_attn_kernel(q_ref, k_ref, vt_ref, o_ref, run_scr, acc_scr):
    tq, tk, cw, sub = Q_TILE, K_TILE, CHAIN, SB_SUB
    n_c = tq // cw
    kt_per_q = tq // tk
    qi = pl.program_id(2)
    s_row = lax.broadcasted_iota(jnp.int32, (sub, sub), 0)
    s_col = lax.broadcasted_iota(jnp.int32, (sub, sub), 1)
    neg_later = -((s_col > s_row).astype(BF16))
    lane = _lane_iota(cw)
    chains = [(hh, c) for hh in range(2) for c in range(n_c)]
    qs = []
    for hh, c in chains:
        q_pair = q_ref[0, c * cw:(c + 1) * cw, :]
        in_head = (lane >= hh * SB_DIM) & (lane < (hh + 1) * SB_DIM)
        qs.append(jnp.where(in_head, q_pair, jnp.zeros_like(q_pair)))

    run_scr[...] = jnp.zeros(run_scr.shape, F32)
    acc_scr[...] = jnp.zeros(acc_scr.shape, F32)

    def suffix_sums(z, visible):
        cost = jnp.maximum(z, 0.0) + jnp.log(1.0 + jnp.exp2(-jnp.abs(z))) * LOG2E
        log_beta = z - cost
        if visible is not None:
            cost = jnp.where(visible, cost, 0.0)
        cost16 = cost.astype(BF16)
        blocks = [slice(b * sub, (b + 1) * sub) for b in range(z.shape[0] // sub)]
        suffix = [_dot(neg_later, cost16[bl]) for bl in blocks]
        return log_beta, cost, blocks, suffix

    def accumulate(idx, stage, vt, visible):
        log_beta, cost, blocks, suffix = stage
        run = run_scr[idx]
        terms = []
        for bl, sfx in reversed(list(zip(blocks, suffix))):
            terms.append(log_beta[bl] + sfx + run)
            run = run + sfx[0:1, :] - cost[bl.start:bl.start + 1, :]
        a = jnp.exp2(jnp.concatenate(terms[::-1], axis=0))
        if visible is not None:
            a = jnp.where(visible, a, 0.0)
        run_scr[idx] = run
        acc_scr[idx] = acc_scr[idx] + _dot(vt, a.astype(BF16))

    def tile_update(idxs, fns, vts, masks):
        n = len(idxs)
        pending, stages = {}, {}
        for w in range(min(LOOKAHEAD, len(fns))):
            pending[w] = fns[w]()
        for v in range(n + 1):
            if v + LOOKAHEAD < len(fns):
                pending[v + LOOKAHEAD] = fns[v + LOOKAHEAD]()
            if v < n:
                stages[v] = suffix_sums(pending.pop(v), masks[v])
            if v >= 1:
                accumulate(idxs[v - 1], stages.pop(v - 1), vts[v - 1], masks[v - 1])

    def scores_fn(idx, start, nk):
        return lambda: _dot_nt(k_ref[0, pl.ds(pl.multiple_of(start, tk), nk), :], qs[idx])

    n_full = qi * kt_per_q
    def still_alive(idxs):
        return functools.reduce(
            jnp.logical_or, [jnp.max(run_scr[idx]) > RUN_FLOOR for idx in idxs])

    idxs, fns, vts, masks = [], [], [], []
    for masked in (True, False):
        for hh in range(2):
            for c, d, nk, mask in _diag_spans(n_c, cw, tk):
                if mask != masked:
                    continue
                visible = None
                if mask:
                    row = lax.broadcasted_iota(jnp.int32, (nk, cw), 0) + d * tk
                    col = lax.broadcasted_iota(jnp.int32, (nk, cw), 1) + c * cw
                    visible = row < col
                idxs.append(hh * n_c + c)
                fns.append(scores_fn(hh * n_c + c, (n_full + d) * tk, nk))
                vts.append(vt_ref[0, n_full + d, hh * SB_DIM:(hh + 1) * SB_DIM, 0:nk])
                masks.append(visible)
    tile_update(idxs, fns, vts, masks)

    def loop_over_tiles(group):
        def step_and_test(carry):
            i, _ = carry
            j = n_full - 1 - i
            tile_update(group, [scores_fn(a, j * tk, tk) for a in group],
                        [vt_ref[0, j, chains[a][0] * SB_DIM:(chains[a][0] + 1) * SB_DIM, :]
                         for a in group],
                        [None] * len(group))
            return i + 1, still_alive(group)

        lax.while_loop(lambda carry: jnp.logical_and(carry[0] < n_full, carry[1]),
                       step_and_test, (jnp.int32(0), still_alive(group)))

    first_slice = [a for a, (_, c) in enumerate(chains) if c == 0]
    loop_over_tiles([a for a in range(len(chains)) if a not in first_slice])
    loop_over_tiles(first_slice)
    outs = [jnp.concatenate([acc_scr[hh * n_c + c] for c in range(n_c)], axis=1)
            for hh in range(2)]
    o_ref[0] = jnp.concatenate(outs, axis=0).T.astype(o_ref.dtype)


def _sb_attn_call(q, k, vt):
    b, s, w = q.shape
    tq = Q_TILE
    nkt = s // K_TILE
    n_chains = 2 * tq // CHAIN
    return pl.pallas_call(
        _sb_attn_kernel,
        grid=(b, w // LANES, s // tq),
        in_specs=[
            pl.BlockSpec((1, tq, LANES), lambda bi, p, i: (bi, i, p)),
            pl.BlockSpec((1, s, LANES), lambda bi, p, i: (bi, 0, p)),
            pl.BlockSpec((1, nkt, LANES, K_TILE), lambda bi, p, i: (bi, 0, p, 0)),
        ],
        out_specs=pl.BlockSpec((1, tq, LANES), lambda bi, p, i: (bi, i, p)),
        out_shape=jax.ShapeDtypeStruct((b, s, w), BF16),
        scratch_shapes=[
            pltpu.VMEM((n_chains, 1, CHAIN), F32),
            pltpu.VMEM((n_chains, SB_DIM, CHAIN), F32),
        ],
        compiler_params=_compiler_params(3),
        name="l0_sb_attn",
    )(q, k, vt)


def _out_ffn_kernel(*refs, n_mix):
    x_ref = refs[0]
    o_refs = refs[1:1 + n_mix]
    wo_refs = refs[1 + n_mix:1 + 2 * n_mix]
    g_ref, wg_ref, wu_ref, wd_ref, y_ref = refs[1 + 2 * n_mix:]
    y = x_ref[...]
    for o_ref, wo_ref in zip(o_refs, wo_refs):
        y = y + _dot(o_ref[...], wo_ref[...])
    h = _rms(y, g_ref[...], D_MODEL).astype(BF16)
    gate = _dot(h, wg_ref[...])
    up = _dot(h, wu_ref[...])
    act = gate * (1.0 / (1.0 + jnp.exp(-gate))) * up
    y_ref[...] = y + _dot(act.astype(BF16), wd_ref[...])


def _out_ffn_call(x2, mixes, wos, g, wg, wu, wd, name):
    n, d = x2.shape
    tm = FFN_TILE
    n_mix = len(mixes)
    row = lambda width: pl.BlockSpec((tm, width), lambda i: (i, 0))
    return pl.pallas_call(
        functools.partial(_out_ffn_kernel, n_mix=n_mix),
        grid=(n // tm,),
        in_specs=([row(d)] + [row(m.shape[1]) for m in mixes]
                  + [_full(w.shape) for w in wos]
                  + [_full((1, d)), _full(wg.shape), _full(wu.shape), _full(wd.shape)]),
        out_specs=row(d),
        out_shape=jax.ShapeDtypeStruct((n, d), F32),
        compiler_params=_compiler_params(1),
        name=name,
    )(x2, *mixes, *wos, g, wg, wu, wd)


def _in1_kernel(x_ref, g_ref, w_ref, wvt_ref, fb_ref, gq_ref, gk_ref, layout_ref,
                qa_ref, ka_ref, vt_ref, carry_ref):
    i = pl.program_id(1)

    @pl.when(i == 0)
    def _():
        carry_ref[...] = jnp.zeros_like(carry_ref)

    n_terms = AUG_TERMS
    used = n_terms * FOX_HEADS
    rows = IN_SUB
    lane = _lane_iota(1)
    half_lane = lane & (FOX_DIM - 1)
    low_half = lane < FOX_DIM
    term = layout_ref[...]
    decay_lane = term < n_terms
    ones_lane = term == n_terms
    own_rows = [jnp.where((half_lane >= n_terms * hd) & (half_lane < n_terms * (hd + 1)), 1.0, 0.0)
                for hd in range(FOX_HEADS)]
    r_i = lax.broadcasted_iota(jnp.int32, (rows, rows), 0)
    c_i = lax.broadcasted_iota(jnp.int32, (rows, rows), 1)
    lower = (c_i <= r_i).astype(BF16)
    g = g_ref[...]
    gq = gq_ref[...]
    gk = gk_ref[...]

    deferred = []
    for r in range(ROW_TILE // IN_SUB):
        sl_rows = slice(r * rows, (r + 1) * rows)
        h = _rms(x_ref[0, sl_rows, :], g, D_MODEL).astype(BF16)
        proj = _dot(h, w_ref[...])
        deferred.append((sl_rows, h))
        q = proj[:, 0:FOX_WIDTH]
        k = proj[:, FOX_WIDTH:2 * FOX_WIDTH]
        f_logit = proj[:, 2 * FOX_WIDTH:]

        log_f = jnp.where(decay_lane, _log_sigmoid(f_logit + fb_ref[...]), 0.0)
        f1, f2, f3 = _split3(log_f)
        decay = carry_ref[...] + _dot(lower, f1) + _dot(lower, f2) + _dot(lower, f3)
        carry_ref[...] = decay[rows - 1:rows, :]
        d1, d2, d3 = (d.astype(F32) for d in _split3(decay * LOG2E))
        terms = jnp.where(term == 0, d1, jnp.where(term == 1, d2, d3))
        aug_q = jnp.where(decay_lane, terms, jnp.where(ones_lane, 1.0, 0.0))
        neg_terms = -terms

        for pair in range(FOX_HEADS // 2):
            sl = slice(pair * LANES, (pair + 1) * LANES)
            qp = q[:, sl]
            kp = k[:, sl]
            qsq = qp * qp
            ksq = kp * kp
            q_gain = qp * gq
            k_gain = kp * gk
            for parity in range(2):
                hd = 2 * pair + parity
                in_head = low_half if parity == 0 else jnp.logical_not(low_half)

                def inv_rms(sq):
                    ss = jnp.sum(jnp.where(in_head, sq, 0.0), axis=-1, keepdims=True)
                    return lax.rsqrt(ss * (1.0 / FOX_DIM) + EPS)

                qa_ref[0, hd, sl_rows, :] = jnp.where(
                    in_head, q_gain * inv_rms(qsq), aug_q).astype(BF16)
                moved = pltpu.roll(neg_terms, used - n_terms * hd, 1)
                aug_k = jnp.where(ones_lane, moved, own_rows[hd])
                ka_ref[0, hd, sl_rows, :] = jnp.where(
                    in_head, k_gain * inv_rms(ksq), aug_k).astype(BF16)

    for sl_rows, h in deferred:
        vt_ref[0, 0, :, sl_rows] = _dot_nt(wvt_ref[...], h).astype(BF16)


def _in1_call(x, g, w, wvt, fb, gq, gk, layout):
    b, s, _ = x.shape
    tm = ROW_TILE
    nt = s // tm
    head_spec = pl.BlockSpec((1, FOX_HEADS, tm, LANES), lambda bi, i: (bi, 0, i, 0))
    lane_vec = _full((1, LANES))
    return pl.pallas_call(
        _in1_kernel,
        grid=(b, nt),
        in_specs=[
            pl.BlockSpec((1, tm, D_MODEL), lambda bi, i: (bi, i, 0)),
            _full((1, D_MODEL)),
            _full(w.shape),
            _full(wvt.shape),
            lane_vec, lane_vec, lane_vec, lane_vec,
        ],
        out_specs=[head_spec, head_spec,
                   pl.BlockSpec((1, 1, FOX_WIDTH, tm), lambda bi, i: (bi, i, 0, 0))],
        out_shape=[
            jax.ShapeDtypeStruct((b, FOX_HEADS, s, LANES), BF16),
            jax.ShapeDtypeStruct((b, FOX_HEADS, s, LANES), BF16),
            jax.ShapeDtypeStruct((b, nt, FOX_WIDTH, tm), BF16),
        ],
        scratch_shapes=[pltpu.VMEM((1, LANES), F32)],
        compiler_params=_compiler_params(2),
        name="l1_in",
    )(x, g, w, wvt, fb, gq, gk, layout)


def _pad_lanes(v, width=LANES):
    return jnp.pad(v, (0, width - v.shape[0])).reshape(1, width)


def _rope_partner(cols):
    half = MLA_ROPE // 2
    return jnp.concatenate([-cols[..., half:], cols[..., :half]], axis=-1)


def _prep_layer0(w_in, w_q_up, w_kv_up, q_norm, k_norm):
    o = MLA_Q_RANK + MLA_KV_RANK
    pad_rope = lambda c: jnp.pad(c, ((0, 0),) * (c.ndim - 1) + ((MLA_NOPE, LANES - MLA_QK),))
    w_rope = w_in[:, o:o + MLA_ROPE]
    sb = w_in[:, o + MLA_ROPE:]
    w = jnp.concatenate([w_in[:, :o], pad_rope(w_rope), pad_rope(_rope_partner(w_rope)),
                         sb[:, :2 * SB_WIDTH]], axis=1)
    wsvt = sb[:, 2 * SB_WIDTH:].T
    wq3 = w_q_up.reshape(MLA_Q_RANK, MLA_HEADS, MLA_QK)
    wq = jnp.pad(wq3, ((0, 0), (0, 0), (0, LANES - MLA_QK)))
    wq_partner = pad_rope(_rope_partner(wq3[:, :, MLA_NOPE:]))
    wq = jnp.concatenate([wq.reshape(MLA_Q_RANK, -1), wq_partner.reshape(MLA_Q_RANK, -1)], axis=1)
    kv = w_kv_up.reshape(MLA_KV_RANK, MLA_HEADS, MLA_NOPE + MLA_V)
    wk = jnp.pad(kv[:, :, :MLA_NOPE], ((0, 0), (0, 0), (0, LANES - MLA_NOPE)))
    wk = wk.reshape(MLA_KV_RANK, MLA_HEADS * LANES)
    wvt = kv[:, :, MLA_NOPE:].reshape(MLA_KV_RANK, MLA_HEADS * MLA_V).T
    gq = q_norm * (MLA_QK ** -0.5 * LOG2E)
    half = MLA_ROPE // 2
    swap_halves = lambda r: jnp.concatenate([r[half:], r[:half]])
    gqp = pad_rope(swap_halves(gq[MLA_NOPE:]))
    gkp = pad_rope(swap_halves(k_norm[MLA_NOPE:]))
    inv_freq = ROPE_THETA ** (-jnp.arange(half, dtype=F32) / half)
    invf = pad_rope(jnp.concatenate([inv_freq, inv_freq]))
    lanes = lambda v: v.reshape(1, LANES)
    return (w.astype(BF16), wsvt.astype(BF16), wq.astype(BF16), wk.astype(BF16), wvt.astype(BF16),
            _pad_lanes(gq), lanes(gqp), _pad_lanes(k_norm), lanes(gkp), lanes(invf))


def _prep_layer1(w_in, f_bias, q_norm, k_norm):
    wf = w_in[:, 3 * FOX_WIDTH:]
    rep = jnp.repeat(wf, AUG_TERMS, axis=1)
    pad = jnp.zeros((wf.shape[0], FOX_DIM - AUG_TERMS * FOX_HEADS), wf.dtype)
    f_group = jnp.concatenate([rep, pad, rep, pad], axis=1)
    w = jnp.concatenate([w_in[:, :2 * FOX_WIDTH], f_group], axis=1)
    wvt = w_in[:, 2 * FOX_WIDTH:3 * FOX_WIDTH].T
    fb = jnp.repeat(f_bias, AUG_TERMS)
    fpad = jnp.zeros((FOX_DIM - AUG_TERMS * FOX_HEADS,), f_bias.dtype)
    fb = jnp.concatenate([fb, fpad, fb, fpad]).reshape(1, LANES)
    gain2 = lambda gvec: jnp.concatenate([gvec, gvec]).reshape(1, LANES)
    used = AUG_TERMS * FOX_HEADS
    half_lane = jnp.arange(LANES, dtype=jnp.int32) % FOX_DIM
    layout = jnp.where(half_lane < used, half_lane % AUG_TERMS,
                       jnp.where(half_lane < used + AUG_TERMS, AUG_TERMS, AUG_TERMS + 1))
    return (w.astype(BF16), wvt.astype(BF16), fb,
            gain2(q_norm * (FOX_DIM ** -0.5 * LOG2E)), gain2(k_norm), layout.reshape(1, LANES))


def kernel(x, positions, l0_attn_norm, l0_w_in, l0_mla_q_a_norm, l0_mla_w_q_up, l0_mla_kv_a_norm, l0_mla_w_kv_up, l0_mla_q_norm, l0_mla_k_norm, l0_w_o, l0_ffn_norm, l0_w_gate, l0_w_up, l0_w_down, l1_attn_norm, l1_w_in, l1_fox_f_bias, l1_fox_q_norm, l1_fox_k_norm, l1_w_o, l1_ffn_norm, l1_w_gate, l1_w_up, l1_w_down):
    b, s, d = x.shape
    assert d == D_MODEL and s % Q_TILE == 0 and K_TILE == ROW_TILE and Q_TILE % K_TILE == 0
    assert K_TILE % CHAIN == 0 and CHAIN % SB_SUB == 0
    row = lambda v: v.reshape(1, -1)

    w0, wsvt, wq, wk, wvt0, gq0, gqp0, gk0, gkp0, invf = _prep_layer0(
        l0_w_in, l0_mla_w_q_up, l0_mla_w_kv_up, l0_mla_q_norm, l0_mla_k_norm)
    qm, km, vtm, sq, sk, vts = _in0_call(
        x, positions.reshape(b, s, 1), row(l0_attn_norm), w0, wsvt, row(l0_mla_q_a_norm), wq,
        row(l0_mla_kv_a_norm), wk, wvt0, gq0, gqp0, gk0, gkp0, invf)
    o_mla = _softmax_attn_call(qm, km, vtm, "l0_mla_attn")
    o_sb = _sb_attn_call(sq, sk, vts)
    n_mla = MLA_HEADS * MLA_V
    x1 = _out_ffn_call(
        x.reshape(b * s, d),
        [o_mla.reshape(b * s, n_mla), o_sb.reshape(b * s, SB_WIDTH)],
        [l0_w_o[:n_mla].astype(BF16), l0_w_o[n_mla:].astype(BF16)],
        row(l0_ffn_norm), l0_w_gate.astype(BF16), l0_w_up.astype(BF16),
        l0_w_down.astype(BF16), "l0_out_ffn")

    w1, wvt1, fb, gq1, gk1, layout = _prep_layer1(
        l1_w_in, l1_fox_f_bias, l1_fox_q_norm, l1_fox_k_norm)
    qa, ka, vtf = _in1_call(
        x1.reshape(b, s, d), row(l1_attn_norm), w1, wvt1, fb, gq1, gk1, layout)
    o_fox = _softmax_attn_call(qa, ka, vtf, "l1_fox_attn")
    out = _out_ffn_call(
        x1, [o_fox.reshape(b * s, FOX_WIDTH)], [l1_w_o.astype(BF16)],
        row(l1_ffn_norm), l1_w_gate.astype(BF16), l1_w_up.astype(BF16),
        l1_w_down.astype(BF16), "l1_out_ffn")
    return out.reshape(b, s, d)
```

```python
import functools

import jax
import jax.numpy as jnp
from jax import lax
from jax.experimental import pallas as pl
from jax.experimental.pallas import tpu as pltpu

F32 = jnp.float32
BF16 = jnp.bfloat16

D_MODEL = 1024
EPS = 1e-6
NEG_INF = -1e30
ROPE_THETA = 10000.0

MLA_HEADS = 8
MLA_Q_RANK = 256
MLA_KV_RANK = 128
MLA_NOPE = 64
MLA_ROPE = 32
MLA_QK = MLA_NOPE + MLA_ROPE
MLA_V = 64
SB_HEADS = 8
SB_DIM = 64
SB_WIDTH = SB_HEADS * SB_DIM
FOX_HEADS = 16
FOX_DIM = 64
FOX_WIDTH = FOX_HEADS * FOX_DIM

LANES = 128
V7X_VMEM_BYTES = 64 * 1024 * 1024
VMEM_LIMIT = V7X_VMEM_BYTES * 7 // 8

ROW_TILE = 512
K_TILE = 512
Q_TILE = 1024
SOFTMAX_HEADS = 8
CHAIN = 256
FFN_TILE = 512
PREFETCH = 2
LOOKAHEAD = 3
DIAG_LOOKAHEAD = 4
ONES_ROWS = 16
SB_SUB = 128
LOG2E = 1.4426950408889634
RUN_FLOOR = -160.0

AUG_TERMS = 3
TRIG_PACK = LANES // MLA_ROPE
IN_SUB = 256


def _compiler_params(n_axes):
    return pltpu.CompilerParams(
        dimension_semantics=("arbitrary",) * n_axes,
        vmem_limit_bytes=VMEM_LIMIT,
    )


def _full(shape):
    return pl.BlockSpec(shape, lambda *_: (0,) * len(shape), pipeline_mode=pl.Buffered(1))


def _lane_iota(rows):
    return lax.broadcasted_iota(jnp.int32, (rows, LANES), 1)


def _rms(x, gain, width):
    ss = jnp.sum(x * x, axis=-1, keepdims=True)
    return x * lax.rsqrt(ss * (1.0 / width) + EPS) * gain


def _log_sigmoid(z):
    return jnp.minimum(z, 0.0) - jnp.log(1.0 + jnp.exp(-jnp.abs(z)))


def _split3(x):
    a = x.astype(BF16)
    r = x - a.astype(F32)
    b = r.astype(BF16)
    c = (r - b.astype(F32)).astype(BF16)
    return a, b, c


def _dot(a, b):
    return jnp.dot(a, b, preferred_element_type=F32)


def _dot_nt(a, b):
    return lax.dot_general(a, b, (((1,), (1,)), ((), ())), preferred_element_type=F32)


def _in0_kernel(x_ref, pos_ref, g_ref, w_ref, wsvt_ref, qan_ref, wq_ref, kvan_ref, wk_ref,
                wvt_ref, gq_ref, gqp_ref, gk_ref, gkp_ref, invf_ref,
                qm_ref, km_ref, vtm_ref, sq_ref, sk_ref, vts_ref):
    g = g_ref[...]
    lane = _lane_iota(1)
    rope_lane = (lane >= MLA_NOPE) & (lane < MLA_QK)
    deferred = []
    for r in range(ROW_TILE // IN_SUB):
        rows = slice(r * IN_SUB, (r + 1) * IN_SUB)
        h = _rms(x_ref[0, rows, :], g, D_MODEL).astype(BF16)
        proj = _dot(h, w_ref[...])
        o = MLA_Q_RANK + MLA_KV_RANK
        cq = proj[:, 0:MLA_Q_RANK]
        ckv = proj[:, MLA_Q_RANK:o]
        k_rope = proj[:, o:o + LANES]
        k_rope_partner = proj[:, o + LANES:o + 2 * LANES]
        o += 2 * LANES
        sq_ref[0, rows, :] = (proj[:, o:o + SB_WIDTH] * (SB_DIM ** -0.5 * LOG2E)).astype(BF16)
        sk_ref[0, rows, :] = proj[:, o + SB_WIDTH:o + 2 * SB_WIDTH].astype(BF16)

        qn = _rms(cq, qan_ref[...], MLA_Q_RANK).astype(BF16)
        q_all = _dot(qn, wq_ref[...])
        kvn = _rms(ckv, kvan_ref[...], MLA_KV_RANK).astype(BF16)
        k_all = _dot(kvn, wk_ref[...])
        deferred.append((h, kvn))

        pos = pos_ref[0, rows, :].astype(F32)
        group = IN_SUB // TRIG_PACK
        packed = pos[0:group] * invf_ref[0:1, :]
        for gi in range(1, TRIG_PACK):
            packed = packed + pos[gi * group:(gi + 1) * group] * invf_ref[gi:gi + 1, :]

        def unpack(t, fill):
            parts = [t] + [pltpu.roll(t, LANES - MLA_ROPE * gi, 1) for gi in range(1, TRIG_PACK)]
            return jnp.where(rope_lane, jnp.concatenate(parts, axis=0), fill)

        cos = unpack(jnp.cos(packed), 1.0)
        sin = unpack(jnp.sin(packed), 0.0)
        gq_cos = gq_ref[...] * cos
        gq_sin = gqp_ref[...] * sin
        gk_cos = gk_ref[...] * cos
        k_partner_term = k_rope_partner * (gkp_ref[...] * sin)

        def inv_rms(t):
            ss = jnp.sum(t * t, axis=-1, keepdims=True)
            return lax.rsqrt(ss * (1.0 / MLA_QK) + EPS)

        for hd in range(MLA_HEADS):
            sl = slice(hd * LANES, (hd + 1) * LANES)
            t = q_all[:, sl]
            t_partner = q_all[:, MLA_HEADS * LANES + hd * LANES:MLA_HEADS * LANES + (hd + 1) * LANES]
            qm_ref[0, hd, rows, :] = (
                (t * gq_cos + t_partner * gq_sin) * inv_rms(t)).astype(BF16)
            tk = k_all[:, sl] + k_rope
            km_ref[0, hd, rows, :] = (
                (tk * gk_cos + k_partner_term) * inv_rms(tk)).astype(BF16)

    h_all = jnp.concatenate([h for h, _ in deferred], axis=0)
    kvn_all = jnp.concatenate([kvn for _, kvn in deferred], axis=0)
    vts_ref[0, 0] = _dot_nt(wsvt_ref[...], h_all).astype(BF16)
    vtm_ref[0, 0] = _dot_nt(wvt_ref[...], kvn_all).astype(BF16)


def _in0_call(x, pos3, g, w, wsvt, qan, wq, kvan, wk, wvt, gq, gqp, gk, gkp, invf):
    b, s, _ = x.shape
    tm = ROW_TILE
    nt = s // tm
    head_spec = pl.BlockSpec((1, MLA_HEADS, tm, LANES), lambda bi, i: (bi, 0, i, 0))
    vt_spec = pl.BlockSpec((1, 1, SB_WIDTH, tm), lambda bi, i: (bi, i, 0, 0))
    row_spec = pl.BlockSpec((1, tm, SB_WIDTH), lambda bi, i: (bi, i, 0))
    lane_vec = _full((1, LANES))
    return pl.pallas_call(
        _in0_kernel,
        grid=(b, nt),
        in_specs=[
            pl.BlockSpec((1, tm, D_MODEL), lambda bi, i: (bi, i, 0)),
            pl.BlockSpec((1, tm, 1), lambda bi, i: (bi, i, 0)),
            _full((1, D_MODEL)),
            _full(w.shape),
            _full(wsvt.shape),
            _full((1, MLA_Q_RANK)),
            _full(wq.shape),
            _full((1, MLA_KV_RANK)),
            _full(wk.shape),
            _full(wvt.shape),
            lane_vec, lane_vec, lane_vec, lane_vec, _full((TRIG_PACK, LANES)),
        ],
        out_specs=[head_spec, head_spec, vt_spec, row_spec, row_spec, vt_spec],
        out_shape=[
            jax.ShapeDtypeStruct((b, MLA_HEADS, s, LANES), BF16),
            jax.ShapeDtypeStruct((b, MLA_HEADS, s, LANES), BF16),
            jax.ShapeDtypeStruct((b, nt, MLA_HEADS * MLA_V, tm), BF16),
            jax.ShapeDtypeStruct((b, s, SB_WIDTH), BF16),
            jax.ShapeDtypeStruct((b, s, SB_WIDTH), BF16),
            jax.ShapeDtypeStruct((b, nt, SB_WIDTH, tm), BF16),
        ],
        compiler_params=_compiler_params(2),
        name="l0_in",
    )(x, pos3, g, w, wsvt, qan, wq, kvan, wk, wvt, gq, gqp, gk, gkp, invf)


def _diag_spans(n_chains, chain, tk):
    for c in range(n_chains):
        end = (c + 1) * chain
        for d in range(-(-end // tk)):
            nk = min(tk, end - d * tk)
            yield c, d, nk, d * tk + nk > c * chain


def _softmax_attn_kernel(q_ref, k_ref, vt_ref, o_ref, m_scr, acc_scr, pre_scr, *, head_dim):
    tq, tk, cw = Q_TILE, K_TILE, CHAIN
    n_c = tq // cw
    kt_per_q = tq // tk
    qi = pl.program_id(2)
    chains = [(hh, c) for hh in range(SOFTMAX_HEADS) for c in range(n_c)]
    qs = [q_ref[0, hh, c * cw:(c + 1) * cw, :] for hh, c in chains]

    m_scr[...] = jnp.full(m_scr.shape, NEG_INF, F32)
    acc_scr[...] = jnp.zeros(acc_scr.shape, F32)

    def with_ones(vt):
        return jnp.concatenate([vt, jnp.ones((ONES_ROWS, vt.shape[1]), vt.dtype)], axis=0)

    def update(idx, s, vt, visible):
        if visible is not None:
            s = jnp.where(visible, s, NEG_INF)
        m = m_scr[idx]
        m_new = jnp.maximum(m, jnp.max(s, axis=0, keepdims=True))
        p = jnp.exp2(s - m_new)
        m_scr[idx] = m_new
        acc_scr[idx] = jnp.exp2(m - m_new) * acc_scr[idx] + _dot(vt, p.astype(BF16))

    def scores_of(idx, tile, nk=tk):
        hh = chains[idx][0]
        return _dot_nt(k_ref[0, hh, pl.ds(pl.multiple_of(tile * tk, tk), nk), :], qs[idx])

    def run_pipeline(items, tail_fns, ahead):
        fns = [score_fn for score_fn, _ in items] + list(tail_fns)
        pending = {w: fns[w]() for w in range(min(ahead, len(fns)))}
        for v, (_, consume) in enumerate(items):
            if v + ahead < len(fns):
                pending[v + ahead] = fns[v + ahead]()
            consume(pending.pop(v))

    def prefetch_fns(tile):
        def store(a):
            def fn():
                pre_scr[a] = scores_of(a, tile)
            return fn
        return [store(a) for a in range(PREFETCH)]

    n_full = qi * kt_per_q
    items = []
    for hh in range(SOFTMAX_HEADS):
        for c, d, nk, mask in _diag_spans(n_c, cw, tk):
            idx = hh * n_c + c
            visible = None
            if mask:
                row = lax.broadcasted_iota(jnp.int32, (nk, cw), 0) + d * tk
                col = lax.broadcasted_iota(jnp.int32, (nk, cw), 1) + c * cw
                visible = row <= col

            def score_fn(idx=idx, d=d, nk=nk):
                return scores_of(idx, n_full + d, nk)

            def consume(s, idx=idx, hh=hh, d=d, nk=nk, visible=visible):
                vt = vt_ref[0, n_full + d, hh * head_dim:(hh + 1) * head_dim, 0:nk]
                update(idx, s, with_ones(vt), visible)

            items.append((score_fn, consume))
    run_pipeline(items, prefetch_fns(0), DIAG_LOOKAHEAD)

    def full_step(i, _):
        items = []
        for u in range(kt_per_q):
            j = i * kt_per_q + u
            for idx, (hh, _) in enumerate(chains):
                if u == 0 and idx < PREFETCH:
                    score_fn = lambda idx=idx: pre_scr[idx]
                else:
                    score_fn = lambda idx=idx, j=j: scores_of(idx, j)

                def consume(s, idx=idx, hh=hh, j=j):
                    vt = vt_ref[0, j, hh * head_dim:(hh + 1) * head_dim, :]
                    update(idx, s, with_ones(vt), None)

                items.append((score_fn, consume))
        run_pipeline(items, prefetch_fns(jnp.minimum(i + 1, qi - 1) * kt_per_q), LOOKAHEAD)
        return 0

    lax.fori_loop(0, qi, full_step, 0)

    def normalized(idx):
        acc = acc_scr[idx]
        return acc[:head_dim] / acc[head_dim:head_dim + 1]

    outs = [jnp.concatenate([normalized(hh * n_c + c) for c in range(n_c)], axis=1)
            for hh in range(SOFTMAX_HEADS)]
    o_ref[0] = jnp.concatenate(outs, axis=0).T.astype(o_ref.dtype)


def _softmax_attn_call(q, k, vt, name):
    b, h, s, _ = q.shape
    tq = Q_TILE
    nkt = s // K_TILE
    head_dim = vt.shape[2] // h
    hps = SOFTMAX_HEADS
    n_chains = hps * tq // CHAIN
    return pl.pallas_call(
        functools.partial(_softmax_attn_kernel, head_dim=head_dim),
        grid=(b, h // hps, s // tq),
        in_specs=[
            pl.BlockSpec((1, hps, tq, LANES), lambda bi, p, i: (bi, p, i, 0)),
            pl.BlockSpec((1, hps, s, LANES), lambda bi, p, i: (bi, p, 0, 0)),
            pl.BlockSpec((1, nkt, hps * head_dim, K_TILE), lambda bi, p, i: (bi, 0, p, 0)),
        ],
        out_specs=pl.BlockSpec((1, tq, hps * head_dim), lambda bi, p, i: (bi, i, p)),
        out_shape=jax.ShapeDtypeStruct((b, s, h * head_dim), BF16),
        scratch_shapes=[
            pltpu.VMEM((n_chains, 1, CHAIN), F32),
            pltpu.VMEM((n_chains, head_dim + ONES_ROWS, CHAIN), F32),
            pltpu.VMEM((PREFETCH, K_TILE, CHAIN), F32),
        ],
        compiler_params=_compiler_params(3),
        name=name,
    )(q, k, vt)


def _sb_attn_kernel(q_ref, k_ref, vt_ref, o_ref, run_scr, acc_scr):
    tq, tk, cw, sub = Q_TILE, K_TILE, CHAIN, SB_SUB
    n_c = tq // cw
    kt_per_q = tq // tk
    qi = pl.program_id(2)
    s_row = lax.broadcasted_iota(jnp.int32, (sub, sub), 0)
    s_col = lax.broadcasted_iota(jnp.int32, (sub, sub), 1)
    neg_later = -((s_col > s_row).astype(BF16))
    lane = _lane_iota(cw)
    chains = [(hh, c) for hh in range(2) for c in range(n_c)]
    qs = []
    for hh, c in chains:
        q_pair = q_ref[0, c * cw:(c + 1) * cw, :]
        in_head = (lane >= hh * SB_DIM) & (lane < (hh + 1) * SB_DIM)
        qs.append(jnp.where(in_head, q_pair, jnp.zeros_like(q_pair)))

    run_scr[...] = jnp.zeros(run_scr.shape, F32)
    acc_scr[...] = jnp.zeros(acc_scr.shape, F32)

    def suffix_sums(z, visible):
        cost = jnp.maximum(z, 0.0) + jnp.log(1.0 + jnp.exp2(-jnp.abs(z))) * LOG2E
        log_beta = z - cost
        if visible is not None:
            cost = jnp.where(visible, cost, 0.0)
        cost16 = cost.astype(BF16)
        blocks = [slice(b * sub, (b + 1) * sub) for b in range(z.shape[0] // sub)]
        suffix = [_dot(neg_later, cost16[bl]) for bl in blocks]
        return log_beta, cost, blocks, suffix

    def accumulate(idx, stage, vt, visible):
        log_beta, cost, blocks, suffix = stage
        run = run_scr[idx]
        terms = []
        for bl, sfx in reversed(list(zip(blocks, suffix))):
            terms.append(log_beta[bl] + sfx + run)
            run = run + sfx[0:1, :] - cost[bl.start:bl.start + 1, :]
        a = jnp.exp2(jnp.concatenate(terms[::-1], axis=0))
        if visible is not None:
            a = jnp.where(visible, a, 0.0)
        run_scr[idx] = run
        acc_scr[idx] = acc_scr[idx] + _dot(vt, a.astype(BF16))

    def tile_update(idxs, fns, vts, masks):
        n = len(idxs)
        pending, stages = {}, {}
        for w in range(min(LOOKAHEAD, len(fns))):
            pending[w] = fns[w]()
        for v in range(n + 1):
            if v + LOOKAHEAD < len(fns):
                pending[v + LOOKAHEAD] = fns[v + LOOKAHEAD]()
            if v < n:
                stages[v] = suffix_sums(pending.pop(v), masks[v])
            if v >= 1:
                accumulate(idxs[v - 1], stages.pop(v - 1), vts[v - 1], masks[v - 1])

    def scores_fn(idx, start, nk):
        return lambda: _dot_nt(k_ref[0, pl.ds(pl.multiple_of(start, tk), nk), :], qs[idx])

    n_full = qi * kt_per_q
    def still_alive(idxs):
        return functools.reduce(
            jnp.logical_or, [jnp.max(run_scr[idx]) > RUN_FLOOR for idx in idxs])

    idxs, fns, vts, masks = [], [], [], []
    for masked in (True, False):
        for hh in range(2):
            for c, d, nk, mask in _diag_spans(n_c, cw, tk):
                if mask != masked:
                    continue
                visible = None
                if mask:
                    row = lax.broadcasted_iota(jnp.int32, (nk, cw), 0) + d * tk
                    col = lax.broadcasted_iota(jnp.int32, (nk, cw), 1) + c * cw
                    visible = row < col
                idxs.append(hh * n_c + c)
                fns.append(scores_fn(hh * n_c + c, (n_full + d) * tk, nk))
                vts.append(vt_ref[0, n_full + d, hh * SB_DIM:(hh + 1) * SB_DIM, 0:nk])
                masks.append(visible)
    tile_update(idxs, fns, vts, masks)

    def loop_over_tiles(group):
        def step_and_test(carry):
            i, _ = carry
            j = n_full - 1 - i
            tile_update(group, [scores_fn(a, j * tk, tk) for a in group],
                        [vt_ref[0, j, chains[a][0] * SB_DIM:(chains[a][0] + 1) * SB_DIM, :]
                         for a in group],
                        [None] * len(group))
            return i + 1, still_alive(group)

        lax.while_loop(lambda carry: jnp.logical_and(carry[0] < n_full, carry[1]),
                       step_and_test, (jnp.int32(0), still_alive(group)))

    first_slice = [a for a, (_, c) in enumerate(chains) if c == 0]
    loop_over_tiles([a for a in range(len(chains)) if a not in first_slice])
    loop_over_tiles(first_slice)
    outs = [jnp.concatenate([acc_scr[hh * n_c + c] for c in range(n_c)], axis=1)
            for hh in range(2)]
    o_ref[0] = jnp.concatenate(outs, axis=0).T.astype(o_ref.dtype)


def _sb_attn_call(q, k, vt):
    b, s, w = q.shape
    tq = Q_TILE
    nkt = s // K_TILE
    n_chains = 2 * tq // CHAIN
    return pl.pallas_call(
        _sb_attn_kernel,
        grid=(b, w // LANES, s // tq),
        in_specs=[
            pl.BlockSpec((1, tq, LANES), lambda bi, p, i: (bi, i, p)),
            pl.BlockSpec((1, s, LANES), lambda bi, p, i: (bi, 0, p)),
            pl.BlockSpec((1, nkt, LANES, K_TILE), lambda bi, p, i: (bi, 0, p, 0)),
        ],
        out_specs=pl.BlockSpec((1, tq, LANES), lambda bi, p, i: (bi, i, p)),
        out_shape=jax.ShapeDtypeStruct((b, s, w), BF16),
        scratch_shapes=[
            pltpu.VMEM((n_chains, 1, CHAIN), F32),
            pltpu.VMEM((n_chains, SB_DIM, CHAIN), F32),
        ],
        compiler_params=_compiler_params(3),
        name="l0_sb_attn",
    )(q, k, vt)


def _out_ffn_kernel(*refs, n_mix):
    x_ref = refs[0]
    o_refs = refs[1:1 + n_mix]
    wo_refs = refs[1 + n_mix:1 + 2 * n_mix]
    g_ref, wg_ref, wu_ref, wd_ref, y_ref = refs[1 + 2 * n_mix:]
    y = x_ref[...]
    for o_ref, wo_ref in zip(o_refs, wo_refs):
        y = y + _dot(o_ref[...], wo_ref[...])
    h = _rms(y, g_ref[...], D_MODEL).astype(BF16)
    gate = _dot(h, wg_ref[...])
    up = _dot(h, wu_ref[...])
    act = gate * (1.0 / (1.0 + jnp.exp(-gate))) * up
    y_ref[...] = y + _dot(act.astype(BF16), wd_ref[...])


def _out_ffn_call(x2, mixes, wos, g, wg, wu, wd, name):
    n, d = x2.shape
    tm = FFN_TILE
    n_mix = len(mixes)
    row = lambda width: pl.BlockSpec((tm, width), lambda i: (i, 0))
    return pl.pallas_call(
        functools.partial(_out_ffn_kernel, n_mix=n_mix),
        grid=(n // tm,),
        in_specs=([row(d)] + [row(m.shape[1]) for m in mixes]
                  + [_full(w.shape) for w in wos]
                  + [_full((1, d)), _full(wg.shape), _full(wu.shape), _full(wd.shape)]),
        out_specs=row(d),
        out_shape=jax.ShapeDtypeStruct((n, d), F32),
        compiler_params=_compiler_params(1),
        name=name,
    )(x2, *mixes, *wos, g, wg, wu, wd)


def _in1_kernel(x_ref, g_ref, w_ref, wvt_ref, fb_ref, gq_ref, gk_ref, layout_ref,
                qa_ref, ka_ref, vt_ref, carry_ref):
    i = pl.program_id(1)

    @pl.when(i == 0)
    def _():
        carry_ref[...] = jnp.zeros_like(carry_ref)

    n_terms = AUG_TERMS
    used = n_terms * FOX_HEADS
    rows = IN_SUB
    lane = _lane_iota(1)
    half_lane = lane & (FOX_DIM - 1)
    low_half = lane < FOX_DIM
    term = layout_ref[...]
    decay_lane = term < n_terms
    ones_lane = term == n_terms
    own_rows = [jnp.where((half_lane >= n_terms * hd) & (half_lane < n_terms * (hd + 1)), 1.0, 0.0)
                for hd in range(FOX_HEADS)]
    r_i = lax.broadcasted_iota(jnp.int32, (rows, rows), 0)
    c_i = lax.broadcasted_iota(jnp.int32, (rows, rows), 1)
    lower = (c_i <= r_i).astype(BF16)
    g = g_ref[...]
    gq = gq_ref[...]
    gk = gk_ref[...]

    deferred = []
    for r in range(ROW_TILE // IN_SUB):
        sl_rows = slice(r * rows, (r + 1) * rows)
        h = _rms(x_ref[0, sl_rows, :], g, D_MODEL).astype(BF16)
        proj = _dot(h, w_ref[...])
        deferred.append(h)
        q = proj[:, 0:FOX_WIDTH]
        k = proj[:, FOX_WIDTH:2 * FOX_WIDTH]
        f_logit = proj[:, 2 * FOX_WIDTH:]

        log_f = jnp.where(decay_lane, _log_sigmoid(f_logit + fb_ref[...]), 0.0)
        f1, f2, f3 = _split3(log_f)
        decay = carry_ref[...] + _dot(lower, f1) + _dot(lower, f2) + _dot(lower, f3)
        carry_ref[...] = decay[rows - 1:rows, :]
        d1, d2, d3 = (d.astype(F32) for d in _split3(decay * LOG2E))
        terms = jnp.where(term == 0, d1, jnp.where(term == 1, d2, d3))
        aug_q = jnp.where(decay_lane, terms, jnp.where(ones_lane, 1.0, 0.0))
        neg_terms = -terms

        for pair in range(FOX_HEADS // 2):
            sl = slice(pair * LANES, (pair + 1) * LANES)
            qp = q[:, sl]
            kp = k[:, sl]
            qsq = qp * qp
            ksq = kp * kp
            q_gain = qp * gq
            k_gain = kp * gk
            for parity in range(2):
                hd = 2 * pair + parity
                in_head = low_half if parity == 0 else jnp.logical_not(low_half)

                def inv_rms(sq):
                    ss = jnp.sum(jnp.where(in_head, sq, 0.0), axis=-1, keepdims=True)
                    return lax.rsqrt(ss * (1.0 / FOX_DIM) + EPS)

                qa_ref[0, hd, sl_rows, :] = jnp.where(
                    in_head, q_gain * inv_rms(qsq), aug_q).astype(BF16)
                moved = pltpu.roll(neg_terms, used - n_terms * hd, 1)
                aug_k = jnp.where(ones_lane, moved, own_rows[hd])
                ka_ref[0, hd, sl_rows, :] = jnp.where(
                    in_head, k_gain * inv_rms(ksq), aug_k).astype(BF16)

    h_all = jnp.concatenate(deferred, axis=0)
    vt_ref[0, 0] = _dot_nt(wvt_ref[...], h_all).astype(BF16)


def _in1_call(x, g, w, wvt, fb, gq, gk, layout):
    b, s, _ = x.shape
    tm = ROW_TILE
    nt = s // tm
    head_spec = pl.BlockSpec((1, FOX_HEADS, tm, LANES), lambda bi, i: (bi, 0, i, 0))
    lane_vec = _full((1, LANES))
    return pl.pallas_call(
        _in1_kernel,
        grid=(b, nt),
        in_specs=[
            pl.BlockSpec((1, tm, D_MODEL), lambda bi, i: (bi, i, 0)),
            _full((1, D_MODEL)),
            _full(w.shape),
            _full(wvt.shape),
            lane_vec, lane_vec, lane_vec, lane_vec,
        ],
        out_specs=[head_spec, head_spec,
                   pl.BlockSpec((1, 1, FOX_WIDTH, tm), lambda bi, i: (bi, i, 0, 0))],
        out_shape=[
            jax.ShapeDtypeStruct((b, FOX_HEADS, s, LANES), BF16),
            jax.ShapeDtypeStruct((b, FOX_HEADS, s, LANES), BF16),
            jax.ShapeDtypeStruct((b, nt, FOX_WIDTH, tm), BF16),
        ],
        scratch_shapes=[pltpu.VMEM((1, LANES), F32)],
        compiler_params=_compiler_params(2),
        name="l1_in",
    )(x, g, w, wvt, fb, gq, gk, layout)


def _pad_lanes(v, width=LANES):
    return jnp.pad(v, (0, width - v.shape[0])).reshape(1, width)


def _rope_partner(cols):
    half = MLA_ROPE // 2
    return jnp.concatenate([-cols[..., half:], cols[..., :half]], axis=-1)


def _prep_layer0(w_in, w_q_up, w_kv_up, q_norm, k_norm):
    o = MLA_Q_RANK + MLA_KV_RANK
    pad_rope = lambda c: jnp.pad(c, ((0, 0),) * (c.ndim - 1) + ((MLA_NOPE, LANES - MLA_QK),))
    w_rope = w_in[:, o:o + MLA_ROPE]
    sb = w_in[:, o + MLA_ROPE:]
    w = jnp.concatenate([w_in[:, :o], pad_rope(w_rope), pad_rope(_rope_partner(w_rope)),
                         sb[:, :2 * SB_WIDTH]], axis=1)
    wsvt = sb[:, 2 * SB_WIDTH:].T
    wq3 = w_q_up.reshape(MLA_Q_RANK, MLA_HEADS, MLA_QK)
    wq = jnp.pad(wq3, ((0, 0), (0, 0), (0, LANES - MLA_QK)))
    wq_partner = pad_rope(_rope_partner(wq3[:, :, MLA_NOPE:]))
    wq = jnp.concatenate([wq.reshape(MLA_Q_RANK, -1), wq_partner.reshape(MLA_Q_RANK, -1)], axis=1)
    kv = w_kv_up.reshape(MLA_KV_RANK, MLA_HEADS, MLA_NOPE + MLA_V)
    wk = jnp.pad(kv[:, :, :MLA_NOPE], ((0, 0), (0, 0), (0, LANES - MLA_NOPE)))
    wk = wk.reshape(MLA_KV_RANK, MLA_HEADS * LANES)
    wvt = kv[:, :, MLA_NOPE:].reshape(MLA_KV_RANK, MLA_HEADS * MLA_V).T
    gq = q_norm * (MLA_QK ** -0.5 * LOG2E)
    half = MLA_ROPE // 2
    swap_halves = lambda r: jnp.concatenate([r[half:], r[:half]])
    gqp = pad_rope(swap_halves(gq[MLA_NOPE:]))
    gkp = pad_rope(swap_halves(k_norm[MLA_NOPE:]))
    inv_freq = ROPE_THETA ** (-jnp.arange(half, dtype=F32) / half)
    invf = pad_rope(jnp.concatenate([inv_freq, inv_freq]))
    invf = jnp.stack([jnp.roll(invf, MLA_ROPE * gi) for gi in range(TRIG_PACK)])
    lanes = lambda v: v.reshape(1, LANES)
    return (w.astype(BF16), wsvt.astype(BF16), wq.astype(BF16), wk.astype(BF16), wvt.astype(BF16),
            _pad_lanes(gq), lanes(gqp), _pad_lanes(k_norm), lanes(gkp), invf)


def _prep_layer1(w_in, f_bias, q_norm, k_norm):
    wf = w_in[:, 3 * FOX_WIDTH:]
    rep = jnp.repeat(wf, AUG_TERMS, axis=1)
    pad = jnp.zeros((wf.shape[0], FOX_DIM - AUG_TERMS * FOX_HEADS), wf.dtype)
    f_group = jnp.concatenate([rep, pad, rep, pad], axis=1)
    w = jnp.concatenate([w_in[:, :2 * FOX_WIDTH], f_group], axis=1)
    wvt = w_in[:, 2 * FOX_WIDTH:3 * FOX_WIDTH].T
    fb = jnp.repeat(f_bias, AUG_TERMS)
    fpad = jnp.zeros((FOX_DIM - AUG_TERMS * FOX_HEADS,), f_bias.dtype)
    fb = jnp.concatenate([fb, fpad, fb, fpad]).reshape(1, LANES)
    gain2 = lambda gvec: jnp.concatenate([gvec, gvec]).reshape(1, LANES)
    used = AUG_TERMS * FOX_HEADS
    half_lane = jnp.arange(LANES, dtype=jnp.int32) % FOX_DIM
    layout = jnp.where(half_lane < used, half_lane % AUG_TERMS,
                       jnp.where(half_lane < used + AUG_TERMS, AUG_TERMS, AUG_TERMS + 1))
    return (w.astype(BF16), wvt.astype(BF16), fb,
            gain2(q_norm * (FOX_DIM ** -0.5 * LOG2E)), gain2(k_norm), layout.reshape(1, LANES))


def kernel(x, positions, l0_attn_norm, l0_w_in, l0_mla_q_a_norm, l0_mla_w_q_up, l0_mla_kv_a_norm, l0_mla_w_kv_up, l0_mla_q_norm, l0_mla_k_norm, l0_w_o, l0_ffn_norm, l0_w_gate, l0_w_up, l0_w_down, l1_attn_norm, l1_w_in, l1_fox_f_bias, l1_fox_q_norm, l1_fox_k_norm, l1_w_o, l1_ffn_norm, l1_w_gate, l1_w_up, l1_w_down):
    b, s, d = x.shape
    assert d == D_MODEL and s % Q_TILE == 0 and K_TILE == ROW_TILE and Q_TILE % K_TILE == 0
    assert K_TILE % CHAIN == 0 and CHAIN % SB_SUB == 0
    row = lambda v: v.reshape(1, -1)

    w0, wsvt, wq, wk, wvt0, gq0, gqp0, gk0, gkp0, invf = _prep_layer0(
        l0_w_in, l0_mla_w_q_up, l0_mla_w_kv_up, l0_mla_q_norm, l0_mla_k_norm)
    qm, km, vtm, sq, sk, vts = _in0_call(
        x, positions.reshape(b, s, 1), row(l0_attn_norm), w0, wsvt, row(l0_mla_q_a_norm), wq,
        row(l0_mla_kv_a_norm), wk, wvt0, gq0, gqp0, gk0, gkp0, invf)
    o_mla = _softmax_attn_call(qm, km, vtm, "l0_mla_attn")
    o_sb = _sb_attn_call(sq, sk, vts)
    n_mla = MLA_HEADS * MLA_V
    x1 = _out_ffn_call(
        x.reshape(b * s, d),
        [o_mla.reshape(b * s, n_mla), o_sb.reshape(b * s, SB_WIDTH)],
        [l0_w_o[:n_mla].astype(BF16), l0_w_o[n_mla:].astype(BF16)],
        row(l0_ffn_norm), l0_w_gate.astype(BF16), l0_w_up.astype(BF16),
        l0_w_down.astype(BF16), "l0_out_ffn")

    w1, wvt1, fb, gq1, gk1, layout = _prep_layer1(
        l1_w_in, l1_fox_f_bias, l1_fox_q_norm, l1_fox_k_norm)
    qa, ka, vtf = _in1_call(
        x1.reshape(b, s, d), row(l1_attn_norm), w1, wvt1, fb, gq1, gk1, layout)
    o_fox = _softmax_attn_call(qa, ka, vtf, "l1_fox_attn")
    out = _out_ffn_call(
        x1, [o_fox.reshape(b * s, FOX_WIDTH)], [l1_w_o.astype(BF16)],
        row(l1_ffn_norm), l1_w_gate.astype(BF16), l1_w_up.astype(BF16),
        l1_w_down.astype(BF16), "l1_out_ffn")
    return out.reshape(b, s, d)
```

```python
import functools

import jax
import jax.numpy as jnp
from jax import lax
from jax.experimental import pallas as pl
from jax.experimental.pallas import tpu as pltpu

F32 = jnp.float32
BF16 = jnp.bfloat16

D_MODEL = 1024
EPS = 1e-6
NEG_INF = -1e30
ROPE_THETA = 10000.0

MLA_HEADS = 8
MLA_Q_RANK = 256
MLA_KV_RANK = 128
MLA_NOPE = 64
MLA_ROPE = 32
MLA_QK = MLA_NOPE + MLA_ROPE
MLA_V = 64
SB_HEADS = 8
SB_DIM = 64
SB_WIDTH = SB_HEADS * SB_DIM
FOX_HEADS = 16
FOX_DIM = 64
FOX_WIDTH = FOX_HEADS * FOX_DIM

LANES = 128
V7X_VMEM_BYTES = 64 * 1024 * 1024
VMEM_LIMIT = V7X_VMEM_BYTES * 7 // 8

ROW_TILE = 512
K_TILE = 512
Q_TILE = 1024
SOFTMAX_HEADS = 8
CHAIN = 256
FFN_TILE = 512
PREFETCH = 2
LOOKAHEAD = 3
DIAG_LOOKAHEAD = 7
SB_LOOKAHEAD = 3
ONES_ROWS = 16
SB_SUB = 128
LOG2E = 1.4426950408889634
RUN_FLOOR = -160.0

AUG_TERMS = 3
TRIG_PACK = LANES // MLA_ROPE
IN_SUB = 256


def _compiler_params(n_axes):
    return pltpu.CompilerParams(
        dimension_semantics=("arbitrary",) * n_axes,
        vmem_limit_bytes=VMEM_LIMIT,
    )


def _full(shape):
    return pl.BlockSpec(shape, lambda *_: (0,) * len(shape), pipeline_mode=pl.Buffered(1))


def _lane_iota(rows):
    return lax.broadcasted_iota(jnp.int32, (rows, LANES), 1)


def _rms(x, gain, width):
    ss = jnp.sum(x * x, axis=-1, keepdims=True)
    return x * lax.rsqrt(ss * (1.0 / width) + EPS) * gain


def _log_sigmoid(z):
    return jnp.minimum(z, 0.0) - jnp.log(1.0 + jnp.exp(-jnp.abs(z)))


def _split3(x):
    a = x.astype(BF16)
    r = x - a.astype(F32)
    b = r.astype(BF16)
    c = (r - b.astype(F32)).astype(BF16)
    return a, b, c


def _dot(a, b):
    return jnp.dot(a, b, preferred_element_type=F32)


def _dot_nt(a, b):
    return lax.dot_general(a, b, (((1,), (1,)), ((), ())), preferred_element_type=F32)


def _in0_kernel(x_ref, pos_ref, g_ref, w_ref, wsvt_ref, qan_ref, wq_ref, kvan_ref, wk_ref,
                wvt_ref, gq_ref, gqp_ref, gk_ref, gkp_ref, invf_ref,
                qm_ref, km_ref, vtm_ref, sq_ref, sk_ref, vts_ref):
    g = g_ref[...]
    lane = _lane_iota(1)
    rope_lane = (lane >= MLA_NOPE) & (lane < MLA_QK)
    deferred = []
    for r in range(ROW_TILE // IN_SUB):
        rows = slice(r * IN_SUB, (r + 1) * IN_SUB)
        h = _rms(x_ref[0, rows, :], g, D_MODEL).astype(BF16)
        proj = _dot(h, w_ref[...])
        o = MLA_Q_RANK + MLA_KV_RANK
        cq = proj[:, 0:MLA_Q_RANK]
        ckv = proj[:, MLA_Q_RANK:o]
        k_rope = proj[:, o:o + LANES]
        k_rope_partner = proj[:, o + LANES:o + 2 * LANES]
        o += 2 * LANES
        sq_ref[0, rows, :] = (proj[:, o:o + SB_WIDTH] * (SB_DIM ** -0.5 * LOG2E)).astype(BF16)
        sk_ref[0, rows, :] = proj[:, o + SB_WIDTH:o + 2 * SB_WIDTH].astype(BF16)

        qn = _rms(cq, qan_ref[...], MLA_Q_RANK).astype(BF16)
        q_all = _dot(qn, wq_ref[...])
        kvn = _rms(ckv, kvan_ref[...], MLA_KV_RANK).astype(BF16)
        k_all = _dot(kvn, wk_ref[...])
        deferred.append((h, kvn))

        pos = pos_ref[0, rows, :].astype(F32)
        group = IN_SUB // TRIG_PACK
        packed = pos[0:group] * invf_ref[0:1, :]
        for gi in range(1, TRIG_PACK):
            packed = packed + pos[gi * group:(gi + 1) * group] * invf_ref[gi:gi + 1, :]

        def unpack(t, fill):
            parts = [t] + [pltpu.roll(t, LANES - MLA_ROPE * gi, 1) for gi in range(1, TRIG_PACK)]
            return jnp.where(rope_lane, jnp.concatenate(parts, axis=0), fill)

        cos = unpack(jnp.cos(packed), 1.0)
        sin = unpack(jnp.sin(packed), 0.0)
        gq_cos = gq_ref[...] * cos
        gq_sin = gqp_ref[...] * sin
        gk_cos = gk_ref[...] * cos
        k_partner_term = k_rope_partner * (gkp_ref[...] * sin)

        def inv_rms(t):
            ss = jnp.sum(t * t, axis=-1, keepdims=True)
            return lax.rsqrt(ss * (1.0 / MLA_QK) + EPS)

        for hd in range(MLA_HEADS):
            sl = slice(hd * LANES, (hd + 1) * LANES)
            t = q_all[:, sl]
            t_partner = q_all[:, MLA_HEADS * LANES + hd * LANES:MLA_HEADS * LANES + (hd + 1) * LANES]
            qm_ref[0, hd, rows, :] = (
                (t * gq_cos + t_partner * gq_sin) * inv_rms(t)).astype(BF16)
            tk = k_all[:, sl] + k_rope
            km_ref[0, hd, rows, :] = (
                (tk * gk_cos + k_partner_term) * inv_rms(tk)).astype(BF16)

    h_all = jnp.concatenate([h for h, _ in deferred], axis=0)
    kvn_all = jnp.concatenate([kvn for _, kvn in deferred], axis=0)
    vts_ref[0, 0] = _dot_nt(wsvt_ref[...], h_all).astype(BF16)
    vtm_ref[0, 0] = _dot_nt(wvt_ref[...], kvn_all).astype(BF16)


def _in0_call(x, pos3, g, w, wsvt, qan, wq, kvan, wk, wvt, gq, gqp, gk, gkp, invf):
    b, s, _ = x.shape
    tm = ROW_TILE
    nt = s // tm
    head_spec = pl.BlockSpec((1, MLA_HEADS, tm, LANES), lambda bi, i: (bi, 0, i, 0))
    vt_spec = pl.BlockSpec((1, 1, SB_WIDTH, tm), lambda bi, i: (bi, i, 0, 0))
    row_spec = pl.BlockSpec((1, tm, SB_WIDTH), lambda bi, i: (bi, i, 0))
    lane_vec = _full((1, LANES))
    return pl.pallas_call(
        _in0_kernel,
        grid=(b, nt),
        in_specs=[
            pl.BlockSpec((1, tm, D_MODEL), lambda bi, i: (bi, i, 0)),
            pl.BlockSpec((1, tm, 1), lambda bi, i: (bi, i, 0)),
            _full((1, D_MODEL)),
            _full(w.shape),
            _full(wsvt.shape),
            _full((1, MLA_Q_RANK)),
            _full(wq.shape),
            _full((1, MLA_KV_RANK)),
            _full(wk.shape),
            _full(wvt.shape),
            lane_vec, lane_vec, lane_vec, lane_vec, _full((TRIG_PACK, LANES)),
        ],
        out_specs=[head_spec, head_spec, vt_spec, row_spec, row_spec, vt_spec],
        out_shape=[
            jax.ShapeDtypeStruct((b, MLA_HEADS, s, LANES), BF16),
            jax.ShapeDtypeStruct((b, MLA_HEADS, s, LANES), BF16),
            jax.ShapeDtypeStruct((b, nt, MLA_HEADS * MLA_V, tm), BF16),
            jax.ShapeDtypeStruct((b, s, SB_WIDTH), BF16),
            jax.ShapeDtypeStruct((b, s, SB_WIDTH), BF16),
            jax.ShapeDtypeStruct((b, nt, SB_WIDTH, tm), BF16),
        ],
        compiler_params=_compiler_params(2),
        name="l0_in",
    )(x, pos3, g, w, wsvt, qan, wq, kvan, wk, wvt, gq, gqp, gk, gkp, invf)


def _diag_spans(n_chains, chain, tk):
    for c in range(n_chains):
        end = (c + 1) * chain
        for d in range(-(-end // tk)):
            nk = min(tk, end - d * tk)
            yield c, d, nk, d * tk + nk > c * chain


def _softmax_attn_kernel(q_ref, k_ref, vt_ref, o_ref, m_scr, acc_scr, pre_scr, *, head_dim):
    tq, tk, cw = Q_TILE, K_TILE, CHAIN
    n_c = tq // cw
    kt_per_q = tq // tk
    qi = pl.program_id(2)
    chains = [(hh, c) for hh in range(SOFTMAX_HEADS) for c in range(n_c)]
    qs = [q_ref[0, hh, c * cw:(c + 1) * cw, :] for hh, c in chains]

    m_scr[...] = jnp.full(m_scr.shape, NEG_INF, F32)
    acc_scr[...] = jnp.zeros(acc_scr.shape, F32)

    def with_ones(vt):
        return jnp.concatenate([vt, jnp.ones((ONES_ROWS, vt.shape[1]), vt.dtype)], axis=0)

    def update(idx, s, vt, visible):
        if visible is not None:
            s = jnp.where(visible, s, NEG_INF)
        m = m_scr[idx]
        m_new = jnp.maximum(m, jnp.max(s, axis=0, keepdims=True))
        p = jnp.exp2(s - m_new)
        m_scr[idx] = m_new
        acc_scr[idx] = jnp.exp2(m - m_new) * acc_scr[idx] + _dot(vt, p.astype(BF16))

    def scores_of(idx, tile, nk=tk):
        hh = chains[idx][0]
        return _dot_nt(k_ref[0, hh, pl.ds(pl.multiple_of(tile * tk, tk), nk), :], qs[idx])

    def run_pipeline(items, tail_fns, ahead):
        fns = [score_fn for score_fn, _ in items] + list(tail_fns)
        pending = {w: fns[w]() for w in range(min(ahead, len(fns)))}
        for v, (_, consume) in enumerate(items):
            if v + ahead < len(fns):
                pending[v + ahead] = fns[v + ahead]()
            consume(pending.pop(v))

    def prefetch_fns(tile):
        def store(a):
            def fn():
                pre_scr[a] = scores_of(a, tile)
            return fn
        return [store(a) for a in range(PREFETCH)]

    n_full = qi * kt_per_q
    items = []
    for hh in range(SOFTMAX_HEADS):
        for c, d, nk, mask in _diag_spans(n_c, cw, tk):
            idx = hh * n_c + c
            visible = None
            if mask:
                row = lax.broadcasted_iota(jnp.int32, (nk, cw), 0) + d * tk
                col = lax.broadcasted_iota(jnp.int32, (nk, cw), 1) + c * cw
                visible = row <= col

            def score_fn(idx=idx, d=d, nk=nk):
                return scores_of(idx, n_full + d, nk)

            def consume(s, idx=idx, hh=hh, d=d, nk=nk, visible=visible):
                vt = vt_ref[0, n_full + d, hh * head_dim:(hh + 1) * head_dim, 0:nk]
                update(idx, s, with_ones(vt), visible)

            items.append((score_fn, consume))
    run_pipeline(items, prefetch_fns(0), DIAG_LOOKAHEAD)

    def full_step(i, _):
        items = []
        for u in range(kt_per_q):
            j = i * kt_per_q + u
            for idx, (hh, _) in enumerate(chains):
                if u == 0 and idx < PREFETCH:
                    score_fn = lambda idx=idx: pre_scr[idx]
                else:
                    score_fn = lambda idx=idx, j=j: scores_of(idx, j)

                def consume(s, idx=idx, hh=hh, j=j):
                    vt = vt_ref[0, j, hh * head_dim:(hh + 1) * head_dim, :]
                    update(idx, s, with_ones(vt), None)

                items.append((score_fn, consume))
        run_pipeline(items, prefetch_fns(jnp.minimum(i + 1, qi - 1) * kt_per_q), LOOKAHEAD)
        return 0

    lax.fori_loop(0, qi, full_step, 0)

    def normalized(idx):
        acc = acc_scr[idx]
        return acc[:head_dim] / acc[head_dim:head_dim + 1]

    outs = [jnp.concatenate([normalized(hh * n_c + c) for c in range(n_c)], axis=1)
            for hh in range(SOFTMAX_HEADS)]
    o_ref[0] = jnp.concatenate(outs, axis=0).T.astype(o_ref.dtype)


def _softmax_attn_call(q, k, vt, name):
    b, h, s, _ = q.shape
    tq = Q_TILE
    nkt = s // K_TILE
    head_dim = vt.shape[2] // h
    hps = SOFTMAX_HEADS
    n_chains = hps * tq // CHAIN
    return pl.pallas_call(
        functools.partial(_softmax_attn_kernel, head_dim=head_dim),
        grid=(b, h // hps, s // tq),
        in_specs=[
            pl.BlockSpec((1, hps, tq, LANES), lambda bi, p, i: (bi, p, i, 0)),
            pl.BlockSpec((1, hps, s, LANES), lambda bi, p, i: (bi, p, 0, 0)),
            pl.BlockSpec((1, nkt, hps * head_dim, K_TILE), lambda bi, p, i: (bi, 0, p, 0)),
        ],
        out_specs=pl.BlockSpec((1, tq, hps * head_dim), lambda bi, p, i: (bi, i, p)),
        out_shape=jax.ShapeDtypeStruct((b, s, h * head_dim), BF16),
        scratch_shapes=[
            pltpu.VMEM((n_chains, 1, CHAIN), F32),
            pltpu.VMEM((n_chains, head_dim + ONES_ROWS, CHAIN), F32),
            pltpu.VMEM((PREFETCH, K_TILE, CHAIN), F32),
        ],
        compiler_params=_compiler_params(3),
        name=name,
    )(q, k, vt)


def _sb_attn_kernel(q_ref, k_ref, vt_ref, o_ref, run_scr, acc_scr):
    tq, tk, cw, sub = Q_TILE, K_TILE, CHAIN, SB_SUB
    n_c = tq // cw
    kt_per_q = tq // tk
    qi = pl.program_id(2)
    s_row = lax.broadcasted_iota(jnp.int32, (sub, sub), 0)
    s_col = lax.broadcasted_iota(jnp.int32, (sub, sub), 1)
    neg_later = -((s_col > s_row).astype(BF16))
    lane = _lane_iota(cw)
    chains = [(hh, c) for hh in range(2) for c in range(n_c)]
    qs = []
    for hh, c in chains:
        q_pair = q_ref[0, c * cw:(c + 1) * cw, :]
        in_head = (lane >= hh * SB_DIM) & (lane < (hh + 1) * SB_DIM)
        qs.append(jnp.where(in_head, q_pair, jnp.zeros_like(q_pair)))

    run_scr[...] = jnp.zeros(run_scr.shape, F32)
    acc_scr[...] = jnp.zeros(acc_scr.shape, F32)

    def suffix_sums(z, visible):
        cost = jnp.maximum(z, 0.0) + jnp.log(1.0 + jnp.exp2(-jnp.abs(z))) * LOG2E
        log_beta = z - cost
        if visible is not None:
            cost = jnp.where(visible, cost, 0.0)
        cost16 = cost.astype(BF16)
        blocks = [slice(b * sub, (b + 1) * sub) for b in range(z.shape[0] // sub)]
        suffix = [_dot(neg_later, cost16[bl]) for bl in blocks]
        return log_beta, cost, blocks, suffix

    def accumulate(idx, stage, vt, visible):
        log_beta, cost, blocks, suffix = stage
        run = run_scr[idx]
        terms = []
        for bl, sfx in reversed(list(zip(blocks, suffix))):
            terms.append(log_beta[bl] + sfx + run)
            run = run + sfx[0:1, :] - cost[bl.start:bl.start + 1, :]
        a = jnp.exp2(jnp.concatenate(terms[::-1], axis=0))
        if visible is not None:
            a = jnp.where(visible, a, 0.0)
        run_scr[idx] = run
        acc_scr[idx] = acc_scr[idx] + _dot(vt, a.astype(BF16))

    def tile_update(idxs, fns, vts, masks):
        n = len(idxs)
        pending, stages = {}, {}
        for w in range(min(SB_LOOKAHEAD, len(fns))):
            pending[w] = fns[w]()
        for v in range(n + 1):
            if v + SB_LOOKAHEAD < len(fns):
                pending[v + SB_LOOKAHEAD] = fns[v + SB_LOOKAHEAD]()
            if v < n:
                stages[v] = suffix_sums(pending.pop(v), masks[v])
            if v >= 1:
                accumulate(idxs[v - 1], stages.pop(v - 1), vts[v - 1], masks[v - 1])

    def scores_fn(idx, start, nk):
        return lambda: _dot_nt(k_ref[0, pl.ds(pl.multiple_of(start, cw), nk), :], qs[idx])

    q0 = qi * tq
    tri_row = lax.broadcasted_iota(jnp.int32, (cw, cw), 0)
    tri_col = lax.broadcasted_iota(jnp.int32, (cw, cw), 1)
    idxs, fns, vts, masks = [], [], [], []
    covered_from = {}
    for phase in range(2):
        for hh in range(2):
            for c in range(n_c):
                idx = hh * n_c + c
                rel = (c - phase) * cw
                start = jnp.maximum(q0 + rel, 0)
                if phase == 0:
                    visible = tri_row < tri_col
                elif rel < 0:
                    visible = jnp.broadcast_to(qi > 0, (cw, cw))
                else:
                    visible = None
                idxs.append(idx)
                fns.append(scores_fn(idx, start, cw))
                vts.append(vt_ref[0, start // tk, hh * SB_DIM:(hh + 1) * SB_DIM,
                                  rel % tk:rel % tk + cw])
                masks.append(visible)
                covered_from[idx] = q0 + rel
    tile_update(idxs, fns, vts, masks)

    all_idx = list(range(len(chains)))
    key_row = lax.broadcasted_iota(jnp.int32, (tk, cw), 0)

    def alive_at(j):
        flags = [jnp.logical_and(covered_from[idx] > j * tk, jnp.max(run_scr[idx]) > RUN_FLOOR)
                 for idx in all_idx]
        return functools.reduce(jnp.logical_or, flags)

    def step_and_test(carry):
        j, _ = carry
        tile_update(all_idx, [scores_fn(idx, j * tk, tk) for idx in all_idx],
                    [vt_ref[0, j, chains[idx][0] * SB_DIM:(chains[idx][0] + 1) * SB_DIM, :]
                     for idx in all_idx],
                    [key_row + j * tk < covered_from[idx] for idx in all_idx])
        return j - 1, alive_at(j - 1)

    top = qi * kt_per_q + ((n_c - 2) * cw - 1) // tk
    lax.while_loop(lambda carry: jnp.logical_and(carry[0] >= 0, carry[1]),
                   step_and_test, (top, alive_at(top)))
    outs = [jnp.concatenate([acc_scr[hh * n_c + c] for c in range(n_c)], axis=1)
            for hh in range(2)]
    o_ref[0] = jnp.concatenate(outs, axis=0).T.astype(o_ref.dtype)


def _sb_attn_call(q, k, vt):
    b, s, w = q.shape
    tq = Q_TILE
    nkt = s // K_TILE
    n_chains = 2 * tq // CHAIN
    return pl.pallas_call(
        _sb_attn_kernel,
        grid=(b, w // LANES, s // tq),
        in_specs=[
            pl.BlockSpec((1, tq, LANES), lambda bi, p, i: (bi, i, p)),
            pl.BlockSpec((1, s, LANES), lambda bi, p, i: (bi, 0, p)),
            pl.BlockSpec((1, nkt, LANES, K_TILE), lambda bi, p, i: (bi, 0, p, 0)),
        ],
        out_specs=pl.BlockSpec((1, tq, LANES), lambda bi, p, i: (bi, i, p)),
        out_shape=jax.ShapeDtypeStruct((b, s, w), BF16),
        scratch_shapes=[
            pltpu.VMEM((n_chains, 1, CHAIN), F32),
            pltpu.VMEM((n_chains, SB_DIM, CHAIN), F32),
        ],
        compiler_params=_compiler_params(3),
        name="l0_sb_attn",
    )(q, k, vt)


def _out_ffn_kernel(*refs, n_mix):
    x_ref = refs[0]
    o_refs = refs[1:1 + n_mix]
    wo_refs = refs[1 + n_mix:1 + 2 * n_mix]
    g_ref, wg_ref, wu_ref, wd_ref, y_ref = refs[1 + 2 * n_mix:]
    y = x_ref[...]
    for o_ref, wo_ref in zip(o_refs, wo_refs):
        y = y + _dot(o_ref[...], wo_ref[...])
    h = _rms(y, g_ref[...], D_MODEL).astype(BF16)
    gate = _dot(h, wg_ref[...])
    up = _dot(h, wu_ref[...])
    act = gate * (1.0 / (1.0 + jnp.exp(-gate))) * up
    y_ref[...] = y + _dot(act.astype(BF16), wd_ref[...])


def _out_ffn_call(x2, mixes, wos, g, wg, wu, wd, name):
    n, d = x2.shape
    tm = FFN_TILE
    n_mix = len(mixes)
    row = lambda width: pl.BlockSpec((tm, width), lambda i: (i, 0))
    return pl.pallas_call(
        functools.partial(_out_ffn_kernel, n_mix=n_mix),
        grid=(n // tm,),
        in_specs=([row(d)] + [row(m.shape[1]) for m in mixes]
                  + [_full(w.shape) for w in wos]
                  + [_full((1, d)), _full(wg.shape), _full(wu.shape), _full(wd.shape)]),
        out_specs=row(d),
        out_shape=jax.ShapeDtypeStruct((n, d), F32),
        compiler_params=_compiler_params(1),
        name=name,
    )(x2, *mixes, *wos, g, wg, wu, wd)


def _in1_kernel(x_ref, g_ref, w_ref, wvt_ref, fb_ref, gq_ref, gk_ref, layout_ref,
                qa_ref, ka_ref, vt_ref, carry_ref):
    i = pl.program_id(1)

    @pl.when(i == 0)
    def _():
        carry_ref[...] = jnp.zeros_like(carry_ref)

    n_terms = AUG_TERMS
    used = n_terms * FOX_HEADS
    rows = IN_SUB
    lane = _lane_iota(1)
    half_lane = lane & (FOX_DIM - 1)
    low_half = lane < FOX_DIM
    term = layout_ref[...]
    decay_lane = term < n_terms
    ones_lane = term == n_terms
    own_rows = [jnp.where((half_lane >= n_terms * hd) & (half_lane < n_terms * (hd + 1)), 1.0, 0.0)
                for hd in range(FOX_HEADS)]
    r_i = lax.broadcasted_iota(jnp.int32, (rows, rows), 0)
    c_i = lax.broadcasted_iota(jnp.int32, (rows, rows), 1)
    lower = (c_i <= r_i).astype(BF16)
    g = g_ref[...]
    gq = gq_ref[...]
    gk = gk_ref[...]

    deferred = []
    for r in range(ROW_TILE // IN_SUB):
        sl_rows = slice(r * rows, (r + 1) * rows)
        h = _rms(x_ref[0, sl_rows, :], g, D_MODEL).astype(BF16)
        proj = _dot(h, w_ref[...])
        deferred.append(h)
        q = proj[:, 0:FOX_WIDTH]
        k = proj[:, FOX_WIDTH:2 * FOX_WIDTH]
        f_logit = proj[:, 2 * FOX_WIDTH:]

        log_f = jnp.where(decay_lane, _log_sigmoid(f_logit + fb_ref[...]), 0.0)
        f1, f2, f3 = _split3(log_f)
        decay = carry_ref[...] + _dot(lower, f1) + _dot(lower, f2) + _dot(lower, f3)
        carry_ref[...] = decay[rows - 1:rows, :]
        d1, d2, d3 = (d.astype(F32) for d in _split3(decay * LOG2E))
        terms = jnp.where(term == 0, d1, jnp.where(term == 1, d2, d3))
        aug_q = jnp.where(decay_lane, terms, jnp.where(ones_lane, 1.0, 0.0))
        neg_terms = -terms

        for pair in range(FOX_HEADS // 2):
            sl = slice(pair * LANES, (pair + 1) * LANES)
            qp = q[:, sl]
            kp = k[:, sl]
            qsq = qp * qp
            ksq = kp * kp
            q_gain = qp * gq
            k_gain = kp * gk
            for parity in range(2):
                hd = 2 * pair + parity
                in_head = low_half if parity == 0 else jnp.logical_not(low_half)

                def inv_rms(sq):
                    ss = jnp.sum(jnp.where(in_head, sq, 0.0), axis=-1, keepdims=True)
                    return lax.rsqrt(ss * (1.0 / FOX_DIM) + EPS)

                qa_ref[0, hd, sl_rows, :] = jnp.where(
                    in_head, q_gain * inv_rms(qsq), aug_q).astype(BF16)
                moved = pltpu.roll(neg_terms, used - n_terms * hd, 1)
                aug_k = jnp.where(ones_lane, moved, own_rows[hd])
                ka_ref[0, hd, sl_rows, :] = jnp.where(
                    in_head, k_gain * inv_rms(ksq), aug_k).astype(BF16)

    h_all = jnp.concatenate(deferred, axis=0)
    vt_ref[0, 0] = _dot_nt(wvt_ref[...], h_all).astype(BF16)


def _in1_call(x, g, w, wvt, fb, gq, gk, layout):
    b, s, _ = x.shape
    tm = ROW_TILE
    nt = s // tm
    head_spec = pl.BlockSpec((1, FOX_HEADS, tm, LANES), lambda bi, i: (bi, 0, i, 0))
    lane_vec = _full((1, LANES))
    return pl.pallas_call(
        _in1_kernel,
        grid=(b, nt),
        in_specs=[
            pl.BlockSpec((1, tm, D_MODEL), lambda bi, i: (bi, i, 0)),
            _full((1, D_MODEL)),
            _full(w.shape),
            _full(wvt.shape),
            lane_vec, lane_vec, lane_vec, lane_vec,
        ],
        out_specs=[head_spec, head_spec,
                   pl.BlockSpec((1, 1, FOX_WIDTH, tm), lambda bi, i: (bi, i, 0, 0))],
        out_shape=[
            jax.ShapeDtypeStruct((b, FOX_HEADS, s, LANES), BF16),
            jax.ShapeDtypeStruct((b, FOX_HEADS, s, LANES), BF16),
            jax.ShapeDtypeStruct((b, nt, FOX_WIDTH, tm), BF16),
        ],
        scratch_shapes=[pltpu.VMEM((1, LANES), F32)],
        compiler_params=_compiler_params(2),
        name="l1_in",
    )(x, g, w, wvt, fb, gq, gk, layout)


def _pad_lanes(v, width=LANES):
    return jnp.pad(v, (0, width - v.shape[0])).reshape(1, width)


def _rope_partner(cols):
    half = MLA_ROPE // 2
    return jnp.concatenate([-cols[..., half:], cols[..., :half]], axis=-1)


def _prep_layer0(w_in, w_q_up, w_kv_up, q_norm, k_norm):
    o = MLA_Q_RANK + MLA_KV_RANK
    pad_rope = lambda c: jnp.pad(c, ((0, 0),) * (c.ndim - 1) + ((MLA_NOPE, LANES - MLA_QK),))
    w_rope = w_in[:, o:o + MLA_ROPE]
    sb = w_in[:, o + MLA_ROPE:]
    w = jnp.concatenate([w_in[:, :o], pad_rope(w_rope), pad_rope(_rope_partner(w_rope)),
                         sb[:, :2 * SB_WIDTH]], axis=1)
    wsvt = sb[:, 2 * SB_WIDTH:].T
    wq3 = w_q_up.reshape(MLA_Q_RANK, MLA_HEADS, MLA_QK)
    wq = jnp.pad(wq3, ((0, 0), (0, 0), (0, LANES - MLA_QK)))
    wq_partner = pad_rope(_rope_partner(wq3[:, :, MLA_NOPE:]))
    wq = jnp.concatenate([wq.reshape(MLA_Q_RANK, -1), wq_partner.reshape(MLA_Q_RANK, -1)], axis=1)
    kv = w_kv_up.reshape(MLA_KV_RANK, MLA_HEADS, MLA_NOPE + MLA_V)
    wk = jnp.pad(kv[:, :, :MLA_NOPE], ((0, 0), (0, 0), (0, LANES - MLA_NOPE)))
    wk = wk.reshape(MLA_KV_RANK, MLA_HEADS * LANES)
    wvt = kv[:, :, MLA_NOPE:].reshape(MLA_KV_RANK, MLA_HEADS * MLA_V).T
    gq = q_norm * (MLA_QK ** -0.5 * LOG2E)
    half = MLA_ROPE // 2
    swap_halves = lambda r: jnp.concatenate([r[half:], r[:half]])
    gqp = pad_rope(swap_halves(gq[MLA_NOPE:]))
    gkp = pad_rope(swap_halves(k_norm[MLA_NOPE:]))
    inv_freq = ROPE_THETA ** (-jnp.arange(half, dtype=F32) / half)
    invf = pad_rope(jnp.concatenate([inv_freq, inv_freq]))
    invf = jnp.stack([jnp.roll(invf, MLA_ROPE * gi) for gi in range(TRIG_PACK)])
    lanes = lambda v: v.reshape(1, LANES)
    return (w.astype(BF16), wsvt.astype(BF16), wq.astype(BF16), wk.astype(BF16), wvt.astype(BF16),
            _pad_lanes(gq), lanes(gqp), _pad_lanes(k_norm), lanes(gkp), invf)


def _prep_layer1(w_in, f_bias, q_norm, k_norm):
    wf = w_in[:, 3 * FOX_WIDTH:]
    rep = jnp.repeat(wf, AUG_TERMS, axis=1)
    pad = jnp.zeros((wf.shape[0], FOX_DIM - AUG_TERMS * FOX_HEADS), wf.dtype)
    f_group = jnp.concatenate([rep, pad, rep, pad], axis=1)
    w = jnp.concatenate([w_in[:, :2 * FOX_WIDTH], f_group], axis=1)
    wvt = w_in[:, 2 * FOX_WIDTH:3 * FOX_WIDTH].T
    fb = jnp.repeat(f_bias, AUG_TERMS)
    fpad = jnp.zeros((FOX_DIM - AUG_TERMS * FOX_HEADS,), f_bias.dtype)
    fb = jnp.concatenate([fb, fpad, fb, fpad]).reshape(1, LANES)
    gain2 = lambda gvec: jnp.concatenate([gvec, gvec]).reshape(1, LANES)
    used = AUG_TERMS * FOX_HEADS
    half_lane = jnp.arange(LANES, dtype=jnp.int32) % FOX_DIM
    layout = jnp.where(half_lane < used, half_lane % AUG_TERMS,
                       jnp.where(half_lane < used + AUG_TERMS, AUG_TERMS, AUG_TERMS + 1))
    return (w.astype(BF16), wvt.astype(BF16), fb,
            gain2(q_norm * (FOX_DIM ** -0.5 * LOG2E)), gain2(k_norm), layout.reshape(1, LANES))


def kernel(x, positions, l0_attn_norm, l0_w_in, l0_mla_q_a_norm, l0_mla_w_q_up, l0_mla_kv_a_norm, l0_mla_w_kv_up, l0_mla_q_norm, l0_mla_k_norm, l0_w_o, l0_ffn_norm, l0_w_gate, l0_w_up, l0_w_down, l1_attn_norm, l1_w_in, l1_fox_f_bias, l1_fox_q_norm, l1_fox_k_norm, l1_w_o, l1_ffn_norm, l1_w_gate, l1_w_up, l1_w_down):
    b, s, d = x.shape
    assert d == D_MODEL and s % Q_TILE == 0 and K_TILE == ROW_TILE and Q_TILE % K_TILE == 0
    assert K_TILE % CHAIN == 0 and CHAIN % SB_SUB == 0
    row = lambda v: v.reshape(1, -1)

    w0, wsvt, wq, wk, wvt0, gq0, gqp0, gk0, gkp0, invf = _prep_layer0(
        l0_w_in, l0_mla_w_q_up, l0_mla_w_kv_up, l0_mla_q_norm, l0_mla_k_norm)
    qm, km, vtm, sq, sk, vts = _in0_call(
        x, positions.reshape(b, s, 1), row(l0_attn_norm), w0, wsvt, row(l0_mla_q_a_norm), wq,
        row(l0_mla_kv_a_norm), wk, wvt0, gq0, gqp0, gk0, gkp0, invf)
    o_mla = _softmax_attn_call(qm, km, vtm, "l0_mla_attn")
    o_sb = _sb_attn_call(sq, sk, vts)
    n_mla = MLA_HEADS * MLA_V
    x1 = _out_ffn_call(
        x.reshape(b * s, d),
        [o_mla.reshape(b * s, n_mla), o_sb.reshape(b * s, SB_WIDTH)],
        [l0_w_o[:n_mla].astype(BF16), l0_w_o[n_mla:].astype(BF16)],
        row(l0_ffn_norm), l0_w_gate.astype(BF16), l0_w_up.astype(BF16),
        l0_w_down.astype(BF16), "l0_out_ffn")

    w1, wvt1, fb, gq1, gk1, layout = _prep_layer1(
        l1_w_in, l1_fox_f_bias, l1_fox_q_norm, l1_fox_k_norm)
    qa, ka, vtf = _in1_call(
        x1.reshape(b, s, d), row(l1_attn_norm), w1, wvt1, fb, gq1, gk1, layout)
    o_fox = _softmax_attn_call(qa, ka, vtf, "l1_fox_attn")
    out = _out_ffn_call(
        x1, [o_fox.reshape(b * s, FOX_WIDTH)], [l1_w_o.astype(BF16)],
        row(l1_ffn_norm), l1_w_gate.astype(BF16), l1_w_up.astype(BF16),
        l1_w_down.astype(BF16), "l1_out_ffn")
    return out.reshape(b, s, d)
```

```python
import functools

import jax
import jax.numpy as jnp
from jax import lax
from jax.experimental import pallas as pl
from jax.experimental.pallas import tpu as pltpu

F32 = jnp.float32
BF16 = jnp.bfloat16

D_MODEL = 1024
EPS = 1e-6
NEG_INF = -1e30
ROPE_THETA = 10000.0

MLA_HEADS = 8
MLA_Q_RANK = 256
MLA_KV_RANK = 128
MLA_NOPE = 64
MLA_ROPE = 32
MLA_QK = MLA_NOPE + MLA_ROPE
MLA_V = 64
SB_HEADS = 8
SB_DIM = 64
SB_WIDTH = SB_HEADS * SB_DIM
FOX_HEADS = 16
FOX_DIM = 64
FOX_WIDTH = FOX_HEADS * FOX_DIM

LANES = 128
BF16_SUBLANES = 16
V7X_VMEM_BYTES = 64 * 1024 * 1024
VMEM_LIMIT = V7X_VMEM_BYTES * 7 // 8

ROW_TILE = 512
K_TILE = 512
Q_TILE = 1024
SOFTMAX_HEADS = 8
CHAIN = 256
FFN_TILE = 512
PREFETCH = 2
LOOKAHEAD = 3
DIAG_LOOKAHEAD = 7
SB_LOOKAHEAD = 3
ONES_ROWS = 16
SB_SUB = 128
LOG2E = 1.4426950408889634
RUN_FLOOR = -160.0

AUG_TERMS = 3
TRIG_PACK = LANES // MLA_ROPE
IN_SUB = 256


def _compiler_params(n_axes):
    return pltpu.CompilerParams(
        dimension_semantics=("arbitrary",) * n_axes,
        vmem_limit_bytes=VMEM_LIMIT,
    )


def _full(shape):
    return pl.BlockSpec(shape, lambda *_: (0,) * len(shape), pipeline_mode=pl.Buffered(1))


def _lane_iota(rows):
    return lax.broadcasted_iota(jnp.int32, (rows, LANES), 1)


def _rms(x, gain, width):
    ss = jnp.sum(x * x, axis=-1, keepdims=True)
    return x * lax.rsqrt(ss * (1.0 / width) + EPS) * gain


def _log_sigmoid(z):
    return jnp.minimum(z, 0.0) - jnp.log(1.0 + jnp.exp(-jnp.abs(z)))


def _split3(x):
    a = x.astype(BF16)
    r = x - a.astype(F32)
    b = r.astype(BF16)
    c = (r - b.astype(F32)).astype(BF16)
    return a, b, c


def _dot(a, b):
    return jnp.dot(a, b, preferred_element_type=F32)


def _dot_nt(a, b):
    return lax.dot_general(a, b, (((1,), (1,)), ((), ())), preferred_element_type=F32)


def _in0_kernel(*refs, n_cast):
    n_in = 15
    (x_ref, pos_ref, g_ref, w_ref, wsvt_ref, qan_ref, wq_ref, kvan_ref, wk_ref,
     wvt_ref, gq_ref, gqp_ref, gk_ref, gkp_ref, invf_ref) = refs[:n_in]
    cast_in = refs[n_in:n_in + n_cast]
    qm_ref, km_ref, vtm_ref, sq_ref, sk_ref, vts_ref = refs[n_in + n_cast:n_in + n_cast + 6]
    cast_out = refs[n_in + n_cast + 6:]
    for src, dst in zip(cast_in, cast_out):
        dst[...] = src[...].astype(dst.dtype)
    g = g_ref[...]
    lane = _lane_iota(1)
    rope_lane = (lane >= MLA_NOPE) & (lane < MLA_QK)
    deferred = []
    for r in range(ROW_TILE // IN_SUB):
        rows = slice(r * IN_SUB, (r + 1) * IN_SUB)
        h = _rms(x_ref[0, rows, :], g, D_MODEL).astype(BF16)
        proj = _dot(h, w_ref[...])
        o = MLA_Q_RANK + MLA_KV_RANK
        cq = proj[:, 0:MLA_Q_RANK]
        ckv = proj[:, MLA_Q_RANK:o]
        k_rope = proj[:, o:o + LANES]
        k_rope_partner = proj[:, o + LANES:o + 2 * LANES]
        o += 2 * LANES
        sq_ref[0, rows, :] = (proj[:, o:o + SB_WIDTH] * (SB_DIM ** -0.5 * LOG2E)).astype(BF16)
        sk_ref[0, rows, :] = proj[:, o + SB_WIDTH:o + 2 * SB_WIDTH].astype(BF16)

        qn = _rms(cq, qan_ref[...], MLA_Q_RANK).astype(BF16)
        q_all = _dot(qn, wq_ref[...])
        kvn = _rms(ckv, kvan_ref[...], MLA_KV_RANK).astype(BF16)
        k_all = _dot(kvn, wk_ref[...])
        deferred.append((h, kvn))

        pos = pos_ref[0, rows, :].astype(F32)
        group = IN_SUB // TRIG_PACK
        packed = pos[0:group] * invf_ref[0:1, :]
        for gi in range(1, TRIG_PACK):
            packed = packed + pos[gi * group:(gi + 1) * group] * invf_ref[gi:gi + 1, :]

        def unpack(t, fill):
            parts = [t] + [pltpu.roll(t, LANES - MLA_ROPE * gi, 1) for gi in range(1, TRIG_PACK)]
            return jnp.where(rope_lane, jnp.concatenate(parts, axis=0), fill)

        cos = unpack(jnp.cos(packed), 1.0)
        sin = unpack(jnp.sin(packed), 0.0)
        gq_cos = gq_ref[...] * cos
        gq_sin = gqp_ref[...] * sin
        gk_cos = gk_ref[...] * cos
        k_partner_term = k_rope_partner * (gkp_ref[...] * sin)

        def inv_rms(t):
            ss = jnp.sum(t * t, axis=-1, keepdims=True)
            return lax.rsqrt(ss * (1.0 / MLA_QK) + EPS)

        for hd in range(MLA_HEADS):
            sl = slice(hd * LANES, (hd + 1) * LANES)
            t = q_all[:, sl]
            t_partner = q_all[:, MLA_HEADS * LANES + hd * LANES:MLA_HEADS * LANES + (hd + 1) * LANES]
            qm_ref[0, hd, rows, :] = (
                (t * gq_cos + t_partner * gq_sin) * inv_rms(t)).astype(BF16)
            tk = k_all[:, sl] + k_rope
            km_ref[0, hd, rows, :] = (
                (tk * gk_cos + k_partner_term) * inv_rms(tk)).astype(BF16)

    h_all = jnp.concatenate([h for h, _ in deferred], axis=0)
    kvn_all = jnp.concatenate([kvn for _, kvn in deferred], axis=0)
    vts_ref[0, 0] = _dot_nt(wsvt_ref[...], h_all).astype(BF16)
    vtm_ref[0, 0] = _dot_nt(wvt_ref[...], kvn_all).astype(BF16)


def _in0_call(x, pos3, g, w, wsvt, qan, wq, kvan, wk, wvt, gq, gqp, gk, gkp, invf, to_cast):
    b, s, _ = x.shape
    tm = ROW_TILE
    nt = s // tm
    steps = b * nt

    def cast_spec(a):
        n_blocks = steps
        while a.shape[0] % n_blocks or (a.shape[0] // n_blocks) % BF16_SUBLANES:
            n_blocks //= 2
        share = steps // n_blocks
        return pl.BlockSpec((a.shape[0] // n_blocks, a.shape[1]),
                            lambda bi, i: ((bi * nt + i) // share, 0))

    cast_specs = [cast_spec(a) for a in to_cast]
    head_spec = pl.BlockSpec((1, MLA_HEADS, tm, LANES), lambda bi, i: (bi, 0, i, 0))
    vt_spec = pl.BlockSpec((1, 1, SB_WIDTH, tm), lambda bi, i: (bi, i, 0, 0))
    row_spec = pl.BlockSpec((1, tm, SB_WIDTH), lambda bi, i: (bi, i, 0))
    lane_vec = _full((1, LANES))
    return pl.pallas_call(
        functools.partial(_in0_kernel, n_cast=len(to_cast)),
        grid=(b, nt),
        in_specs=[
            pl.BlockSpec((1, tm, D_MODEL), lambda bi, i: (bi, i, 0)),
            pl.BlockSpec((1, tm, 1), lambda bi, i: (bi, i, 0)),
            _full((1, D_MODEL)),
            _full(w.shape),
            _full(wsvt.shape),
            _full((1, MLA_Q_RANK)),
            _full(wq.shape),
            _full((1, MLA_KV_RANK)),
            _full(wk.shape),
            _full(wvt.shape),
            lane_vec, lane_vec, lane_vec, lane_vec, _full((TRIG_PACK, LANES)),
        ] + cast_specs,
        out_specs=[head_spec, head_spec, vt_spec, row_spec, row_spec, vt_spec] + cast_specs,
        out_shape=[
            jax.ShapeDtypeStruct((b, MLA_HEADS, s, LANES), BF16),
            jax.ShapeDtypeStruct((b, MLA_HEADS, s, LANES), BF16),
            jax.ShapeDtypeStruct((b, nt, MLA_HEADS * MLA_V, tm), BF16),
            jax.ShapeDtypeStruct((b, s, SB_WIDTH), BF16),
            jax.ShapeDtypeStruct((b, s, SB_WIDTH), BF16),
            jax.ShapeDtypeStruct((b, nt, SB_WIDTH, tm), BF16),
        ] + [jax.ShapeDtypeStruct(a.shape, BF16) for a in to_cast],
        compiler_params=_compiler_params(2),
        name="l0_in",
    )(x, pos3, g, w, wsvt, qan, wq, kvan, wk, wvt, gq, gqp, gk, gkp, invf, *to_cast)


def _diag_spans(n_chains, chain, tk):
    for c in range(n_chains):
        end = (c + 1) * chain
        for d in range(-(-end // tk)):
            nk = min(tk, end - d * tk)
            yield c, d, nk, d * tk + nk > c * chain


def _softmax_attn_kernel(q_ref, k_ref, vt_ref, o_ref, m_scr, acc_scr, pre_scr, *, head_dim):
    tq, tk, cw = Q_TILE, K_TILE, CHAIN
    n_c = tq // cw
    kt_per_q = tq // tk
    qi = pl.program_id(2)
    chains = [(hh, c) for hh in range(SOFTMAX_HEADS) for c in range(n_c)]
    qs = [q_ref[0, hh, c * cw:(c + 1) * cw, :] for hh, c in chains]

    m_scr[...] = jnp.full(m_scr.shape, NEG_INF, F32)
    acc_scr[...] = jnp.zeros(acc_scr.shape, F32)

    def with_ones(vt):
        return jnp.concatenate([vt, jnp.ones((ONES_ROWS, vt.shape[1]), vt.dtype)], axis=0)

    def update(idx, s, vt, visible):
        if visible is not None:
            s = jnp.where(visible, s, NEG_INF)
        m = m_scr[idx]
        m_new = jnp.maximum(m, jnp.max(s, axis=0, keepdims=True))
        p = jnp.exp2(s - m_new)
        m_scr[idx] = m_new
        acc_scr[idx] = jnp.exp2(m - m_new) * acc_scr[idx] + _dot(vt, p.astype(BF16))

    def scores_of(idx, tile, nk=tk):
        hh = chains[idx][0]
        return _dot_nt(k_ref[0, hh, pl.ds(pl.multiple_of(tile * tk, tk), nk), :], qs[idx])

    def run_pipeline(items, tail_fns, ahead):
        fns = [score_fn for score_fn, _ in items] + list(tail_fns)
        pending = {w: fns[w]() for w in range(min(ahead, len(fns)))}
        for v, (_, consume) in enumerate(items):
            if v + ahead < len(fns):
                pending[v + ahead] = fns[v + ahead]()
            consume(pending.pop(v))

    def prefetch_fns(tile):
        def store(a):
            def fn():
                pre_scr[a] = scores_of(a, tile)
            return fn
        return [store(a) for a in range(PREFETCH)]

    n_full = qi * kt_per_q
    items = []
    for hh in range(SOFTMAX_HEADS):
        for c, d, nk, mask in _diag_spans(n_c, cw, tk):
            idx = hh * n_c + c
            visible = None
            if mask:
                row = lax.broadcasted_iota(jnp.int32, (nk, cw), 0) + d * tk
                col = lax.broadcasted_iota(jnp.int32, (nk, cw), 1) + c * cw
                visible = row <= col

            def score_fn(idx=idx, d=d, nk=nk):
                return scores_of(idx, n_full + d, nk)

            def consume(s, idx=idx, hh=hh, d=d, nk=nk, visible=visible):
                vt = vt_ref[0, n_full + d, hh * head_dim:(hh + 1) * head_dim, 0:nk]
                update(idx, s, with_ones(vt), visible)

            items.append((score_fn, consume))
    run_pipeline(items, prefetch_fns(0), DIAG_LOOKAHEAD)

    def full_step(i, _):
        items = []
        for u in range(kt_per_q):
            j = i * kt_per_q + u
            for idx, (hh, _) in enumerate(chains):
                if u == 0 and idx < PREFETCH:
                    score_fn = lambda idx=idx: pre_scr[idx]
                else:
                    score_fn = lambda idx=idx, j=j: scores_of(idx, j)

                def consume(s, idx=idx, hh=hh, j=j):
                    vt = vt_ref[0, j, hh * head_dim:(hh + 1) * head_dim, :]
                    update(idx, s, with_ones(vt), None)

                items.append((score_fn, consume))
        run_pipeline(items, prefetch_fns(jnp.minimum(i + 1, qi - 1) * kt_per_q), LOOKAHEAD)
        return 0

    lax.fori_loop(0, qi, full_step, 0)

    def normalized(idx):
        acc = acc_scr[idx]
        return acc[:head_dim] / acc[head_dim:head_dim + 1]

    outs = [jnp.concatenate([normalized(hh * n_c + c) for c in range(n_c)], axis=1)
            for hh in range(SOFTMAX_HEADS)]
    o_ref[0] = jnp.concatenate(outs, axis=0).T.astype(o_ref.dtype)


def _softmax_attn_call(q, k, vt, name):
    b, h, s, _ = q.shape
    tq = Q_TILE
    nkt = s // K_TILE
    head_dim = vt.shape[2] // h
    hps = SOFTMAX_HEADS
    n_chains = hps * tq // CHAIN
    return pl.pallas_call(
        functools.partial(_softmax_attn_kernel, head_dim=head_dim),
        grid=(b, h // hps, s // tq),
        in_specs=[
            pl.BlockSpec((1, hps, tq, LANES), lambda bi, p, i: (bi, p, i, 0)),
            pl.BlockSpec((1, hps, s, LANES), lambda bi, p, i: (bi, p, 0, 0)),
            pl.BlockSpec((1, nkt, hps * head_dim, K_TILE), lambda bi, p, i: (bi, 0, p, 0)),
        ],
        out_specs=pl.BlockSpec((1, tq, hps * head_dim), lambda bi, p, i: (bi, i, p)),
        out_shape=jax.ShapeDtypeStruct((b, s, h * head_dim), BF16),
        scratch_shapes=[
            pltpu.VMEM((n_chains, 1, CHAIN), F32),
            pltpu.VMEM((n_chains, head_dim + ONES_ROWS, CHAIN), F32),
            pltpu.VMEM((PREFETCH, K_TILE, CHAIN), F32),
        ],
        compiler_params=_compiler_params(3),
        name=name,
    )(q, k, vt)


def _sb_attn_kernel(q_ref, k_ref, vt_ref, o_ref, run_scr, acc_scr):
    tq, tk, cw, sub = Q_TILE, K_TILE, CHAIN, SB_SUB
    n_c = tq // cw
    kt_per_q = tq // tk
    qi = pl.program_id(2)
    s_row = lax.broadcasted_iota(jnp.int32, (sub, sub), 0)
    s_col = lax.broadcasted_iota(jnp.int32, (sub, sub), 1)
    neg_later = -((s_col > s_row).astype(BF16))
    lane = _lane_iota(cw)
    chains = [(hh, c) for hh in range(2) for c in range(n_c)]
    qs = []
    for hh, c in chains:
        q_pair = q_ref[0, c * cw:(c + 1) * cw, :]
        in_head = (lane >= hh * SB_DIM) & (lane < (hh + 1) * SB_DIM)
        qs.append(jnp.where(in_head, q_pair, jnp.zeros_like(q_pair)))

    run_scr[...] = jnp.zeros(run_scr.shape, F32)
    acc_scr[...] = jnp.zeros(acc_scr.shape, F32)

    def suffix_sums(z, visible):
        cost = jnp.maximum(z, 0.0) + jnp.log(1.0 + jnp.exp2(-jnp.abs(z))) * LOG2E
        log_beta = z - cost
        if visible is not None:
            cost = jnp.where(visible, cost, 0.0)
        cost16 = cost.astype(BF16)
        blocks = [slice(b * sub, (b + 1) * sub) for b in range(z.shape[0] // sub)]
        suffix = [_dot(neg_later, cost16[bl]) for bl in blocks]
        return log_beta, cost, blocks, suffix

    def accumulate(idx, stage, vt, visible):
        log_beta, cost, blocks, suffix = stage
        run = run_scr[idx]
        terms = []
        for bl, sfx in reversed(list(zip(blocks, suffix))):
            terms.append(log_beta[bl] + sfx + run)
            run = run + sfx[0:1, :] - cost[bl.start:bl.start + 1, :]
        a = jnp.exp2(jnp.concatenate(terms[::-1], axis=0))
        if visible is not None:
            a = jnp.where(visible, a, 0.0)
        run_scr[idx] = run
        acc_scr[idx] = acc_scr[idx] + _dot(vt, a.astype(BF16))

    def tile_update(idxs, fns, vts, masks):
        n = len(idxs)
        pending, stages = {}, {}
        for w in range(min(SB_LOOKAHEAD, len(fns))):
            pending[w] = fns[w]()
        for v in range(n + 1):
            if v + SB_LOOKAHEAD < len(fns):
                pending[v + SB_LOOKAHEAD] = fns[v + SB_LOOKAHEAD]()
            if v < n:
                stages[v] = suffix_sums(pending.pop(v), masks[v])
            if v >= 1:
                accumulate(idxs[v - 1], stages.pop(v - 1), vts[v - 1], masks[v - 1])

    def scores_fn(idx, start, nk):
        return lambda: _dot_nt(k_ref[0, pl.ds(pl.multiple_of(start, cw), nk), :], qs[idx])

    q0 = qi * tq
    tri_row = lax.broadcasted_iota(jnp.int32, (cw, cw), 0)
    tri_col = lax.broadcasted_iota(jnp.int32, (cw, cw), 1)
    idxs, fns, vts, masks = [], [], [], []
    covered_from = {}
    for phase in range(2):
        for hh in range(2):
            for c in range(n_c):
                idx = hh * n_c + c
                rel = (c - phase) * cw
                start = jnp.maximum(q0 + rel, 0)
                if phase == 0:
                    visible = tri_row < tri_col
                elif rel < 0:
                    visible = jnp.broadcast_to(qi > 0, (cw, cw))
                else:
                    visible = None
                idxs.append(idx)
                fns.append(scores_fn(idx, start, cw))
                vts.append(vt_ref[0, start // tk, hh * SB_DIM:(hh + 1) * SB_DIM,
                                  rel % tk:rel % tk + cw])
                masks.append(visible)
                covered_from[idx] = q0 + rel
    tile_update(idxs, fns, vts, masks)

    all_idx = list(range(len(chains)))
    key_row = lax.broadcasted_iota(jnp.int32, (tk, cw), 0)

    def alive_at(j):
        flags = [jnp.logical_and(covered_from[idx] > j * tk, jnp.max(run_scr[idx]) > RUN_FLOOR)
                 for idx in all_idx]
        return functools.reduce(jnp.logical_or, flags)

    def step_and_test(carry):
        j, _ = carry
        tile_update(all_idx, [scores_fn(idx, j * tk, tk) for idx in all_idx],
                    [vt_ref[0, j, chains[idx][0] * SB_DIM:(chains[idx][0] + 1) * SB_DIM, :]
                     for idx in all_idx],
                    [key_row + j * tk < covered_from[idx] for idx in all_idx])
        return j - 1, alive_at(j - 1)

    top = qi * kt_per_q + ((n_c - 2) * cw - 1) // tk
    lax.while_loop(lambda carry: jnp.logical_and(carry[0] >= 0, carry[1]),
                   step_and_test, (top, alive_at(top)))
    outs = [jnp.concatenate([acc_scr[hh * n_c + c] for c in range(n_c)], axis=1)
            for hh in range(2)]
    o_ref[0] = jnp.concatenate(outs, axis=0).T.astype(o_ref.dtype)


def _sb_attn_call(q, k, vt):
    b, s, w = q.shape
    tq = Q_TILE
    nkt = s // K_TILE
    n_chains = 2 * tq // CHAIN
    return pl.pallas_call(
        _sb_attn_kernel,
        grid=(b, w // LANES, s // tq),
        in_specs=[
            pl.BlockSpec((1, tq, LANES), lambda bi, p, i: (bi, i, p)),
            pl.BlockSpec((1, s, LANES), lambda bi, p, i: (bi, 0, p)),
            pl.BlockSpec((1, nkt, LANES, K_TILE), lambda bi, p, i: (bi, 0, p, 0)),
        ],
        out_specs=pl.BlockSpec((1, tq, LANES), lambda bi, p, i: (bi, i, p)),
        out_shape=jax.ShapeDtypeStruct((b, s, w), BF16),
        scratch_shapes=[
            pltpu.VMEM((n_chains, 1, CHAIN), F32),
            pltpu.VMEM((n_chains, SB_DIM, CHAIN), F32),
        ],
        compiler_params=_compiler_params(3),
        name="l0_sb_attn",
    )(q, k, vt)


def _out_ffn_kernel(*refs, n_mix):
    x_ref = refs[0]
    o_refs = refs[1:1 + n_mix]
    wo_refs = refs[1 + n_mix:1 + 2 * n_mix]
    g_ref, wg_ref, wu_ref, wd_ref, y_ref = refs[1 + 2 * n_mix:]
    y = x_ref[...]
    for o_ref, wo_ref in zip(o_refs, wo_refs):
        y = y + _dot(o_ref[...], wo_ref[...])
    h = _rms(y, g_ref[...], D_MODEL).astype(BF16)
    gate = _dot(h, wg_ref[...])
    up = _dot(h, wu_ref[...])
    act = gate * (1.0 / (1.0 + jnp.exp(-gate))) * up
    y_ref[...] = y + _dot(act.astype(BF16), wd_ref[...])


def _out_ffn_call(x2, mixes, wos, g, wg, wu, wd, name):
    n, d = x2.shape
    tm = FFN_TILE
    n_mix = len(mixes)
    row = lambda width: pl.BlockSpec((tm, width), lambda i: (i, 0))
    return pl.pallas_call(
        functools.partial(_out_ffn_kernel, n_mix=n_mix),
        grid=(n // tm,),
        in_specs=([row(d)] + [row(m.shape[1]) for m in mixes]
                  + [_full(w.shape) for w in wos]
                  + [_full((1, d)), _full(wg.shape), _full(wu.shape), _full(wd.shape)]),
        out_specs=row(d),
        out_shape=jax.ShapeDtypeStruct((n, d), F32),
        compiler_params=_compiler_params(1),
        name=name,
    )(x2, *mixes, *wos, g, wg, wu, wd)


def _in1_kernel(x_ref, g_ref, w_ref, wvt_ref, fb_ref, gq_ref, gk_ref, layout_ref,
                qa_ref, ka_ref, vt_ref, carry_ref):
    i = pl.program_id(1)

    @pl.when(i == 0)
    def _():
        carry_ref[...] = jnp.zeros_like(carry_ref)

    n_terms = AUG_TERMS
    used = n_terms * FOX_HEADS
    rows = IN_SUB
    lane = _lane_iota(1)
    half_lane = lane & (FOX_DIM - 1)
    low_half = lane < FOX_DIM
    term = layout_ref[...]
    decay_lane = term < n_terms
    ones_lane = term == n_terms
    own_rows = [jnp.where((half_lane >= n_terms * hd) & (half_lane < n_terms * (hd + 1)), 1.0, 0.0)
                for hd in range(FOX_HEADS)]
    r_i = lax.broadcasted_iota(jnp.int32, (rows, rows), 0)
    c_i = lax.broadcasted_iota(jnp.int32, (rows, rows), 1)
    lower = (c_i <= r_i).astype(BF16)
    g = g_ref[...]
    gq = gq_ref[...]
    gk = gk_ref[...]

    deferred = []
    for r in range(ROW_TILE // IN_SUB):
        sl_rows = slice(r * rows, (r + 1) * rows)
        h = _rms(x_ref[0, sl_rows, :], g, D_MODEL).astype(BF16)
        proj = _dot(h, w_ref[...])
        deferred.append(h)
        q = proj[:, 0:FOX_WIDTH]
        k = proj[:, FOX_WIDTH:2 * FOX_WIDTH]
        f_logit = proj[:, 2 * FOX_WIDTH:]

        log_f = jnp.where(decay_lane, _log_sigmoid(f_logit + fb_ref[...]), 0.0)
        f1, f2, f3 = _split3(log_f)
        decay = carry_ref[...] + _dot(lower, f1) + _dot(lower, f2) + _dot(lower, f3)
        carry_ref[...] = decay[rows - 1:rows, :]
        d1, d2, d3 = (d.astype(F32) for d in _split3(decay * LOG2E))
        terms = jnp.where(term == 0, d1, jnp.where(term == 1, d2, d3))
        aug_q = jnp.where(decay_lane, terms, jnp.where(ones_lane, 1.0, 0.0))
        neg_terms = -terms

        for pair in range(FOX_HEADS // 2):
            sl = slice(pair * LANES, (pair + 1) * LANES)
            qp = q[:, sl]
            kp = k[:, sl]
            qsq = qp * qp
            ksq = kp * kp
            q_gain = qp * gq
            k_gain = kp * gk
            for parity in range(2):
                hd = 2 * pair + parity
                in_head = low_half if parity == 0 else jnp.logical_not(low_half)

                def inv_rms(sq):
                    ss = jnp.sum(jnp.where(in_head, sq, 0.0), axis=-1, keepdims=True)
                    return lax.rsqrt(ss * (1.0 / FOX_DIM) + EPS)

                qa_ref[0, hd, sl_rows, :] = jnp.where(
                    in_head, q_gain * inv_rms(qsq), aug_q).astype(BF16)
                moved = pltpu.roll(neg_terms, used - n_terms * hd, 1)
                aug_k = jnp.where(ones_lane, moved, own_rows[hd])
                ka_ref[0, hd, sl_rows, :] = jnp.where(
                    in_head, k_gain * inv_rms(ksq), aug_k).astype(BF16)

    h_all = jnp.concatenate(deferred, axis=0)
    vt_ref[0, 0] = _dot_nt(wvt_ref[...], h_all).astype(BF16)


def _in1_call(x, g, w, wvt, fb, gq, gk, layout):
    b, s, _ = x.shape
    tm = ROW_TILE
    nt = s // tm
    head_spec = pl.BlockSpec((1, FOX_HEADS, tm, LANES), lambda bi, i: (bi, 0, i, 0))
    lane_vec = _full((1, LANES))
    return pl.pallas_call(
        _in1_kernel,
        grid=(b, nt),
        in_specs=[
            pl.BlockSpec((1, tm, D_MODEL), lambda bi, i: (bi, i, 0)),
            _full((1, D_MODEL)),
            _full(w.shape),
            _full(wvt.shape),
            lane_vec, lane_vec, lane_vec, lane_vec,
        ],
        out_specs=[head_spec, head_spec,
                   pl.BlockSpec((1, 1, FOX_WIDTH, tm), lambda bi, i: (bi, i, 0, 0))],
        out_shape=[
            jax.ShapeDtypeStruct((b, FOX_HEADS, s, LANES), BF16),
            jax.ShapeDtypeStruct((b, FOX_HEADS, s, LANES), BF16),
            jax.ShapeDtypeStruct((b, nt, FOX_WIDTH, tm), BF16),
        ],
        scratch_shapes=[pltpu.VMEM((1, LANES), F32)],
        compiler_params=_compiler_params(2),
        name="l1_in",
    )(x, g, w, wvt, fb, gq, gk, layout)


def _pad_lanes(v, width=LANES):
    return jnp.pad(v, (0, width - v.shape[0])).reshape(1, width)


def _rope_partner(cols):
    half = MLA_ROPE // 2
    return jnp.concatenate([-cols[..., half:], cols[..., :half]], axis=-1)


def _prep_layer0(w_in, w_q_up, w_kv_up, q_norm, k_norm):
    o = MLA_Q_RANK + MLA_KV_RANK
    pad_rope = lambda c: jnp.pad(c, ((0, 0),) * (c.ndim - 1) + ((MLA_NOPE, LANES - MLA_QK),))
    w_rope = w_in[:, o:o + MLA_ROPE]
    sb = w_in[:, o + MLA_ROPE:]
    w = jnp.concatenate([w_in[:, :o], pad_rope(w_rope), pad_rope(_rope_partner(w_rope)),
                         sb[:, :2 * SB_WIDTH]], axis=1)
    wsvt = sb[:, 2 * SB_WIDTH:].T
    wq3 = w_q_up.reshape(MLA_Q_RANK, MLA_HEADS, MLA_QK)
    wq = jnp.pad(wq3, ((0, 0), (0, 0), (0, LANES - MLA_QK)))
    wq_partner = pad_rope(_rope_partner(wq3[:, :, MLA_NOPE:]))
    wq = jnp.concatenate([wq.reshape(MLA_Q_RANK, -1), wq_partner.reshape(MLA_Q_RANK, -1)], axis=1)
    kv = w_kv_up.reshape(MLA_KV_RANK, MLA_HEADS, MLA_NOPE + MLA_V)
    wk = jnp.pad(kv[:, :, :MLA_NOPE], ((0, 0), (0, 0), (0, LANES - MLA_NOPE)))
    wk = wk.reshape(MLA_KV_RANK, MLA_HEADS * LANES)
    wvt = kv[:, :, MLA_NOPE:].reshape(MLA_KV_RANK, MLA_HEADS * MLA_V).T
    gq = q_norm * (MLA_QK ** -0.5 * LOG2E)
    half = MLA_ROPE // 2
    swap_halves = lambda r: jnp.concatenate([r[half:], r[:half]])
    gqp = pad_rope(swap_halves(gq[MLA_NOPE:]))
    gkp = pad_rope(swap_halves(k_norm[MLA_NOPE:]))
    inv_freq = ROPE_THETA ** (-jnp.arange(half, dtype=F32) / half)
    invf = pad_rope(jnp.concatenate([inv_freq, inv_freq]))
    invf = jnp.stack([jnp.roll(invf, MLA_ROPE * gi) for gi in range(TRIG_PACK)])
    lanes = lambda v: v.reshape(1, LANES)
    return (w.astype(BF16), wsvt.astype(BF16), wq.astype(BF16), wk.astype(BF16), wvt.astype(BF16),
            _pad_lanes(gq), lanes(gqp), _pad_lanes(k_norm), lanes(gkp), invf)


def _prep_layer1(w_in, f_bias, q_norm, k_norm):
    wf = w_in[:, 3 * FOX_WIDTH:]
    rep = jnp.repeat(wf, AUG_TERMS, axis=1)
    pad = jnp.zeros((wf.shape[0], FOX_DIM - AUG_TERMS * FOX_HEADS), wf.dtype)
    f_group = jnp.concatenate([rep, pad, rep, pad], axis=1)
    w = jnp.concatenate([w_in[:, :2 * FOX_WIDTH], f_group], axis=1)
    wvt = w_in[:, 2 * FOX_WIDTH:3 * FOX_WIDTH].T
    fb = jnp.repeat(f_bias, AUG_TERMS)
    fpad = jnp.zeros((FOX_DIM - AUG_TERMS * FOX_HEADS,), f_bias.dtype)
    fb = jnp.concatenate([fb, fpad, fb, fpad]).reshape(1, LANES)
    gain2 = lambda gvec: jnp.concatenate([gvec, gvec]).reshape(1, LANES)
    used = AUG_TERMS * FOX_HEADS
    half_lane = jnp.arange(LANES, dtype=jnp.int32) % FOX_DIM
    layout = jnp.where(half_lane < used, half_lane % AUG_TERMS,
                       jnp.where(half_lane < used + AUG_TERMS, AUG_TERMS, AUG_TERMS + 1))
    return (w.astype(BF16), wvt.astype(BF16), fb,
            gain2(q_norm * (FOX_DIM ** -0.5 * LOG2E)), gain2(k_norm), layout.reshape(1, LANES))


def kernel(x, positions, l0_attn_norm, l0_w_in, l0_mla_q_a_norm, l0_mla_w_q_up, l0_mla_kv_a_norm, l0_mla_w_kv_up, l0_mla_q_norm, l0_mla_k_norm, l0_w_o, l0_ffn_norm, l0_w_gate, l0_w_up, l0_w_down, l1_attn_norm, l1_w_in, l1_fox_f_bias, l1_fox_q_norm, l1_fox_k_norm, l1_w_o, l1_ffn_norm, l1_w_gate, l1_w_up, l1_w_down):
    b, s, d = x.shape
    assert d == D_MODEL and s % Q_TILE == 0 and K_TILE == ROW_TILE and Q_TILE % K_TILE == 0
    assert K_TILE % CHAIN == 0 and CHAIN % SB_SUB == 0
    row = lambda v: v.reshape(1, -1)

    w0, wsvt, wq, wk, wvt0, gq0, gqp0, gk0, gkp0, invf = _prep_layer0(
        l0_w_in, l0_mla_w_q_up, l0_mla_w_kv_up, l0_mla_q_norm, l0_mla_k_norm)
    ffn_f32 = [l0_w_gate, l0_w_up, l0_w_down, l1_w_gate, l1_w_up, l1_w_down]
    qm, km, vtm, sq, sk, vts, *ffn_bf16 = _in0_call(
        x, positions.reshape(b, s, 1), row(l0_attn_norm), w0, wsvt, row(l0_mla_q_a_norm), wq,
        row(l0_mla_kv_a_norm), wk, wvt0, gq0, gqp0, gk0, gkp0, invf, ffn_f32)
    wg0, wu0, wd0, wg1, wu1, wd1 = ffn_bf16
    o_mla = _softmax_attn_call(qm, km, vtm, "l0_mla_attn")
    o_sb = _sb_attn_call(sq, sk, vts)
    n_mla = MLA_HEADS * MLA_V
    x1 = _out_ffn_call(
        x.reshape(b * s, d),
        [o_mla.reshape(b * s, n_mla), o_sb.reshape(b * s, SB_WIDTH)],
        [l0_w_o[:n_mla].astype(BF16), l0_w_o[n_mla:].astype(BF16)],
        row(l0_ffn_norm), wg0, wu0, wd0, "l0_out_ffn")

    w1, wvt1, fb, gq1, gk1, layout = _prep_layer1(
        l1_w_in, l1_fox_f_bias, l1_fox_q_norm, l1_fox_k_norm)
    qa, ka, vtf = _in1_call(
        x1.reshape(b, s, d), row(l1_attn_norm), w1, wvt1, fb, gq1, gk1, layout)
    o_fox = _softmax_attn_call(qa, ka, vtf, "l1_fox_attn")
    out = _out_ffn_call(
        x1, [o_fox.reshape(b * s, FOX_WIDTH)], [l1_w_o.astype(BF16)],
        row(l1_ffn_norm), wg1, wu1, wd1, "l1_out_ffn")
    return out.reshape(b, s, d)
```

```python
import functools

import jax
import jax.numpy as jnp
from jax import lax
from jax.experimental import pallas as pl
from jax.experimental.pallas import tpu as pltpu

F32 = jnp.float32
BF16 = jnp.bfloat16

D_MODEL = 1024
EPS = 1e-6
NEG_INF = -1e30
ROPE_THETA = 10000.0

MLA_HEADS = 8
MLA_Q_RANK = 256
MLA_KV_RANK = 128
MLA_NOPE = 64
MLA_ROPE = 32
MLA_QK = MLA_NOPE + MLA_ROPE
MLA_V = 64
SB_HEADS = 8
SB_DIM = 64
SB_WIDTH = SB_HEADS * SB_DIM
FOX_HEADS = 16
FOX_DIM = 64
FOX_WIDTH = FOX_HEADS * FOX_DIM

LANES = 128
BF16_SUBLANES = 16
V7X_VMEM_BYTES = 64 * 1024 * 1024
VMEM_LIMIT = V7X_VMEM_BYTES * 7 // 8

ROW_TILE = 512
K_TILE = 512
Q_TILE = 1024
SB_STEP_HEADS = 4
SOFTMAX_HEADS = 8
CHAIN = 256
FFN_TILE = 512
PREFETCH = 2
LOOKAHEAD = 3
DIAG_LOOKAHEAD = 7
SB_LOOKAHEAD = 3
ONES_ROWS = 16
SB_SUB = 128
LOG2E = 1.4426950408889634
RUN_FLOOR = -160.0

AUG_TERMS = 3
TRIG_PACK = LANES // MLA_ROPE
IN_SUB = 256


def _compiler_params(n_axes):
    return pltpu.CompilerParams(
        dimension_semantics=("arbitrary",) * n_axes,
        vmem_limit_bytes=VMEM_LIMIT,
    )


def _full(shape):
    return pl.BlockSpec(shape, lambda *_: (0,) * len(shape), pipeline_mode=pl.Buffered(1))


def _lane_iota(rows):
    return lax.broadcasted_iota(jnp.int32, (rows, LANES), 1)


def _rms(x, gain, width):
    ss = jnp.sum(x * x, axis=-1, keepdims=True)
    return x * lax.rsqrt(ss * (1.0 / width) + EPS) * gain


def _log_sigmoid(z):
    return jnp.minimum(z, 0.0) - jnp.log(1.0 + jnp.exp(-jnp.abs(z)))


def _split3(x):
    a = x.astype(BF16)
    r = x - a.astype(F32)
    b = r.astype(BF16)
    c = (r - b.astype(F32)).astype(BF16)
    return a, b, c


def _dot(a, b):
    return jnp.dot(a, b, preferred_element_type=F32)


def _dot_nt(a, b):
    return lax.dot_general(a, b, (((1,), (1,)), ((), ())), preferred_element_type=F32)


def _in0_kernel(*refs, n_cast):
    n_in = 15
    (x_ref, pos_ref, g_ref, w_ref, wsvt_ref, qan_ref, wq_ref, kvan_ref, wk_ref,
     wvt_ref, gq_ref, gqp_ref, gk_ref, gkp_ref, invf_ref) = refs[:n_in]
    cast_in = refs[n_in:n_in + n_cast]
    qm_ref, km_ref, vtm_ref, sq_ref, sk_ref, vts_ref = refs[n_in + n_cast:n_in + n_cast + 6]
    cast_out = refs[n_in + n_cast + 6:]
    for src, dst in zip(cast_in, cast_out):
        dst[...] = src[...].astype(dst.dtype)
    g = g_ref[...]
    lane = _lane_iota(1)
    rope_lane = (lane >= MLA_NOPE) & (lane < MLA_QK)
    deferred = []
    for r in range(ROW_TILE // IN_SUB):
        rows = slice(r * IN_SUB, (r + 1) * IN_SUB)
        h = _rms(x_ref[0, rows, :], g, D_MODEL).astype(BF16)
        proj = _dot(h, w_ref[...])
        o = MLA_Q_RANK + MLA_KV_RANK
        cq = proj[:, 0:MLA_Q_RANK]
        ckv = proj[:, MLA_Q_RANK:o]
        k_rope = proj[:, o:o + LANES]
        k_rope_partner = proj[:, o + LANES:o + 2 * LANES]
        o += 2 * LANES
        sq_ref[0, rows, :] = (proj[:, o:o + SB_WIDTH] * (SB_DIM ** -0.5 * LOG2E)).astype(BF16)
        sk_ref[0, rows, :] = proj[:, o + SB_WIDTH:o + 2 * SB_WIDTH].astype(BF16)

        qn = _rms(cq, qan_ref[...], MLA_Q_RANK).astype(BF16)
        q_all = _dot(qn, wq_ref[...])
        kvn = _rms(ckv, kvan_ref[...], MLA_KV_RANK).astype(BF16)
        k_all = _dot(kvn, wk_ref[...])
        deferred.append((h, kvn))

        pos = pos_ref[0, rows, :].astype(F32)
        group = IN_SUB // TRIG_PACK
        packed = pos[0:group] * invf_ref[0:1, :]
        for gi in range(1, TRIG_PACK):
            packed = packed + pos[gi * group:(gi + 1) * group] * invf_ref[gi:gi + 1, :]

        def unpack(t, fill):
            parts = [t] + [pltpu.roll(t, LANES - MLA_ROPE * gi, 1) for gi in range(1, TRIG_PACK)]
            return jnp.where(rope_lane, jnp.concatenate(parts, axis=0), fill)

        cos = unpack(jnp.cos(packed), 1.0)
        sin = unpack(jnp.sin(packed), 0.0)
        gq_cos = gq_ref[...] * cos
        gq_sin = gqp_ref[...] * sin
        gk_cos = gk_ref[...] * cos
        k_partner_term = k_rope_partner * (gkp_ref[...] * sin)

        def inv_rms(t):
            ss = jnp.sum(t * t, axis=-1, keepdims=True)
            return lax.rsqrt(ss * (1.0 / MLA_QK) + EPS)

        for hd in range(MLA_HEADS):
            sl = slice(hd * LANES, (hd + 1) * LANES)
            t = q_all[:, sl]
            t_partner = q_all[:, MLA_HEADS * LANES + hd * LANES:MLA_HEADS * LANES + (hd + 1) * LANES]
            qm_ref[0, hd, rows, :] = (
                (t * gq_cos + t_partner * gq_sin) * inv_rms(t)).astype(BF16)
            tk = k_all[:, sl] + k_rope
            km_ref[0, hd, rows, :] = (
                (tk * gk_cos + k_partner_term) * inv_rms(tk)).astype(BF16)

    h_all = jnp.concatenate([h for h, _ in deferred], axis=0)
    kvn_all = jnp.concatenate([kvn for _, kvn in deferred], axis=0)
    vts_ref[0, 0] = _dot_nt(wsvt_ref[...], h_all).astype(BF16)
    vtm_ref[0, 0] = _dot_nt(wvt_ref[...], kvn_all).astype(BF16)


def _in0_call(x, pos3, g, w, wsvt, qan, wq, kvan, wk, wvt, gq, gqp, gk, gkp, invf, to_cast):
    b, s, _ = x.shape
    tm = ROW_TILE
    nt = s // tm
    steps = b * nt

    def cast_spec(a):
        n_blocks = steps
        while a.shape[0] % n_blocks or (a.shape[0] // n_blocks) % BF16_SUBLANES:
            n_blocks //= 2
        share = steps // n_blocks
        return pl.BlockSpec((a.shape[0] // n_blocks, a.shape[1]),
                            lambda bi, i: ((bi * nt + i) // share, 0))

    cast_specs = [cast_spec(a) for a in to_cast]
    head_spec = pl.BlockSpec((1, MLA_HEADS, tm, LANES), lambda bi, i: (bi, 0, i, 0))
    vt_spec = pl.BlockSpec((1, 1, SB_WIDTH, tm), lambda bi, i: (bi, i, 0, 0))
    row_spec = pl.BlockSpec((1, tm, SB_WIDTH), lambda bi, i: (bi, i, 0))
    lane_vec = _full((1, LANES))
    return pl.pallas_call(
        functools.partial(_in0_kernel, n_cast=len(to_cast)),
        grid=(b, nt),
        in_specs=[
            pl.BlockSpec((1, tm, D_MODEL), lambda bi, i: (bi, i, 0)),
            pl.BlockSpec((1, tm, 1), lambda bi, i: (bi, i, 0)),
            _full((1, D_MODEL)),
            _full(w.shape),
            _full(wsvt.shape),
            _full((1, MLA_Q_RANK)),
            _full(wq.shape),
            _full((1, MLA_KV_RANK)),
            _full(wk.shape),
            _full(wvt.shape),
            lane_vec, lane_vec, lane_vec, lane_vec, _full((TRIG_PACK, LANES)),
        ] + cast_specs,
        out_specs=[head_spec, head_spec, vt_spec, row_spec, row_spec, vt_spec] + cast_specs,
        out_shape=[
            jax.ShapeDtypeStruct((b, MLA_HEADS, s, LANES), BF16),
            jax.ShapeDtypeStruct((b, MLA_HEADS, s, LANES), BF16),
            jax.ShapeDtypeStruct((b, nt, MLA_HEADS * MLA_V, tm), BF16),
            jax.ShapeDtypeStruct((b, s, SB_WIDTH), BF16),
            jax.ShapeDtypeStruct((b, s, SB_WIDTH), BF16),
            jax.ShapeDtypeStruct((b, nt, SB_WIDTH, tm), BF16),
        ] + [jax.ShapeDtypeStruct(a.shape, BF16) for a in to_cast],
        compiler_params=_compiler_params(2),
        name="l0_in",
    )(x, pos3, g, w, wsvt, qan, wq, kvan, wk, wvt, gq, gqp, gk, gkp, invf, *to_cast)


def _diag_spans(n_chains, chain, tk):
    for c in range(n_chains):
        end = (c + 1) * chain
        for d in range(-(-end // tk)):
            nk = min(tk, end - d * tk)
            yield c, d, nk, d * tk + nk > c * chain


def _softmax_attn_kernel(q_ref, k_ref, vt_ref, o_ref, m_scr, acc_scr, pre_scr, *, head_dim):
    tq, tk, cw = Q_TILE, K_TILE, CHAIN
    n_c = tq // cw
    kt_per_q = tq // tk
    qi = pl.program_id(2)
    chains = [(hh, c) for hh in range(SOFTMAX_HEADS) for c in range(n_c)]
    qs = [q_ref[0, hh, c * cw:(c + 1) * cw, :] for hh, c in chains]

    m_scr[...] = jnp.full(m_scr.shape, NEG_INF, F32)
    acc_scr[...] = jnp.zeros(acc_scr.shape, F32)

    def with_ones(vt):
        return jnp.concatenate([vt, jnp.ones((ONES_ROWS, vt.shape[1]), vt.dtype)], axis=0)

    def update(idx, s, vt, visible):
        if visible is not None:
            s = jnp.where(visible, s, NEG_INF)
        m = m_scr[idx]
        m_new = jnp.maximum(m, jnp.max(s, axis=0, keepdims=True))
        p = jnp.exp2(s - m_new)
        m_scr[idx] = m_new
        acc_scr[idx] = jnp.exp2(m - m_new) * acc_scr[idx] + _dot(vt, p.astype(BF16))

    def scores_of(idx, tile, nk=tk):
        hh = chains[idx][0]
        return _dot_nt(k_ref[0, hh, pl.ds(pl.multiple_of(tile * tk, tk), nk), :], qs[idx])

    def run_pipeline(items, tail_fns, ahead):
        fns = [score_fn for score_fn, _ in items] + list(tail_fns)
        pending = {w: fns[w]() for w in range(min(ahead, len(fns)))}
        for v, (_, consume) in enumerate(items):
            if v + ahead < len(fns):
                pending[v + ahead] = fns[v + ahead]()
            consume(pending.pop(v))

    def prefetch_fns(tile):
        def store(a):
            def fn():
                pre_scr[a] = scores_of(a, tile)
            return fn
        return [store(a) for a in range(PREFETCH)]

    n_full = qi * kt_per_q
    items = []
    for hh in range(SOFTMAX_HEADS):
        for c, d, nk, mask in _diag_spans(n_c, cw, tk):
            idx = hh * n_c + c
            visible = None
            if mask:
                row = lax.broadcasted_iota(jnp.int32, (nk, cw), 0) + d * tk
                col = lax.broadcasted_iota(jnp.int32, (nk, cw), 1) + c * cw
                visible = row <= col

            def score_fn(idx=idx, d=d, nk=nk):
                return scores_of(idx, n_full + d, nk)

            def consume(s, idx=idx, hh=hh, d=d, nk=nk, visible=visible):
                vt = vt_ref[0, n_full + d, hh * head_dim:(hh + 1) * head_dim, 0:nk]
                update(idx, s, with_ones(vt), visible)

            items.append((score_fn, consume))
    run_pipeline(items, prefetch_fns(0), DIAG_LOOKAHEAD)

    def full_step(i, _):
        items = []
        for u in range(kt_per_q):
            j = i * kt_per_q + u
            for idx, (hh, _) in enumerate(chains):
                if u == 0 and idx < PREFETCH:
                    score_fn = lambda idx=idx: pre_scr[idx]
                else:
                    score_fn = lambda idx=idx, j=j: scores_of(idx, j)

                def consume(s, idx=idx, hh=hh, j=j):
                    vt = vt_ref[0, j, hh * head_dim:(hh + 1) * head_dim, :]
                    update(idx, s, with_ones(vt), None)

                items.append((score_fn, consume))
        run_pipeline(items, prefetch_fns(jnp.minimum(i + 1, qi - 1) * kt_per_q), LOOKAHEAD)
        return 0

    lax.fori_loop(0, qi, full_step, 0)

    def normalized(idx):
        acc = acc_scr[idx]
        return acc[:head_dim] / acc[head_dim:head_dim + 1]

    outs = [jnp.concatenate([normalized(hh * n_c + c) for c in range(n_c)], axis=1)
            for hh in range(SOFTMAX_HEADS)]
    o_ref[0] = jnp.concatenate(outs, axis=0).T.astype(o_ref.dtype)


def _softmax_attn_call(q, k, vt, name):
    b, h, s, _ = q.shape
    tq = Q_TILE
    nkt = s // K_TILE
    head_dim = vt.shape[2] // h
    hps = SOFTMAX_HEADS
    n_chains = hps * tq // CHAIN
    return pl.pallas_call(
        functools.partial(_softmax_attn_kernel, head_dim=head_dim),
        grid=(b, h // hps, s // tq),
        in_specs=[
            pl.BlockSpec((1, hps, tq, LANES), lambda bi, p, i: (bi, p, i, 0)),
            pl.BlockSpec((1, hps, s, LANES), lambda bi, p, i: (bi, p, 0, 0)),
            pl.BlockSpec((1, nkt, hps * head_dim, K_TILE), lambda bi, p, i: (bi, 0, p, 0)),
        ],
        out_specs=pl.BlockSpec((1, tq, hps * head_dim), lambda bi, p, i: (bi, i, p)),
        out_shape=jax.ShapeDtypeStruct((b, s, h * head_dim), BF16),
        scratch_shapes=[
            pltpu.VMEM((n_chains, 1, CHAIN), F32),
            pltpu.VMEM((n_chains, head_dim + ONES_ROWS, CHAIN), F32),
            pltpu.VMEM((PREFETCH, K_TILE, CHAIN), F32),
        ],
        compiler_params=_compiler_params(3),
        name=name,
    )(q, k, vt)


def _sb_attn_kernel(q_ref, k_ref, vt_ref, o_ref, run_scr, acc_scr):
    tq, tk, cw, sub = Q_TILE, K_TILE, CHAIN, SB_SUB
    n_c = tq // cw
    kt_per_q = tq // tk
    qi = pl.program_id(2)
    s_row = lax.broadcasted_iota(jnp.int32, (sub, sub), 0)
    s_col = lax.broadcasted_iota(jnp.int32, (sub, sub), 1)
    neg_later = -((s_col > s_row).astype(BF16))
    lane = _lane_iota(cw)
    heads = SB_STEP_HEADS
    chains = [(hh, c) for hh in range(heads) for c in range(n_c)]

    def pair_lanes(hh):
        return slice((hh // 2) * LANES, (hh // 2 + 1) * LANES)

    qs = []
    for hh, c in chains:
        q_pair = q_ref[0, c * cw:(c + 1) * cw, pair_lanes(hh)]
        in_head = (lane >= (hh % 2) * SB_DIM) & (lane < (hh % 2 + 1) * SB_DIM)
        qs.append(jnp.where(in_head, q_pair, jnp.zeros_like(q_pair)))

    run_scr[...] = jnp.zeros(run_scr.shape, F32)
    acc_scr[...] = jnp.zeros(acc_scr.shape, F32)

    def suffix_sums(z, visible):
        cost = jnp.maximum(z, 0.0) + jnp.log(1.0 + jnp.exp2(-jnp.abs(z))) * LOG2E
        log_beta = z - cost
        if visible is not None:
            cost = jnp.where(visible, cost, 0.0)
        cost16 = cost.astype(BF16)
        blocks = [slice(b * sub, (b + 1) * sub) for b in range(z.shape[0] // sub)]
        suffix = [_dot(neg_later, cost16[bl]) for bl in blocks]
        return log_beta, cost, blocks, suffix

    def accumulate(idx, stage, vt, visible):
        log_beta, cost, blocks, suffix = stage
        run = run_scr[idx]
        terms = []
        for bl, sfx in reversed(list(zip(blocks, suffix))):
            terms.append(log_beta[bl] + sfx + run)
            run = run + sfx[0:1, :] - cost[bl.start:bl.start + 1, :]
        a = jnp.exp2(jnp.concatenate(terms[::-1], axis=0))
        if visible is not None:
            a = jnp.where(visible, a, 0.0)
        run_scr[idx] = run
        acc_scr[idx] = acc_scr[idx] + _dot(vt, a.astype(BF16))

    def tile_update(idxs, fns, vts, masks):
        n = len(idxs)
        pending, stages = {}, {}
        for w in range(min(SB_LOOKAHEAD, len(fns))):
            pending[w] = fns[w]()
        for v in range(n + 1):
            if v + SB_LOOKAHEAD < len(fns):
                pending[v + SB_LOOKAHEAD] = fns[v + SB_LOOKAHEAD]()
            if v < n:
                stages[v] = suffix_sums(pending.pop(v), masks[v])
            if v >= 1:
                accumulate(idxs[v - 1], stages.pop(v - 1), vts[v - 1], masks[v - 1])

    def scores_fn(idx, start, nk):
        return lambda: _dot_nt(
            k_ref[0, pl.ds(pl.multiple_of(start, cw), nk), pair_lanes(chains[idx][0])], qs[idx])

    q0 = qi * tq
    tri_row = lax.broadcasted_iota(jnp.int32, (cw, cw), 0)
    tri_col = lax.broadcasted_iota(jnp.int32, (cw, cw), 1)
    idxs, fns, vts, masks = [], [], [], []
    covered_from = {}
    for phase in range(2):
        for hh in range(heads):
            for c in range(n_c):
                idx = hh * n_c + c
                rel = (c - phase) * cw
                start = jnp.maximum(q0 + rel, 0)
                if phase == 0:
                    visible = tri_row < tri_col
                elif rel < 0:
                    visible = jnp.broadcast_to(qi > 0, (cw, cw))
                else:
                    visible = None
                idxs.append(idx)
                fns.append(scores_fn(idx, start, cw))
                vts.append(vt_ref[0, start // tk, hh * SB_DIM:(hh + 1) * SB_DIM,
                                  rel % tk:rel % tk + cw])
                masks.append(visible)
                covered_from[idx] = q0 + rel
    tile_update(idxs, fns, vts, masks)

    all_idx = list(range(len(chains)))
    key_row = lax.broadcasted_iota(jnp.int32, (tk, cw), 0)

    def alive_at(j):
        flags = [jnp.logical_and(covered_from[idx] > j * tk, jnp.max(run_scr[idx]) > RUN_FLOOR)
                 for idx in all_idx]
        return functools.reduce(jnp.logical_or, flags)

    def step_and_test(carry):
        j, _ = carry
        tile_update(all_idx, [scores_fn(idx, j * tk, tk) for idx in all_idx],
                    [vt_ref[0, j, chains[idx][0] * SB_DIM:(chains[idx][0] + 1) * SB_DIM, :]
                     for idx in all_idx],
                    [key_row + j * tk < covered_from[idx] for idx in all_idx])
        return j - 1, alive_at(j - 1)

    top = qi * kt_per_q + ((n_c - 2) * cw - 1) // tk
    lax.while_loop(lambda carry: jnp.logical_and(carry[0] >= 0, carry[1]),
                   step_and_test, (top, alive_at(top)))
    outs = [jnp.concatenate([acc_scr[hh * n_c + c] for c in range(n_c)], axis=1)
            for hh in range(heads)]
    o_ref[0] = jnp.concatenate(outs, axis=0).T.astype(o_ref.dtype)


def _sb_attn_call(q, k, vt):
    b, s, w = q.shape
    tq = Q_TILE
    nkt = s // K_TILE
    width = SB_STEP_HEADS * SB_DIM
    n_chains = SB_STEP_HEADS * tq // CHAIN
    return pl.pallas_call(
        _sb_attn_kernel,
        grid=(b, w // width, s // tq),
        in_specs=[
            pl.BlockSpec((1, tq, width), lambda bi, p, i: (bi, i, p)),
            pl.BlockSpec((1, s, width), lambda bi, p, i: (bi, 0, p)),
            pl.BlockSpec((1, nkt, width, K_TILE), lambda bi, p, i: (bi, 0, p, 0)),
        ],
        out_specs=pl.BlockSpec((1, tq, width), lambda bi, p, i: (bi, i, p)),
        out_shape=jax.ShapeDtypeStruct((b, s, w), BF16),
        scratch_shapes=[
            pltpu.VMEM((n_chains, 1, CHAIN), F32),
            pltpu.VMEM((n_chains, SB_DIM, CHAIN), F32),
        ],
        compiler_params=_compiler_params(3),
        name="l0_sb_attn",
    )(q, k, vt)


def _out_ffn_kernel(*refs, n_mix):
    x_ref = refs[0]
    o_refs = refs[1:1 + n_mix]
    wo_refs = refs[1 + n_mix:1 + 2 * n_mix]
    g_ref, wg_ref, wu_ref, wd_ref, y_ref = refs[1 + 2 * n_mix:]
    y = x_ref[...]
    for o_ref, wo_ref in zip(o_refs, wo_refs):
        y = y + _dot(o_ref[...], wo_ref[...])
    h = _rms(y, g_ref[...], D_MODEL).astype(BF16)
    gate = _dot(h, wg_ref[...])
    up = _dot(h, wu_ref[...])
    act = gate * (1.0 / (1.0 + jnp.exp(-gate))) * up
    y_ref[...] = y + _dot(act.astype(BF16), wd_ref[...])


def _out_ffn_call(x2, mixes, wos, g, wg, wu, wd, name):
    n, d = x2.shape
    tm = FFN_TILE
    n_mix = len(mixes)
    row = lambda width: pl.BlockSpec((tm, width), lambda i: (i, 0))
    return pl.pallas_call(
        functools.partial(_out_ffn_kernel, n_mix=n_mix),
        grid=(n // tm,),
        in_specs=([row(d)] + [row(m.shape[1]) for m in mixes]
                  + [_full(w.shape) for w in wos]
                  + [_full((1, d)), _full(wg.shape), _full(wu.shape), _full(wd.shape)]),
        out_specs=row(d),
        out_shape=jax.ShapeDtypeStruct((n, d), F32),
        compiler_params=_compiler_params(1),
        name=name,
    )(x2, *mixes, *wos, g, wg, wu, wd)


def _in1_kernel(x_ref, g_ref, w_ref, wvt_ref, fb_ref, gq_ref, gk_ref, layout_ref,
                qa_ref, ka_ref, vt_ref, carry_ref):
    i = pl.program_id(1)

    @pl.when(i == 0)
    def _():
        carry_ref[...] = jnp.zeros_like(carry_ref)

    n_terms = AUG_TERMS
    used = n_terms * FOX_HEADS
    rows = IN_SUB
    lane = _lane_iota(1)
    half_lane = lane & (FOX_DIM - 1)
    low_half = lane < FOX_DIM
    term = layout_ref[...]
    decay_lane = term < n_terms
    ones_lane = term == n_terms
    own_rows = [jnp.where((half_lane >= n_terms * hd) & (half_lane < n_terms * (hd + 1)), 1.0, 0.0)
                for hd in range(FOX_HEADS)]
    r_i = lax.broadcasted_iota(jnp.int32, (rows, rows), 0)
    c_i = lax.broadcasted_iota(jnp.int32, (rows, rows), 1)
    lower = (c_i <= r_i).astype(BF16)
    g = g_ref[...]
    gq = gq_ref[...]
    gk = gk_ref[...]

    deferred = []
    for r in range(ROW_TILE // IN_SUB):
        sl_rows = slice(r * rows, (r + 1) * rows)
        h = _rms(x_ref[0, sl_rows, :], g, D_MODEL).astype(BF16)
        proj = _dot(h, w_ref[...])
        deferred.append(h)
        q = proj[:, 0:FOX_WIDTH]
        k = proj[:, FOX_WIDTH:2 * FOX_WIDTH]
        f_logit = proj[:, 2 * FOX_WIDTH:]

        log_f = jnp.where(decay_lane, _log_sigmoid(f_logit + fb_ref[...]), 0.0)
        f1, f2, f3 = _split3(log_f)
        decay = carry_ref[...] + _dot(lower, f1) + _dot(lower, f2) + _dot(lower, f3)
        carry_ref[...] = decay[rows - 1:rows, :]
        d1, d2, d3 = (d.astype(F32) for d in _split3(decay * LOG2E))
        terms = jnp.where(term == 0, d1, jnp.where(term == 1, d2, d3))
        aug_q = jnp.where(decay_lane, terms, jnp.where(ones_lane, 1.0, 0.0))
        neg_terms = -terms

        for pair in range(FOX_HEADS // 2):
            sl = slice(pair * LANES, (pair + 1) * LANES)
            qp = q[:, sl]
            kp = k[:, sl]
            qsq = qp * qp
            ksq = kp * kp
            q_gain = qp * gq
            k_gain = kp * gk
            for parity in range(2):
                hd = 2 * pair + parity
                in_head = low_half if parity == 0 else jnp.logical_not(low_half)

                def inv_rms(sq):
                    ss = jnp.sum(jnp.where(in_head, sq, 0.0), axis=-1, keepdims=True)
                    return lax.rsqrt(ss * (1.0 / FOX_DIM) + EPS)

                qa_ref[0, hd, sl_rows, :] = jnp.where(
                    in_head, q_gain * inv_rms(qsq), aug_q).astype(BF16)
                moved = pltpu.roll(neg_terms, used - n_terms * hd, 1)
                aug_k = jnp.where(ones_lane, moved, own_rows[hd])
                ka_ref[0, hd, sl_rows, :] = jnp.where(
                    in_head, k_gain * inv_rms(ksq), aug_k).astype(BF16)

    h_all = jnp.concatenate(deferred, axis=0)
    vt_ref[0, 0] = _dot_nt(wvt_ref[...], h_all).astype(BF16)


def _in1_call(x, g, w, wvt, fb, gq, gk, layout):
    b, s, _ = x.shape
    tm = ROW_TILE
    nt = s // tm
    head_spec = pl.BlockSpec((1, FOX_HEADS, tm, LANES), lambda bi, i: (bi, 0, i, 0))
    lane_vec = _full((1, LANES))
    return pl.pallas_call(
        _in1_kernel,
        grid=(b, nt),
        in_specs=[
            pl.BlockSpec((1, tm, D_MODEL), lambda bi, i: (bi, i, 0)),
            _full((1, D_MODEL)),
            _full(w.shape),
            _full(wvt.shape),
            lane_vec, lane_vec, lane_vec, lane_vec,
        ],
        out_specs=[head_spec, head_spec,
                   pl.BlockSpec((1, 1, FOX_WIDTH, tm), lambda bi, i: (bi, i, 0, 0))],
        out_shape=[
            jax.ShapeDtypeStruct((b, FOX_HEADS, s, LANES), BF16),
            jax.ShapeDtypeStruct((b, FOX_HEADS, s, LANES), BF16),
            jax.ShapeDtypeStruct((b, nt, FOX_WIDTH, tm), BF16),
        ],
        scratch_shapes=[pltpu.VMEM((1, LANES), F32)],
        compiler_params=_compiler_params(2),
        name="l1_in",
    )(x, g, w, wvt, fb, gq, gk, layout)


def _pad_lanes(v, width=LANES):
    return jnp.pad(v, (0, width - v.shape[0])).reshape(1, width)


def _rope_partner(cols):
    half = MLA_ROPE // 2
    return jnp.concatenate([-cols[..., half:], cols[..., :half]], axis=-1)


def _prep_layer0(w_in, w_q_up, w_kv_up, q_norm, k_norm):
    o = MLA_Q_RANK + MLA_KV_RANK
    pad_rope = lambda c: jnp.pad(c, ((0, 0),) * (c.ndim - 1) + ((MLA_NOPE, LANES - MLA_QK),))
    w_rope = w_in[:, o:o + MLA_ROPE]
    sb = w_in[:, o + MLA_ROPE:]
    w = jnp.concatenate([w_in[:, :o], pad_rope(w_rope), pad_rope(_rope_partner(w_rope)),
                         sb[:, :2 * SB_WIDTH]], axis=1)
    wsvt = sb[:, 2 * SB_WIDTH:].T
    wq3 = w_q_up.reshape(MLA_Q_RANK, MLA_HEADS, MLA_QK)
    wq = jnp.pad(wq3, ((0, 0), (0, 0), (0, LANES - MLA_QK)))
    wq_partner = pad_rope(_rope_partner(wq3[:, :, MLA_NOPE:]))
    wq = jnp.concatenate([wq.reshape(MLA_Q_RANK, -1), wq_partner.reshape(MLA_Q_RANK, -1)], axis=1)
    kv = w_kv_up.reshape(MLA_KV_RANK, MLA_HEADS, MLA_NOPE + MLA_V)
    wk = jnp.pad(kv[:, :, :MLA_NOPE], ((0, 0), (0, 0), (0, LANES - MLA_NOPE)))
    wk = wk.reshape(MLA_KV_RANK, MLA_HEADS * LANES)
    wvt = kv[:, :, MLA_NOPE:].reshape(MLA_KV_RANK, MLA_HEADS * MLA_V).T
    gq = q_norm * (MLA_QK ** -0.5 * LOG2E)
    half = MLA_ROPE // 2
    swap_halves = lambda r: jnp.concatenate([r[half:], r[:half]])
    gqp = pad_rope(swap_halves(gq[MLA_NOPE:]))
    gkp = pad_rope(swap_halves(k_norm[MLA_NOPE:]))
    inv_freq = ROPE_THETA ** (-jnp.arange(half, dtype=F32) / half)
    invf = pad_rope(jnp.concatenate([inv_freq, inv_freq]))
    invf = jnp.stack([jnp.roll(invf, MLA_ROPE * gi) for gi in range(TRIG_PACK)])
    lanes = lambda v: v.reshape(1, LANES)
    return (w.astype(BF16), wsvt.astype(BF16), wq.astype(BF16), wk.astype(BF16), wvt.astype(BF16),
            _pad_lanes(gq), lanes(gqp), _pad_lanes(k_norm), lanes(gkp), invf)


def _prep_layer1(w_in, f_bias, q_norm, k_norm):
    wf = w_in[:, 3 * FOX_WIDTH:]
    rep = jnp.repeat(wf, AUG_TERMS, axis=1)
    pad = jnp.zeros((wf.shape[0], FOX_DIM - AUG_TERMS * FOX_HEADS), wf.dtype)
    f_group = jnp.concatenate([rep, pad, rep, pad], axis=1)
    w = jnp.concatenate([w_in[:, :2 * FOX_WIDTH], f_group], axis=1)
    wvt = w_in[:, 2 * FOX_WIDTH:3 * FOX_WIDTH].T
    fb = jnp.repeat(f_bias, AUG_TERMS)
    fpad = jnp.zeros((FOX_DIM - AUG_TERMS * FOX_HEADS,), f_bias.dtype)
    fb = jnp.concatenate([fb, fpad, fb, fpad]).reshape(1, LANES)
    gain2 = lambda gvec: jnp.concatenate([gvec, gvec]).reshape(1, LANES)
    used = AUG_TERMS * FOX_HEADS
    half_lane = jnp.arange(LANES, dtype=jnp.int32) % FOX_DIM
    layout = jnp.where(half_lane < used, half_lane % AUG_TERMS,
                       jnp.where(half_lane < used + AUG_TERMS, AUG_TERMS, AUG_TERMS + 1))
    return (w.astype(BF16), wvt.astype(BF16), fb,
            gain2(q_norm * (FOX_DIM ** -0.5 * LOG2E)), gain2(k_norm), layout.reshape(1, LANES))


def kernel(x, positions, l0_attn_norm, l0_w_in, l0_mla_q_a_norm, l0_mla_w_q_up, l0_mla_kv_a_norm, l0_mla_w_kv_up, l0_mla_q_norm, l0_mla_k_norm, l0_w_o, l0_ffn_norm, l0_w_gate, l0_w_up, l0_w_down, l1_attn_norm, l1_w_in, l1_fox_f_bias, l1_fox_q_norm, l1_fox_k_norm, l1_w_o, l1_ffn_norm, l1_w_gate, l1_w_up, l1_w_down):
    b, s, d = x.shape
    assert d == D_MODEL and s % Q_TILE == 0 and K_TILE == ROW_TILE and Q_TILE % K_TILE == 0
    assert K_TILE % CHAIN == 0 and CHAIN % SB_SUB == 0
    row = lambda v: v.reshape(1, -1)

    w0, wsvt, wq, wk, wvt0, gq0, gqp0, gk0, gkp0, invf = _prep_layer0(
        l0_w_in, l0_mla_w_q_up, l0_mla_w_kv_up, l0_mla_q_norm, l0_mla_k_norm)
    ffn_f32 = [l0_w_gate, l0_w_up, l0_w_down, l1_w_gate, l1_w_up, l1_w_down]
    qm, km, vtm, sq, sk, vts, *ffn_bf16 = _in0_call(
        x, positions.reshape(b, s, 1), row(l0_attn_norm), w0, wsvt, row(l0_mla_q_a_norm), wq,
        row(l0_mla_kv_a_norm), wk, wvt0, gq0, gqp0, gk0, gkp0, invf, ffn_f32)
    wg0, wu0, wd0, wg1, wu1, wd1 = ffn_bf16
    o_mla = _softmax_attn_call(qm, km, vtm, "l0_mla_attn")
    o_sb = _sb_attn_call(sq, sk, vts)
    n_mla = MLA_HEADS * MLA_V
    x1 = _out_ffn_call(
        x.reshape(b * s, d),
        [o_mla.reshape(b * s, n_mla), o_sb.reshape(b * s, SB_WIDTH)],
        [l0_w_o[:n_mla].astype(BF16), l0_w_o[n_mla:].astype(BF16)],
        row(l0_ffn_norm), wg0, wu0, wd0, "l0_out_ffn")

    w1, wvt1, fb, gq1, gk1, layout = _prep_layer1(
        l1_w_in, l1_fox_f_bias, l1_fox_q_norm, l1_fox_k_norm)
    qa, ka, vtf = _in1_call(
        x1.reshape(b, s, d), row(l1_attn_norm), w1, wvt1, fb, gq1, gk1, layout)
    o_fox = _softmax_attn_call(qa, ka, vtf, "l1_fox_attn")
    out = _out_ffn_call(
        x1, [o_fox.reshape(b * s, FOX_WIDTH)], [l1_w_o.astype(BF16)],
        row(l1_ffn_norm), wg1, wu1, wd1, "l1_out_ffn")
    return out.reshape(b, s, d)
```

```python
import functools

import jax
import jax.numpy as jnp
from jax import lax
from jax.experimental import pallas as pl
from jax.experimental.pallas import tpu as pltpu

F32 = jnp.float32
BF16 = jnp.bfloat16

D_MODEL = 1024
EPS = 1e-6
NEG_INF = -1e30
ROPE_THETA = 10000.0

MLA_HEADS = 8
MLA_Q_RANK = 256
MLA_KV_RANK = 128
MLA_NOPE = 64
MLA_ROPE = 32
MLA_QK = MLA_NOPE + MLA_ROPE
MLA_V = 64
SB_HEADS = 8
SB_DIM = 64
SB_WIDTH = SB_HEADS * SB_DIM
FOX_HEADS = 16
FOX_DIM = 64
FOX_WIDTH = FOX_HEADS * FOX_DIM

LANES = 128
BF16_SUBLANES = 16
V7X_VMEM_BYTES = 64 * 1024 * 1024
VMEM_LIMIT = V7X_VMEM_BYTES * 7 // 8

ROW_TILE = 512
K_TILE = 512
Q_TILE = 1024
SB_STEP_HEADS = 8
SOFTMAX_HEADS = 8
CHAIN = 256
FFN_TILE = 512
PREFETCH = 2
LOOKAHEAD = 3
DIAG_LOOKAHEAD = 7
SB_LOOKAHEAD = 3
ONES_ROWS = 16
SB_SUB = 128
LOG2E = 1.4426950408889634
RUN_FLOOR = -160.0

AUG_TERMS = 3
TRIG_PACK = LANES // MLA_ROPE
IN_SUB = 256


def _compiler_params(n_axes):
    return pltpu.CompilerParams(
        dimension_semantics=("arbitrary",) * n_axes,
        vmem_limit_bytes=VMEM_LIMIT,
    )


def _full(shape):
    return pl.BlockSpec(shape, lambda *_: (0,) * len(shape), pipeline_mode=pl.Buffered(1))


def _lane_iota(rows):
    return lax.broadcasted_iota(jnp.int32, (rows, LANES), 1)


def _rms(x, gain, width):
    ss = jnp.sum(x * x, axis=-1, keepdims=True)
    return x * lax.rsqrt(ss * (1.0 / width) + EPS) * gain


def _log_sigmoid(z):
    return jnp.minimum(z, 0.0) - jnp.log(1.0 + jnp.exp(-jnp.abs(z)))


def _split3(x):
    a = x.astype(BF16)
    r = x - a.astype(F32)
    b = r.astype(BF16)
    c = (r - b.astype(F32)).astype(BF16)
    return a, b, c


def _dot(a, b):
    return jnp.dot(a, b, preferred_element_type=F32)


def _dot_nt(a, b):
    return lax.dot_general(a, b, (((1,), (1,)), ((), ())), preferred_element_type=F32)


def _in0_kernel(*refs, n_cast):
    n_in = 15
    (x_ref, pos_ref, g_ref, w_ref, wsvt_ref, qan_ref, wq_ref, kvan_ref, wk_ref,
     wvt_ref, gq_ref, gqp_ref, gk_ref, gkp_ref, invf_ref) = refs[:n_in]
    cast_in = refs[n_in:n_in + n_cast]
    qm_ref, km_ref, vtm_ref, sq_ref, sk_ref, vts_ref = refs[n_in + n_cast:n_in + n_cast + 6]
    cast_out = refs[n_in + n_cast + 6:]
    for src, dst in zip(cast_in, cast_out):
        dst[...] = src[...].astype(dst.dtype)
    g = g_ref[...]
    lane = _lane_iota(1)
    rope_lane = (lane >= MLA_NOPE) & (lane < MLA_QK)
    deferred = []
    for r in range(ROW_TILE // IN_SUB):
        rows = slice(r * IN_SUB, (r + 1) * IN_SUB)
        h = _rms(x_ref[0, rows, :], g, D_MODEL).astype(BF16)
        proj = _dot(h, w_ref[...])
        o = MLA_Q_RANK + MLA_KV_RANK
        cq = proj[:, 0:MLA_Q_RANK]
        ckv = proj[:, MLA_Q_RANK:o]
        k_rope = proj[:, o:o + LANES]
        k_rope_partner = proj[:, o + LANES:o + 2 * LANES]
        o += 2 * LANES
        sq_ref[0, rows, :] = (proj[:, o:o + SB_WIDTH] * (SB_DIM ** -0.5 * LOG2E)).astype(BF16)
        sk_ref[0, rows, :] = proj[:, o + SB_WIDTH:o + 2 * SB_WIDTH].astype(BF16)

        qn = _rms(cq, qan_ref[...], MLA_Q_RANK).astype(BF16)
        q_all = _dot(qn, wq_ref[...])
        kvn = _rms(ckv, kvan_ref[...], MLA_KV_RANK).astype(BF16)
        k_all = _dot(kvn, wk_ref[...])
        deferred.append((h, kvn))

        pos = pos_ref[0, rows, :].astype(F32)
        group = IN_SUB // TRIG_PACK
        packed = pos[0:group] * invf_ref[0:1, :]
        for gi in range(1, TRIG_PACK):
            packed = packed + pos[gi * group:(gi + 1) * group] * invf_ref[gi:gi + 1, :]

        def unpack(t, fill):
            parts = [t] + [pltpu.roll(t, LANES - MLA_ROPE * gi, 1) for gi in range(1, TRIG_PACK)]
            return jnp.where(rope_lane, jnp.concatenate(parts, axis=0), fill)

        cos = unpack(jnp.cos(packed), 1.0)
        sin = unpack(jnp.sin(packed), 0.0)
        gq_cos = gq_ref[...] * cos
        gq_sin = gqp_ref[...] * sin
        gk_cos = gk_ref[...] * cos
        k_partner_term = k_rope_partner * (gkp_ref[...] * sin)

        def inv_rms(t):
            ss = jnp.sum(t * t, axis=-1, keepdims=True)
            return lax.rsqrt(ss * (1.0 / MLA_QK) + EPS)

        for hd in range(MLA_HEADS):
            sl = slice(hd * LANES, (hd + 1) * LANES)
            t = q_all[:, sl]
            t_partner = q_all[:, MLA_HEADS * LANES + hd * LANES:MLA_HEADS * LANES + (hd + 1) * LANES]
            qm_ref[0, hd, rows, :] = (
                (t * gq_cos + t_partner * gq_sin) * inv_rms(t)).astype(BF16)
            tk = k_all[:, sl] + k_rope
            km_ref[0, hd, rows, :] = (
                (tk * gk_cos + k_partner_term) * inv_rms(tk)).astype(BF16)

    h_all = jnp.concatenate([h for h, _ in deferred], axis=0)
    kvn_all = jnp.concatenate([kvn for _, kvn in deferred], axis=0)
    vts_ref[0, 0] = _dot_nt(wsvt_ref[...], h_all).astype(BF16)
    vtm_ref[0, 0] = _dot_nt(wvt_ref[...], kvn_all).astype(BF16)


def _in0_call(x, pos3, g, w, wsvt, qan, wq, kvan, wk, wvt, gq, gqp, gk, gkp, invf, to_cast):
    b, s, _ = x.shape
    tm = ROW_TILE
    nt = s // tm
    steps = b * nt

    def cast_spec(a):
        n_blocks = steps
        while a.shape[0] % n_blocks or (a.shape[0] // n_blocks) % BF16_SUBLANES:
            n_blocks //= 2
        share = steps // n_blocks
        return pl.BlockSpec((a.shape[0] // n_blocks, a.shape[1]),
                            lambda bi, i: ((bi * nt + i) // share, 0))

    cast_specs = [cast_spec(a) for a in to_cast]
    head_spec = pl.BlockSpec((1, MLA_HEADS, tm, LANES), lambda bi, i: (bi, 0, i, 0))
    vt_spec = pl.BlockSpec((1, 1, SB_WIDTH, tm), lambda bi, i: (bi, i, 0, 0))
    row_spec = pl.BlockSpec((1, tm, SB_WIDTH), lambda bi, i: (bi, i, 0))
    lane_vec = _full((1, LANES))
    return pl.pallas_call(
        functools.partial(_in0_kernel, n_cast=len(to_cast)),
        grid=(b, nt),
        in_specs=[
            pl.BlockSpec((1, tm, D_MODEL), lambda bi, i: (bi, i, 0)),
            pl.BlockSpec((1, tm, 1), lambda bi, i: (bi, i, 0)),
            _full((1, D_MODEL)),
            _full(w.shape),
            _full(wsvt.shape),
            _full((1, MLA_Q_RANK)),
            _full(wq.shape),
            _full((1, MLA_KV_RANK)),
            _full(wk.shape),
            _full(wvt.shape),
            lane_vec, lane_vec, lane_vec, lane_vec, _full((TRIG_PACK, LANES)),
        ] + cast_specs,
        out_specs=[head_spec, head_spec, vt_spec, row_spec, row_spec, vt_spec] + cast_specs,
        out_shape=[
            jax.ShapeDtypeStruct((b, MLA_HEADS, s, LANES), BF16),
            jax.ShapeDtypeStruct((b, MLA_HEADS, s, LANES), BF16),
            jax.ShapeDtypeStruct((b, nt, MLA_HEADS * MLA_V, tm), BF16),
            jax.ShapeDtypeStruct((b, s, SB_WIDTH), BF16),
            jax.ShapeDtypeStruct((b, s, SB_WIDTH), BF16),
            jax.ShapeDtypeStruct((b, nt, SB_WIDTH, tm), BF16),
        ] + [jax.ShapeDtypeStruct(a.shape, BF16) for a in to_cast],
        compiler_params=_compiler_params(2),
        name="l0_in",
    )(x, pos3, g, w, wsvt, qan, wq, kvan, wk, wvt, gq, gqp, gk, gkp, invf, *to_cast)


def _diag_spans(n_chains, chain, tk):
    for c in range(n_chains):
        end = (c + 1) * chain
        for d in range(-(-end // tk)):
            nk = min(tk, end - d * tk)
            yield c, d, nk, d * tk + nk > c * chain


def _softmax_attn_kernel(q_ref, k_ref, vt_ref, o_ref, m_scr, acc_scr, pre_scr, *, head_dim):
    tq, tk, cw = Q_TILE, K_TILE, CHAIN
    n_c = tq // cw
    kt_per_q = tq // tk
    qi = pl.program_id(2)
    chains = [(hh, c) for hh in range(SOFTMAX_HEADS) for c in range(n_c)]
    qs = [q_ref[0, hh, c * cw:(c + 1) * cw, :] for hh, c in chains]

    m_scr[...] = jnp.full(m_scr.shape, NEG_INF, F32)
    acc_scr[...] = jnp.zeros(acc_scr.shape, F32)

    def with_ones(vt):
        return jnp.concatenate([vt, jnp.ones((ONES_ROWS, vt.shape[1]), vt.dtype)], axis=0)

    def update(idx, s, vt, visible):
        if visible is not None:
            s = jnp.where(visible, s, NEG_INF)
        m = m_scr[idx]
        m_new = jnp.maximum(m, jnp.max(s, axis=0, keepdims=True))
        p = jnp.exp2(s - m_new)
        m_scr[idx] = m_new
        acc_scr[idx] = jnp.exp2(m - m_new) * acc_scr[idx] + _dot(vt, p.astype(BF16))

    def scores_of(idx, tile, nk=tk):
        hh = chains[idx][0]
        return _dot_nt(k_ref[0, hh, pl.ds(pl.multiple_of(tile * tk, tk), nk), :], qs[idx])

    def run_pipeline(items, tail_fns, ahead):
        fns = [score_fn for score_fn, _ in items] + list(tail_fns)
        pending = {w: fns[w]() for w in range(min(ahead, len(fns)))}
        for v, (_, consume) in enumerate(items):
            if v + ahead < len(fns):
                pending[v + ahead] = fns[v + ahead]()
            consume(pending.pop(v))

    def prefetch_fns(tile):
        def store(a):
            def fn():
                pre_scr[a] = scores_of(a, tile)
            return fn
        return [store(a) for a in range(PREFETCH)]

    n_full = qi * kt_per_q
    items = []
    for hh in range(SOFTMAX_HEADS):
        for c, d, nk, mask in _diag_spans(n_c, cw, tk):
            idx = hh * n_c + c
            visible = None
            if mask:
                row = lax.broadcasted_iota(jnp.int32, (nk, cw), 0) + d * tk
                col = lax.broadcasted_iota(jnp.int32, (nk, cw), 1) + c * cw
                visible = row <= col

            def score_fn(idx=idx, d=d, nk=nk):
                return scores_of(idx, n_full + d, nk)

            def consume(s, idx=idx, hh=hh, d=d, nk=nk, visible=visible):
                vt = vt_ref[0, n_full + d, hh * head_dim:(hh + 1) * head_dim, 0:nk]
                update(idx, s, with_ones(vt), visible)

            items.append((score_fn, consume))
    run_pipeline(items, prefetch_fns(0), DIAG_LOOKAHEAD)

    def full_step(i, _):
        items = []
        for u in range(kt_per_q):
            j = i * kt_per_q + u
            for idx, (hh, _) in enumerate(chains):
                if u == 0 and idx < PREFETCH:
                    score_fn = lambda idx=idx: pre_scr[idx]
                else:
                    score_fn = lambda idx=idx, j=j: scores_of(idx, j)

                def consume(s, idx=idx, hh=hh, j=j):
                    vt = vt_ref[0, j, hh * head_dim:(hh + 1) * head_dim, :]
                    update(idx, s, with_ones(vt), None)

                items.append((score_fn, consume))
        run_pipeline(items, prefetch_fns(jnp.minimum(i + 1, qi - 1) * kt_per_q), LOOKAHEAD)
        return 0

    lax.fori_loop(0, qi, full_step, 0)

    def normalized(idx):
        acc = acc_scr[idx]
        return acc[:head_dim] / acc[head_dim:head_dim + 1]

    outs = [jnp.concatenate([normalized(hh * n_c + c) for c in range(n_c)], axis=1)
            for hh in range(SOFTMAX_HEADS)]
    o_ref[0] = jnp.concatenate(outs, axis=0).astype(o_ref.dtype)


def _softmax_attn_call(q, k, vt, name):
    b, h, s, _ = q.shape
    tq = Q_TILE
    nkt = s // K_TILE
    head_dim = vt.shape[2] // h
    hps = SOFTMAX_HEADS
    n_chains = hps * tq // CHAIN
    return pl.pallas_call(
        functools.partial(_softmax_attn_kernel, head_dim=head_dim),
        grid=(b, h // hps, s // tq),
        in_specs=[
            pl.BlockSpec((1, hps, tq, LANES), lambda bi, p, i: (bi, p, i, 0)),
            pl.BlockSpec((1, hps, s, LANES), lambda bi, p, i: (bi, p, 0, 0)),
            pl.BlockSpec((1, nkt, hps * head_dim, K_TILE), lambda bi, p, i: (bi, 0, p, 0)),
        ],
        out_specs=pl.BlockSpec((1, hps * head_dim, tq), lambda bi, p, i: (bi, p, i)),
        out_shape=jax.ShapeDtypeStruct((b, h * head_dim, s), BF16),
        scratch_shapes=[
            pltpu.VMEM((n_chains, 1, CHAIN), F32),
            pltpu.VMEM((n_chains, head_dim + ONES_ROWS, CHAIN), F32),
            pltpu.VMEM((PREFETCH, K_TILE, CHAIN), F32),
        ],
        compiler_params=_compiler_params(3),
        name=name,
    )(q, k, vt)


def _sb_attn_kernel(q_ref, k_ref, vt_ref, o_ref, run_scr, acc_scr):
    tq, tk, cw, sub = Q_TILE, K_TILE, CHAIN, SB_SUB
    n_c = tq // cw
    kt_per_q = tq // tk
    qi = pl.program_id(2)
    s_row = lax.broadcasted_iota(jnp.int32, (sub, sub), 0)
    s_col = lax.broadcasted_iota(jnp.int32, (sub, sub), 1)
    neg_later = -((s_col > s_row).astype(BF16))
    lane = _lane_iota(cw)
    heads = SB_STEP_HEADS
    chains = [(hh, c) for hh in range(heads) for c in range(n_c)]

    def pair_lanes(hh):
        return slice((hh // 2) * LANES, (hh // 2 + 1) * LANES)

    qs = []
    for hh, c in chains:
        q_pair = q_ref[0, c * cw:(c + 1) * cw, pair_lanes(hh)]
        in_head = (lane >= (hh % 2) * SB_DIM) & (lane < (hh % 2 + 1) * SB_DIM)
        qs.append(jnp.where(in_head, q_pair, jnp.zeros_like(q_pair)))

    run_scr[...] = jnp.zeros(run_scr.shape, F32)
    acc_scr[...] = jnp.zeros(acc_scr.shape, F32)

    def suffix_sums(z, visible):
        cost = jnp.maximum(z, 0.0) + jnp.log(1.0 + jnp.exp2(-jnp.abs(z))) * LOG2E
        log_beta = z - cost
        if visible is not None:
            cost = jnp.where(visible, cost, 0.0)
        cost16 = cost.astype(BF16)
        blocks = [slice(b * sub, (b + 1) * sub) for b in range(z.shape[0] // sub)]
        suffix = [_dot(neg_later, cost16[bl]) for bl in blocks]
        return log_beta, cost, blocks, suffix

    def accumulate(idx, stage, vt, visible):
        log_beta, cost, blocks, suffix = stage
        run = run_scr[idx]
        terms = []
        for bl, sfx in reversed(list(zip(blocks, suffix))):
            terms.append(log_beta[bl] + sfx + run)
            run = run + sfx[0:1, :] - cost[bl.start:bl.start + 1, :]
        a = jnp.exp2(jnp.concatenate(terms[::-1], axis=0))
        if visible is not None:
            a = jnp.where(visible, a, 0.0)
        run_scr[idx] = run
        acc_scr[idx] = acc_scr[idx] + _dot(vt, a.astype(BF16))

    def tile_update(idxs, fns, vts, masks):
        n = len(idxs)
        pending, stages = {}, {}
        for w in range(min(SB_LOOKAHEAD, len(fns))):
            pending[w] = fns[w]()
        for v in range(n + 1):
            if v + SB_LOOKAHEAD < len(fns):
                pending[v + SB_LOOKAHEAD] = fns[v + SB_LOOKAHEAD]()
            if v < n:
                stages[v] = suffix_sums(pending.pop(v), masks[v])
            if v >= 1:
                accumulate(idxs[v - 1], stages.pop(v - 1), vts[v - 1], masks[v - 1])

    def scores_fn(idx, start, nk):
        return lambda: _dot_nt(
            k_ref[0, pl.ds(pl.multiple_of(start, cw), nk), pair_lanes(chains[idx][0])], qs[idx])

    q0 = qi * tq
    tri_row = lax.broadcasted_iota(jnp.int32, (cw, cw), 0)
    tri_col = lax.broadcasted_iota(jnp.int32, (cw, cw), 1)
    idxs, fns, vts, masks = [], [], [], []
    covered_from = {}
    for phase in range(2):
        for hh in range(heads):
            for c in range(n_c):
                idx = hh * n_c + c
                rel = (c - phase) * cw
                start = jnp.maximum(q0 + rel, 0)
                if phase == 0:
                    visible = tri_row < tri_col
                elif rel < 0:
                    visible = jnp.broadcast_to(qi > 0, (cw, cw))
                else:
                    visible = None
                idxs.append(idx)
                fns.append(scores_fn(idx, start, cw))
                vts.append(vt_ref[0, start // tk, hh * SB_DIM:(hh + 1) * SB_DIM,
                                  rel % tk:rel % tk + cw])
                masks.append(visible)
                covered_from[idx] = q0 + rel
    tile_update(idxs, fns, vts, masks)

    all_idx = list(range(len(chains)))
    key_row = lax.broadcasted_iota(jnp.int32, (tk, cw), 0)

    def alive_at(j):
        flags = [jnp.logical_and(covered_from[idx] > j * tk, jnp.max(run_scr[idx]) > RUN_FLOOR)
                 for idx in all_idx]
        return functools.reduce(jnp.logical_or, flags)

    def step_and_test(carry):
        j, _ = carry
        tile_update(all_idx, [scores_fn(idx, j * tk, tk) for idx in all_idx],
                    [vt_ref[0, j, chains[idx][0] * SB_DIM:(chains[idx][0] + 1) * SB_DIM, :]
                     for idx in all_idx],
                    [key_row + j * tk < covered_from[idx] for idx in all_idx])
        return j - 1, alive_at(j - 1)

    top = qi * kt_per_q + ((n_c - 2) * cw - 1) // tk
    lax.while_loop(lambda carry: jnp.logical_and(carry[0] >= 0, carry[1]),
                   step_and_test, (top, alive_at(top)))
    outs = [jnp.concatenate([acc_scr[hh * n_c + c] for c in range(n_c)], axis=1)
            for hh in range(heads)]
    o_ref[0] = jnp.concatenate(outs, axis=0).astype(o_ref.dtype)


def _sb_attn_call(q, k, vt):
    b, s, w = q.shape
    tq = Q_TILE
    nkt = s // K_TILE
    width = SB_STEP_HEADS * SB_DIM
    n_chains = SB_STEP_HEADS * tq // CHAIN
    return pl.pallas_call(
        _sb_attn_kernel,
        grid=(b, w // width, s // tq),
        in_specs=[
            pl.BlockSpec((1, tq, width), lambda bi, p, i: (bi, i, p)),
            pl.BlockSpec((1, s, width), lambda bi, p, i: (bi, 0, p)),
            pl.BlockSpec((1, nkt, width, K_TILE), lambda bi, p, i: (bi, 0, p, 0)),
        ],
        out_specs=pl.BlockSpec((1, width, tq), lambda bi, p, i: (bi, p, i)),
        out_shape=jax.ShapeDtypeStruct((b, w, s), BF16),
        scratch_shapes=[
            pltpu.VMEM((n_chains, 1, CHAIN), F32),
            pltpu.VMEM((n_chains, SB_DIM, CHAIN), F32),
        ],
        compiler_params=_compiler_params(3),
        name="l0_sb_attn",
    )(q, k, vt)


def _out_ffn_kernel(*refs, n_mix):
    x_ref = refs[0]
    o_refs = refs[1:1 + n_mix]
    wo_refs = refs[1 + n_mix:1 + 2 * n_mix]
    g_ref, wg_ref, wu_ref, wd_ref, y_ref = refs[1 + 2 * n_mix:]
    y = x_ref[...]
    for o_ref, wo_ref in zip(o_refs, wo_refs):
        y = y + lax.dot_general(o_ref[0], wo_ref[...], (((0,), (0,)), ((), ())),
                                preferred_element_type=F32)
    h = _rms(y, g_ref[...], D_MODEL).astype(BF16)
    gate = _dot(h, wg_ref[...])
    up = _dot(h, wu_ref[...])
    act = gate * (1.0 / (1.0 + jnp.exp(-gate))) * up
    y_ref[...] = y + _dot(act.astype(BF16), wd_ref[...])


def _out_ffn_call(x2, mixes, wos, g, wg, wu, wd, name):
    n, d = x2.shape
    tm = FFN_TILE
    n_mix = len(mixes)
    nt = mixes[0].shape[2] // tm
    row = lambda width: pl.BlockSpec((tm, width), lambda i: (i, 0))
    mix_spec = lambda m: pl.BlockSpec((1, m.shape[1], tm), lambda i: (i // nt, 0, i % nt))
    return pl.pallas_call(
        functools.partial(_out_ffn_kernel, n_mix=n_mix),
        grid=(n // tm,),
        in_specs=([row(d)] + [mix_spec(m) for m in mixes]
                  + [_full(w.shape) for w in wos]
                  + [_full((1, d)), _full(wg.shape), _full(wu.shape), _full(wd.shape)]),
        out_specs=row(d),
        out_shape=jax.ShapeDtypeStruct((n, d), F32),
        compiler_params=_compiler_params(1),
        name=name,
    )(x2, *mixes, *wos, g, wg, wu, wd)


def _in1_kernel(x_ref, g_ref, w_ref, wvt_ref, fb_ref, gq_ref, gk_ref, layout_ref,
                qa_ref, ka_ref, vt_ref, carry_ref):
    i = pl.program_id(1)

    @pl.when(i == 0)
    def _():
        carry_ref[...] = jnp.zeros_like(carry_ref)

    n_terms = AUG_TERMS
    used = n_terms * FOX_HEADS
    rows = IN_SUB
    lane = _lane_iota(1)
    half_lane = lane & (FOX_DIM - 1)
    low_half = lane < FOX_DIM
    term = layout_ref[...]
    decay_lane = term < n_terms
    ones_lane = term == n_terms
    own_rows = [jnp.where((half_lane >= n_terms * hd) & (half_lane < n_terms * (hd + 1)), 1.0, 0.0)
                for hd in range(FOX_HEADS)]
    r_i = lax.broadcasted_iota(jnp.int32, (rows, rows), 0)
    c_i = lax.broadcasted_iota(jnp.int32, (rows, rows), 1)
    lower = (c_i <= r_i).astype(BF16)
    g = g_ref[...]
    gq = gq_ref[...]
    gk = gk_ref[...]

    deferred = []
    for r in range(ROW_TILE // IN_SUB):
        sl_rows = slice(r * rows, (r + 1) * rows)
        h = _rms(x_ref[0, sl_rows, :], g, D_MODEL).astype(BF16)
        proj = _dot(h, w_ref[...])
        deferred.append(h)
        q = proj[:, 0:FOX_WIDTH]
        k = proj[:, FOX_WIDTH:2 * FOX_WIDTH]
        f_logit = proj[:, 2 * FOX_WIDTH:]

        log_f = jnp.where(decay_lane, _log_sigmoid(f_logit + fb_ref[...]), 0.0)
        f1, f2, f3 = _split3(log_f)
        decay = carry_ref[...] + _dot(lower, f1) + _dot(lower, f2) + _dot(lower, f3)
        carry_ref[...] = decay[rows - 1:rows, :]
        d1, d2, d3 = (d.astype(F32) for d in _split3(decay * LOG2E))
        terms = jnp.where(term == 0, d1, jnp.where(term == 1, d2, d3))
        aug_q = jnp.where(decay_lane, terms, jnp.where(ones_lane, 1.0, 0.0))
        neg_terms = -terms

        for pair in range(FOX_HEADS // 2):
            sl = slice(pair * LANES, (pair + 1) * LANES)
            qp = q[:, sl]
            kp = k[:, sl]
            qsq = qp * qp
            ksq = kp * kp
            q_gain = qp * gq
            k_gain = kp * gk
            for parity in range(2):
                hd = 2 * pair + parity
                in_head = low_half if parity == 0 else jnp.logical_not(low_half)

                def inv_rms(sq):
                    ss = jnp.sum(jnp.where(in_head, sq, 0.0), axis=-1, keepdims=True)
                    return lax.rsqrt(ss * (1.0 / FOX_DIM) + EPS)

                qa_ref[0, hd, sl_rows, :] = jnp.where(
                    in_head, q_gain * inv_rms(qsq), aug_q).astype(BF16)
                moved = pltpu.roll(neg_terms, used - n_terms * hd, 1)
                aug_k = jnp.where(ones_lane, moved, own_rows[hd])
                ka_ref[0, hd, sl_rows, :] = jnp.where(
                    in_head, k_gain * inv_rms(ksq), aug_k).astype(BF16)

    h_all = jnp.concatenate(deferred, axis=0)
    vt_ref[0, 0] = _dot_nt(wvt_ref[...], h_all).astype(BF16)


def _in1_call(x, g, w, wvt, fb, gq, gk, layout):
    b, s, _ = x.shape
    tm = ROW_TILE
    nt = s // tm
    head_spec = pl.BlockSpec((1, FOX_HEADS, tm, LANES), lambda bi, i: (bi, 0, i, 0))
    lane_vec = _full((1, LANES))
    return pl.pallas_call(
        _in1_kernel,
        grid=(b, nt),
        in_specs=[
            pl.BlockSpec((1, tm, D_MODEL), lambda bi, i: (bi, i, 0)),
            _full((1, D_MODEL)),
            _full(w.shape),
            _full(wvt.shape),
            lane_vec, lane_vec, lane_vec, lane_vec,
        ],
        out_specs=[head_spec, head_spec,
                   pl.BlockSpec((1, 1, FOX_WIDTH, tm), lambda bi, i: (bi, i, 0, 0))],
        out_shape=[
            jax.ShapeDtypeStruct((b, FOX_HEADS, s, LANES), BF16),
            jax.ShapeDtypeStruct((b, FOX_HEADS, s, LANES), BF16),
            jax.ShapeDtypeStruct((b, nt, FOX_WIDTH, tm), BF16),
        ],
        scratch_shapes=[pltpu.VMEM((1, LANES), F32)],
        compiler_params=_compiler_params(2),
        name="l1_in",
    )(x, g, w, wvt, fb, gq, gk, layout)


def _pad_lanes(v, width=LANES):
    return jnp.pad(v, (0, width - v.shape[0])).reshape(1, width)


def _rope_partner(cols):
    half = MLA_ROPE // 2
    return jnp.concatenate([-cols[..., half:], cols[..., :half]], axis=-1)


def _prep_layer0(w_in, w_q_up, w_kv_up, q_norm, k_norm):
    o = MLA_Q_RANK + MLA_KV_RANK
    pad_rope = lambda c: jnp.pad(c, ((0, 0),) * (c.ndim - 1) + ((MLA_NOPE, LANES - MLA_QK),))
    w_rope = w_in[:, o:o + MLA_ROPE]
    sb = w_in[:, o + MLA_ROPE:]
    w = jnp.concatenate([w_in[:, :o], pad_rope(w_rope), pad_rope(_rope_partner(w_rope)),
                         sb[:, :2 * SB_WIDTH]], axis=1)
    wsvt = sb[:, 2 * SB_WIDTH:].T
    wq3 = w_q_up.reshape(MLA_Q_RANK, MLA_HEADS, MLA_QK)
    wq = jnp.pad(wq3, ((0, 0), (0, 0), (0, LANES - MLA_QK)))
    wq_partner = pad_rope(_rope_partner(wq3[:, :, MLA_NOPE:]))
    wq = jnp.concatenate([wq.reshape(MLA_Q_RANK, -1), wq_partner.reshape(MLA_Q_RANK, -1)], axis=1)
    kv = w_kv_up.reshape(MLA_KV_RANK, MLA_HEADS, MLA_NOPE + MLA_V)
    wk = jnp.pad(kv[:, :, :MLA_NOPE], ((0, 0), (0, 0), (0, LANES - MLA_NOPE)))
    wk = wk.reshape(MLA_KV_RANK, MLA_HEADS * LANES)
    wvt = kv[:, :, MLA_NOPE:].reshape(MLA_KV_RANK, MLA_HEADS * MLA_V).T
    gq = q_norm * (MLA_QK ** -0.5 * LOG2E)
    half = MLA_ROPE // 2
    swap_halves = lambda r: jnp.concatenate([r[half:], r[:half]])
    gqp = pad_rope(swap_halves(gq[MLA_NOPE:]))
    gkp = pad_rope(swap_halves(k_norm[MLA_NOPE:]))
    inv_freq = ROPE_THETA ** (-jnp.arange(half, dtype=F32) / half)
    invf = pad_rope(jnp.concatenate([inv_freq, inv_freq]))
    invf = jnp.stack([jnp.roll(invf, MLA_ROPE * gi) for gi in range(TRIG_PACK)])
    lanes = lambda v: v.reshape(1, LANES)
    return (w.astype(BF16), wsvt.astype(BF16), wq.astype(BF16), wk.astype(BF16), wvt.astype(BF16),
            _pad_lanes(gq), lanes(gqp), _pad_lanes(k_norm), lanes(gkp), invf)


def _prep_layer1(w_in, f_bias, q_norm, k_norm):
    wf = w_in[:, 3 * FOX_WIDTH:]
    rep = jnp.repeat(wf, AUG_TERMS, axis=1)
    pad = jnp.zeros((wf.shape[0], FOX_DIM - AUG_TERMS * FOX_HEADS), wf.dtype)
    f_group = jnp.concatenate([rep, pad, rep, pad], axis=1)
    w = jnp.concatenate([w_in[:, :2 * FOX_WIDTH], f_group], axis=1)
    wvt = w_in[:, 2 * FOX_WIDTH:3 * FOX_WIDTH].T
    fb = jnp.repeat(f_bias, AUG_TERMS)
    fpad = jnp.zeros((FOX_DIM - AUG_TERMS * FOX_HEADS,), f_bias.dtype)
    fb = jnp.concatenate([fb, fpad, fb, fpad]).reshape(1, LANES)
    gain2 = lambda gvec: jnp.concatenate([gvec, gvec]).reshape(1, LANES)
    used = AUG_TERMS * FOX_HEADS
    half_lane = jnp.arange(LANES, dtype=jnp.int32) % FOX_DIM
    layout = jnp.where(half_lane < used, half_lane % AUG_TERMS,
                       jnp.where(half_lane < used + AUG_TERMS, AUG_TERMS, AUG_TERMS + 1))
    return (w.astype(BF16), wvt.astype(BF16), fb,
            gain2(q_norm * (FOX_DIM ** -0.5 * LOG2E)), gain2(k_norm), layout.reshape(1, LANES))


def kernel(x, positions, l0_attn_norm, l0_w_in, l0_mla_q_a_norm, l0_mla_w_q_up, l0_mla_kv_a_norm, l0_mla_w_kv_up, l0_mla_q_norm, l0_mla_k_norm, l0_w_o, l0_ffn_norm, l0_w_gate, l0_w_up, l0_w_down, l1_attn_norm, l1_w_in, l1_fox_f_bias, l1_fox_q_norm, l1_fox_k_norm, l1_w_o, l1_ffn_norm, l1_w_gate, l1_w_up, l1_w_down):
    b, s, d = x.shape
    assert d == D_MODEL and s % Q_TILE == 0 and K_TILE == ROW_TILE and Q_TILE % K_TILE == 0
    assert K_TILE % CHAIN == 0 and CHAIN % SB_SUB == 0
    row = lambda v: v.reshape(1, -1)

    w0, wsvt, wq, wk, wvt0, gq0, gqp0, gk0, gkp0, invf = _prep_layer0(
        l0_w_in, l0_mla_w_q_up, l0_mla_w_kv_up, l0_mla_q_norm, l0_mla_k_norm)
    ffn_f32 = [l0_w_gate, l0_w_up, l0_w_down, l1_w_gate, l1_w_up, l1_w_down]
    qm, km, vtm, sq, sk, vts, *ffn_bf16 = _in0_call(
        x, positions.reshape(b, s, 1), row(l0_attn_norm), w0, wsvt, row(l0_mla_q_a_norm), wq,
        row(l0_mla_kv_a_norm), wk, wvt0, gq0, gqp0, gk0, gkp0, invf, ffn_f32)
    wg0, wu0, wd0, wg1, wu1, wd1 = ffn_bf16
    o_mla = _softmax_attn_call(qm, km, vtm, "l0_mla_attn")
    o_sb = _sb_attn_call(sq, sk, vts)
    n_mla = MLA_HEADS * MLA_V
    x1 = _out_ffn_call(
        x.reshape(b * s, d),
        [o_mla, o_sb],
        [l0_w_o[:n_mla].astype(BF16), l0_w_o[n_mla:].astype(BF16)],
        row(l0_ffn_norm), wg0, wu0, wd0, "l0_out_ffn")

    w1, wvt1, fb, gq1, gk1, layout = _prep_layer1(
        l1_w_in, l1_fox_f_bias, l1_fox_q_norm, l1_fox_k_norm)
    qa, ka, vtf = _in1_call(
        x1.reshape(b, s, d), row(l1_attn_norm), w1, wvt1, fb, gq1, gk1, layout)
    o_fox = _softmax_attn_call(qa, ka, vtf, "l1_fox_attn")
    out = _out_ffn_call(
        x1, [o_fox], [l1_w_o.astype(BF16)],
        row(l1_ffn_norm), wg1, wu1, wd1, "l1_out_ffn")
    return out.reshape(b, s, d)
```

```python
import functools

import jax
import jax.numpy as jnp
from jax import lax
from jax.experimental import pallas as pl
from jax.experimental.pallas import tpu as pltpu

F32 = jnp.float32
BF16 = jnp.bfloat16

D_MODEL = 1024
EPS = 1e-6
NEG_INF = -1e30
ROPE_THETA = 10000.0

MLA_HEADS = 8
MLA_Q_RANK = 256
MLA_KV_RANK = 128
MLA_NOPE = 64
MLA_ROPE = 32
MLA_QK = MLA_NOPE + MLA_ROPE
MLA_V = 64
SB_HEADS = 8
SB_DIM = 64
SB_WIDTH = SB_HEADS * SB_DIM
FOX_HEADS = 16
FOX_DIM = 64
FOX_WIDTH = FOX_HEADS * FOX_DIM

LANES = 128
BF16_SUBLANES = 16
V7X_VMEM_BYTES = 64 * 1024 * 1024
VMEM_LIMIT = V7X_VMEM_BYTES * 7 // 8

ROW_TILE = 512
K_TILE = 512
Q_TILE = 1024
SB_STEP_HEADS = 8
SOFTMAX_HEADS = 8
CHAIN = 256
FFN_TILE = 512
PREFETCH = 2
LOOKAHEAD = 3
DIAG_LOOKAHEAD = 7
SB_LOOKAHEAD = 3
ONES_ROWS = 16
SB_SUB = 128
LOG2E = 1.4426950408889634
RUN_FLOOR = -160.0

AUG_TERMS = 3
TRIG_PACK = LANES // MLA_ROPE
IN_SUB = 256


def _compiler_params(n_axes):
    return pltpu.CompilerParams(
        dimension_semantics=("arbitrary",) * n_axes,
        vmem_limit_bytes=VMEM_LIMIT,
    )


def _full(shape):
    return pl.BlockSpec(shape, lambda *_: (0,) * len(shape), pipeline_mode=pl.Buffered(1))


def _lane_iota(rows):
    return lax.broadcasted_iota(jnp.int32, (rows, LANES), 1)


def _rms(x, gain, width):
    ss = jnp.sum(x * x, axis=-1, keepdims=True)
    return x * lax.rsqrt(ss * (1.0 / width) + EPS) * gain


def _log_sigmoid(z):
    return jnp.minimum(z, 0.0) - jnp.log(1.0 + jnp.exp(-jnp.abs(z)))


def _split3(x):
    a = x.astype(BF16)
    r = x - a.astype(F32)
    b = r.astype(BF16)
    c = (r - b.astype(F32)).astype(BF16)
    return a, b, c


def _dot(a, b):
    return jnp.dot(a, b, preferred_element_type=F32)


def _dot_nt(a, b):
    return lax.dot_general(a, b, (((1,), (1,)), ((), ())), preferred_element_type=F32)


def _in0_kernel(*refs, n_cast):
    n_in = 15
    (x_ref, pos_ref, g_ref, w_ref, wsvt_ref, qan_ref, wq_ref, kvan_ref, wk_ref,
     wvt_ref, gq_ref, gqp_ref, gk_ref, gkp_ref, invf_ref) = refs[:n_in]
    cast_in = refs[n_in:n_in + n_cast]
    qm_ref, km_ref, vtm_ref, sq_ref, sk_ref, vts_ref = refs[n_in + n_cast:n_in + n_cast + 6]
    cast_out = refs[n_in + n_cast + 6:]
    for src, dst in zip(cast_in, cast_out):
        dst[...] = src[...].astype(dst.dtype)
    g = g_ref[...]
    lane = _lane_iota(1)
    rope_lane = (lane >= MLA_NOPE) & (lane < MLA_QK)
    deferred = []
    for r in range(ROW_TILE // IN_SUB):
        rows = slice(r * IN_SUB, (r + 1) * IN_SUB)
        h = _rms(x_ref[0, rows, :], g, D_MODEL).astype(BF16)
        proj = _dot(h, w_ref[...])
        o = MLA_Q_RANK + MLA_KV_RANK
        cq = proj[:, 0:MLA_Q_RANK]
        ckv = proj[:, MLA_Q_RANK:o]
        k_rope = proj[:, o:o + LANES]
        k_rope_partner = proj[:, o + LANES:o + 2 * LANES]
        o += 2 * LANES
        sq_ref[0, rows, :] = (proj[:, o:o + SB_WIDTH] * (SB_DIM ** -0.5 * LOG2E)).astype(BF16)
        sk_ref[0, rows, :] = proj[:, o + SB_WIDTH:o + 2 * SB_WIDTH].astype(BF16)

        qn = _rms(cq, qan_ref[...], MLA_Q_RANK).astype(BF16)
        q_all = _dot(qn, wq_ref[...])
        kvn = _rms(ckv, kvan_ref[...], MLA_KV_RANK).astype(BF16)
        k_all = _dot(kvn, wk_ref[...])
        deferred.append((h, kvn))

        pos = pos_ref[0, rows, :].astype(F32)
        group = IN_SUB // TRIG_PACK
        packed = pos[0:group] * invf_ref[0:1, :]
        for gi in range(1, TRIG_PACK):
            packed = packed + pos[gi * group:(gi + 1) * group] * invf_ref[gi:gi + 1, :]

        def unpack(t, fill):
            parts = [t] + [pltpu.roll(t, LANES - MLA_ROPE * gi, 1) for gi in range(1, TRIG_PACK)]
            return jnp.where(rope_lane, jnp.concatenate(parts, axis=0), fill)

        cos = unpack(jnp.cos(packed), 1.0)
        sin = unpack(jnp.sin(packed), 0.0)
        gq_cos = gq_ref[...] * cos
        gq_sin = gqp_ref[...] * sin
        gk_cos = gk_ref[...] * cos
        k_partner_term = k_rope_partner * (gkp_ref[...] * sin)

        def inv_rms(t):
            ss = jnp.sum(t * t, axis=-1, keepdims=True)
            return lax.rsqrt(ss * (1.0 / MLA_QK) + EPS)

        for hd in range(MLA_HEADS):
            sl = slice(hd * LANES, (hd + 1) * LANES)
            t = q_all[:, sl]
            t_partner = q_all[:, MLA_HEADS * LANES + hd * LANES:MLA_HEADS * LANES + (hd + 1) * LANES]
            qm_ref[0, hd, rows, :] = (
                (t * gq_cos + t_partner * gq_sin) * inv_rms(t)).astype(BF16)
            tk = k_all[:, sl] + k_rope
            km_ref[0, hd, rows, :] = (
                (tk * gk_cos + k_partner_term) * inv_rms(tk)).astype(BF16)

    h_all = jnp.concatenate([h for h, _ in deferred], axis=0)
    kvn_all = jnp.concatenate([kvn for _, kvn in deferred], axis=0)
    vts_ref[0, 0] = _dot_nt(wsvt_ref[...], h_all).astype(BF16)
    vtm_ref[0, 0] = _dot_nt(wvt_ref[...], kvn_all).astype(BF16)


def _in0_call(x, pos3, g, w, wsvt, qan, wq, kvan, wk, wvt, gq, gqp, gk, gkp, invf, to_cast):
    b, s, _ = x.shape
    tm = ROW_TILE
    nt = s // tm
    steps = b * nt

    def cast_spec(a):
        n_blocks = steps
        while a.shape[0] % n_blocks or (a.shape[0] // n_blocks) % BF16_SUBLANES:
            n_blocks //= 2
        share = steps // n_blocks
        return pl.BlockSpec((a.shape[0] // n_blocks, a.shape[1]),
                            lambda bi, i: ((bi * nt + i) // share, 0))

    cast_specs = [cast_spec(a) for a in to_cast]
    head_spec = pl.BlockSpec((1, MLA_HEADS, tm, LANES), lambda bi, i: (bi, 0, i, 0))
    vt_spec = pl.BlockSpec((1, 1, SB_WIDTH, tm), lambda bi, i: (bi, i, 0, 0))
    row_spec = pl.BlockSpec((1, tm, SB_WIDTH), lambda bi, i: (bi, i, 0))
    lane_vec = _full((1, LANES))
    return pl.pallas_call(
        functools.partial(_in0_kernel, n_cast=len(to_cast)),
        grid=(b, nt),
        in_specs=[
            pl.BlockSpec((1, tm, D_MODEL), lambda bi, i: (bi, i, 0)),
            pl.BlockSpec((1, tm, 1), lambda bi, i: (bi, i, 0)),
            _full((1, D_MODEL)),
            _full(w.shape),
            _full(wsvt.shape),
            _full((1, MLA_Q_RANK)),
            _full(wq.shape),
            _full((1, MLA_KV_RANK)),
            _full(wk.shape),
            _full(wvt.shape),
            lane_vec, lane_vec, lane_vec, lane_vec, _full((TRIG_PACK, LANES)),
        ] + cast_specs,
        out_specs=[head_spec, head_spec, vt_spec, row_spec, row_spec, vt_spec] + cast_specs,
        out_shape=[
            jax.ShapeDtypeStruct((b, MLA_HEADS, s, LANES), BF16),
            jax.ShapeDtypeStruct((b, MLA_HEADS, s, LANES), BF16),
            jax.ShapeDtypeStruct((b, nt, MLA_HEADS * MLA_V, tm), BF16),
            jax.ShapeDtypeStruct((b, s, SB_WIDTH), BF16),
            jax.ShapeDtypeStruct((b, s, SB_WIDTH), BF16),
            jax.ShapeDtypeStruct((b, nt, SB_WIDTH, tm), BF16),
        ] + [jax.ShapeDtypeStruct(a.shape, BF16) for a in to_cast],
        compiler_params=_compiler_params(2),
        name="l0_in",
    )(x, pos3, g, w, wsvt, qan, wq, kvan, wk, wvt, gq, gqp, gk, gkp, invf, *to_cast)


def _diag_spans(n_chains, chain, tk):
    for c in range(n_chains):
        end = (c + 1) * chain
        for d in range(-(-end // tk)):
            nk = min(tk, end - d * tk)
            yield c, d, nk, d * tk + nk > c * chain


def _softmax_attn_kernel(q_ref, k_ref, vt_ref, o_ref, m_scr, acc_scr, pre_scr, *, head_dim):
    tq, tk, cw = Q_TILE, K_TILE, CHAIN
    n_c = tq // cw
    kt_per_q = tq // tk
    qi = pl.program_id(2)
    chains = [(hh, c) for hh in range(SOFTMAX_HEADS) for c in range(n_c)]
    qs = [q_ref[0, hh, c * cw:(c + 1) * cw, :] for hh, c in chains]

    m_scr[...] = jnp.full(m_scr.shape, NEG_INF, F32)
    acc_scr[...] = jnp.zeros(acc_scr.shape, F32)

    def with_ones(vt):
        return jnp.concatenate([vt, jnp.ones((ONES_ROWS, vt.shape[1]), vt.dtype)], axis=0)

    def update(idx, s, vt, visible):
        if visible is not None:
            s = jnp.where(visible, s, NEG_INF)
        m = m_scr[idx]
        m_new = jnp.maximum(m, jnp.max(s, axis=0, keepdims=True))
        p = jnp.exp2(s - m_new)
        m_scr[idx] = m_new
        acc_scr[idx] = jnp.exp2(m - m_new) * acc_scr[idx] + _dot(vt, p.astype(BF16))

    def scores_of(idx, tile, nk=tk):
        hh = chains[idx][0]
        return _dot_nt(k_ref[0, hh, pl.ds(pl.multiple_of(tile * tk, tk), nk), :], qs[idx])

    def run_pipeline(items, tail_fns, ahead):
        fns = [score_fn for score_fn, _ in items] + list(tail_fns)
        pending = {w: fns[w]() for w in range(min(ahead, len(fns)))}
        for v, (_, consume) in enumerate(items):
            if v + ahead < len(fns):
                pending[v + ahead] = fns[v + ahead]()
            consume(pending.pop(v))

    def prefetch_fns(tile):
        def store(a):
            def fn():
                pre_scr[a] = scores_of(a, tile)
            return fn
        return [store(a) for a in range(PREFETCH)]

    n_full = qi * kt_per_q
    items = []
    for hh in range(SOFTMAX_HEADS):
        for c, d, nk, mask in _diag_spans(n_c, cw, tk):
            idx = hh * n_c + c
            visible = None
            if mask:
                row = lax.broadcasted_iota(jnp.int32, (nk, cw), 0) + d * tk
                col = lax.broadcasted_iota(jnp.int32, (nk, cw), 1) + c * cw
                visible = row <= col

            def score_fn(idx=idx, d=d, nk=nk):
                return scores_of(idx, n_full + d, nk)

            def consume(s, idx=idx, hh=hh, d=d, nk=nk, visible=visible):
                vt = vt_ref[0, n_full + d, hh * head_dim:(hh + 1) * head_dim, 0:nk]
                update(idx, s, with_ones(vt), visible)

            items.append((score_fn, consume))
    run_pipeline(items, prefetch_fns(0), DIAG_LOOKAHEAD)

    def full_step(i, _):
        items = []
        for u in range(kt_per_q):
            j = i * kt_per_q + u
            for idx, (hh, _) in enumerate(chains):
                if u == 0 and idx < PREFETCH:
                    score_fn = lambda idx=idx: pre_scr[idx]
                else:
                    score_fn = lambda idx=idx, j=j: scores_of(idx, j)

                def consume(s, idx=idx, hh=hh, j=j):
                    vt = vt_ref[0, j, hh * head_dim:(hh + 1) * head_dim, :]
                    update(idx, s, with_ones(vt), None)

                items.append((score_fn, consume))
        run_pipeline(items, prefetch_fns(jnp.minimum(i + 1, qi - 1) * kt_per_q), LOOKAHEAD)
        return 0

    lax.fori_loop(0, qi, full_step, 0)

    def normalized(idx):
        acc = acc_scr[idx]
        return acc[:head_dim] / acc[head_dim:head_dim + 1]

    outs = [jnp.concatenate([normalized(hh * n_c + c) for c in range(n_c)], axis=1)
            for hh in range(SOFTMAX_HEADS)]
    o_ref[0] = jnp.concatenate(outs, axis=0).astype(o_ref.dtype)


def _softmax_attn_call(q, k, vt, name):
    b, h, s, _ = q.shape
    tq = Q_TILE
    nkt = s // K_TILE
    head_dim = vt.shape[2] // h
    hps = SOFTMAX_HEADS
    n_chains = hps * tq // CHAIN
    return pl.pallas_call(
        functools.partial(_softmax_attn_kernel, head_dim=head_dim),
        grid=(b, h // hps, s // tq),
        in_specs=[
            pl.BlockSpec((1, hps, tq, LANES), lambda bi, p, i: (bi, p, i, 0)),
            pl.BlockSpec((1, hps, s, LANES), lambda bi, p, i: (bi, p, 0, 0)),
            pl.BlockSpec((1, nkt, hps * head_dim, K_TILE), lambda bi, p, i: (bi, 0, p, 0)),
        ],
        out_specs=pl.BlockSpec((1, hps * head_dim, tq), lambda bi, p, i: (bi, p, i)),
        out_shape=jax.ShapeDtypeStruct((b, h * head_dim, s), BF16),
        scratch_shapes=[
            pltpu.VMEM((n_chains, 1, CHAIN), F32),
            pltpu.VMEM((n_chains, head_dim + ONES_ROWS, CHAIN), F32),
            pltpu.VMEM((PREFETCH, K_TILE, CHAIN), F32),
        ],
        compiler_params=_compiler_params(3),
        name=name,
    )(q, k, vt)


def _sb_attn_kernel(q_ref, k_ref, vt_ref, o_ref, run_scr, acc_scr):
    tq, tk, cw, sub = Q_TILE, K_TILE, CHAIN, SB_SUB
    n_c = tq // cw
    kt_per_q = tq // tk
    qi = pl.program_id(2)
    s_row = lax.broadcasted_iota(jnp.int32, (sub, sub), 0)
    s_col = lax.broadcasted_iota(jnp.int32, (sub, sub), 1)
    neg_later = -((s_col > s_row).astype(BF16))
    lane = _lane_iota(cw)
    heads = SB_STEP_HEADS
    chains = [(hh, c) for hh in range(heads) for c in range(n_c)]

    def pair_lanes(hh):
        return slice((hh // 2) * LANES, (hh // 2 + 1) * LANES)

    qs = []
    for hh, c in chains:
        q_pair = q_ref[0, c * cw:(c + 1) * cw, pair_lanes(hh)]
        in_head = (lane >= (hh % 2) * SB_DIM) & (lane < (hh % 2 + 1) * SB_DIM)
        qs.append(jnp.where(in_head, q_pair, jnp.zeros_like(q_pair)))

    run_scr[...] = jnp.zeros(run_scr.shape, F32)
    acc_scr[...] = jnp.zeros(acc_scr.shape, F32)

    def suffix_sums(z, visible):
        cost = jnp.maximum(z, 0.0) + jnp.log(1.0 + jnp.exp2(-jnp.abs(z))) * LOG2E
        log_beta = z - cost
        if visible is not None:
            cost = jnp.where(visible, cost, 0.0)
        cost16 = cost.astype(BF16)
        blocks = [slice(b * sub, (b + 1) * sub) for b in range(z.shape[0] // sub)]
        suffix = [_dot(neg_later, cost16[bl]) for bl in blocks]
        return log_beta, cost, blocks, suffix

    def accumulate(idx, stage, vt, visible):
        log_beta, cost, blocks, suffix = stage
        run = run_scr[idx]
        terms = []
        for bl, sfx in reversed(list(zip(blocks, suffix))):
            terms.append(log_beta[bl] + sfx + run)
            run = run + sfx[0:1, :] - cost[bl.start:bl.start + 1, :]
        a = jnp.exp2(jnp.concatenate(terms[::-1], axis=0))
        if visible is not None:
            a = jnp.where(visible, a, 0.0)
        run_scr[idx] = run
        acc_scr[idx] = acc_scr[idx] + _dot(vt, a.astype(BF16))

    def tile_update(idxs, fns, vts, masks):
        n = len(idxs)
        pending, stages = {}, {}
        for w in range(min(SB_LOOKAHEAD, len(fns))):
            pending[w] = fns[w]()
        for v in range(n + 1):
            if v + SB_LOOKAHEAD < len(fns):
                pending[v + SB_LOOKAHEAD] = fns[v + SB_LOOKAHEAD]()
            if v < n:
                stages[v] = suffix_sums(pending.pop(v), masks[v])
            if v >= 1:
                accumulate(idxs[v - 1], stages.pop(v - 1), vts[v - 1], masks[v - 1])

    def scores_fn(idx, start, nk):
        return lambda: _dot_nt(
            k_ref[0, pl.ds(pl.multiple_of(start, cw), nk), pair_lanes(chains[idx][0])], qs[idx])

    q0 = qi * tq
    tri_row = lax.broadcasted_iota(jnp.int32, (cw, cw), 0)
    tri_col = lax.broadcasted_iota(jnp.int32, (cw, cw), 1)
    idxs, fns, vts, masks = [], [], [], []
    covered_from = {}
    for phase in range(2):
        for hh in range(heads):
            for c in range(n_c):
                idx = hh * n_c + c
                rel = (c - phase) * cw
                start = jnp.maximum(q0 + rel, 0)
                if phase == 0:
                    visible = tri_row < tri_col
                elif rel < 0:
                    visible = jnp.broadcast_to(qi > 0, (cw, cw))
                else:
                    visible = None
                idxs.append(idx)
                fns.append(scores_fn(idx, start, cw))
                vts.append(vt_ref[0, start // tk, hh * SB_DIM:(hh + 1) * SB_DIM,
                                  rel % tk:rel % tk + cw])
                masks.append(visible)
                covered_from[idx] = q0 + rel
    tile_update(idxs, fns, vts, masks)

    all_idx = list(range(len(chains)))
    key_row = lax.broadcasted_iota(jnp.int32, (tk, cw), 0)

    def alive_at(j):
        flags = [jnp.logical_and(covered_from[idx] > j * tk, jnp.max(run_scr[idx]) > RUN_FLOOR)
                 for idx in all_idx]
        return functools.reduce(jnp.logical_or, flags)

    def step_and_test(carry):
        j, _ = carry
        tile_update(all_idx, [scores_fn(idx, j * tk, tk) for idx in all_idx],
                    [vt_ref[0, j, chains[idx][0] * SB_DIM:(chains[idx][0] + 1) * SB_DIM, :]
                     for idx in all_idx],
                    [key_row + j * tk < covered_from[idx] for idx in all_idx])
        return j - 1, alive_at(j - 1)

    top = qi * kt_per_q + ((n_c - 2) * cw - 1) // tk
    lax.while_loop(lambda carry: jnp.logical_and(carry[0] >= 0, carry[1]),
                   step_and_test, (top, alive_at(top)))
    outs = [jnp.concatenate([acc_scr[hh * n_c + c] for c in range(n_c)], axis=1)
            for hh in range(heads)]
    o_ref[0] = jnp.concatenate(outs, axis=0).astype(o_ref.dtype)


def _sb_attn_call(q, k, vt):
    b, s, w = q.shape
    tq = Q_TILE
    nkt = s // K_TILE
    width = SB_STEP_HEADS * SB_DIM
    n_chains = SB_STEP_HEADS * tq // CHAIN
    return pl.pallas_call(
        _sb_attn_kernel,
        grid=(b, w // width, s // tq),
        in_specs=[
            pl.BlockSpec((1, tq, width), lambda bi, p, i: (bi, i, p)),
            pl.BlockSpec((1, s, width), lambda bi, p, i: (bi, 0, p)),
            pl.BlockSpec((1, nkt, width, K_TILE), lambda bi, p, i: (bi, 0, p, 0)),
        ],
        out_specs=pl.BlockSpec((1, width, tq), lambda bi, p, i: (bi, p, i)),
        out_shape=jax.ShapeDtypeStruct((b, w, s), BF16),
        scratch_shapes=[
            pltpu.VMEM((n_chains, 1, CHAIN), F32),
            pltpu.VMEM((n_chains, SB_DIM, CHAIN), F32),
        ],
        compiler_params=_compiler_params(3),
        name="l0_sb_attn",
    )(q, k, vt)


def _out_ffn_kernel(*refs, n_mix):
    x_ref = refs[0]
    o_refs = refs[1:1 + n_mix]
    wo_refs = refs[1 + n_mix:1 + 2 * n_mix]
    g_ref, wg_ref, wu_ref, wd_ref, y_ref = refs[1 + 2 * n_mix:]
    y = x_ref[...]
    for o_ref, wo_ref in zip(o_refs, wo_refs):
        y = y + lax.dot_general(o_ref[0], wo_ref[...], (((0,), (0,)), ((), ())),
                                preferred_element_type=F32)
    h = _rms(y, g_ref[...], D_MODEL).astype(BF16)
    gate = _dot(h, wg_ref[...])
    up = _dot(h, wu_ref[...])
    act = gate * (1.0 / (1.0 + jnp.exp(-gate))) * up
    y_ref[...] = y + _dot(act.astype(BF16), wd_ref[...])


def _out_ffn_call(x2, mixes, wos, g, wg, wu, wd, name):
    n, d = x2.shape
    tm = FFN_TILE
    n_mix = len(mixes)
    nt = mixes[0].shape[2] // tm
    row = lambda width: pl.BlockSpec((tm, width), lambda i: (i, 0))
    mix_spec = lambda m: pl.BlockSpec((1, m.shape[1], tm), lambda i: (i // nt, 0, i % nt))
    return pl.pallas_call(
        functools.partial(_out_ffn_kernel, n_mix=n_mix),
        grid=(n // tm,),
        in_specs=([row(d)] + [mix_spec(m) for m in mixes]
                  + [_full(w.shape) for w in wos]
                  + [_full((1, d)), _full(wg.shape), _full(wu.shape), _full(wd.shape)]),
        out_specs=row(d),
        out_shape=jax.ShapeDtypeStruct((n, d), F32),
        compiler_params=_compiler_params(1),
        name=name,
    )(x2, *mixes, *wos, g, wg, wu, wd)


def _in1_kernel(x_ref, g_ref, w_ref, wvt_ref, fb_ref, gq_ref, gk_ref, layout_ref,
                qa_ref, ka_ref, vt_ref, carry_ref):
    i = pl.program_id(1)

    @pl.when(i == 0)
    def _():
        carry_ref[...] = jnp.zeros_like(carry_ref)

    n_terms = AUG_TERMS
    used = n_terms * FOX_HEADS
    rows = IN_SUB
    lane = _lane_iota(1)
    half_lane = lane & (FOX_DIM - 1)
    low_half = lane < FOX_DIM
    term = layout_ref[...]
    decay_lane = term < n_terms
    ones_lane = term == n_terms
    own_rows = [jnp.where((half_lane >= n_terms * hd) & (half_lane < n_terms * (hd + 1)), 1.0, 0.0)
                for hd in range(FOX_HEADS)]
    r_i = lax.broadcasted_iota(jnp.int32, (rows, rows), 0)
    c_i = lax.broadcasted_iota(jnp.int32, (rows, rows), 1)
    lower = (c_i <= r_i).astype(BF16)
    g = g_ref[...]
    gq = gq_ref[...]
    gk = gk_ref[...]

    deferred = []
    for r in range(ROW_TILE // IN_SUB):
        sl_rows = slice(r * rows, (r + 1) * rows)
        h = _rms(x_ref[0, sl_rows, :], g, D_MODEL).astype(BF16)
        proj = _dot(h, w_ref[...])
        deferred.append(h)
        q = proj[:, 0:FOX_WIDTH]
        k = proj[:, FOX_WIDTH:2 * FOX_WIDTH]
        f_logit = proj[:, 2 * FOX_WIDTH:]

        log_f = jnp.where(decay_lane, _log_sigmoid(f_logit + fb_ref[...]), 0.0)
        f1, f2, f3 = _split3(log_f)
        decay = carry_ref[...] + _dot(lower, f1) + _dot(lower, f2) + _dot(lower, f3)
        carry_ref[...] = decay[rows - 1:rows, :]
        d1, d2, d3 = (d.astype(F32) for d in _split3(decay * LOG2E))
        terms = jnp.where(term == 0, d1, jnp.where(term == 1, d2, d3))
        aug_q = jnp.where(decay_lane, terms, jnp.where(ones_lane, 1.0, 0.0))
        neg_terms = -terms

        for pair in range(FOX_HEADS // 2):
            sl = slice(pair * LANES, (pair + 1) * LANES)
            qp = q[:, sl]
            kp = k[:, sl]
            qsq = qp * qp
            ksq = kp * kp
            q_gain = qp * gq
            k_gain = kp * gk
            for parity in range(2):
                hd = 2 * pair + parity
                in_head = low_half if parity == 0 else jnp.logical_not(low_half)

                def inv_rms(sq):
                    ss = jnp.sum(jnp.where(in_head, sq, 0.0), axis=-1, keepdims=True)
                    return lax.rsqrt(ss * (1.0 / FOX_DIM) + EPS)

                qa_ref[0, hd, sl_rows, :] = jnp.where(
                    in_head, q_gain * inv_rms(qsq), aug_q).astype(BF16)
                moved = pltpu.roll(neg_terms, used - n_terms * hd, 1)
                aug_k = jnp.where(ones_lane, moved, own_rows[hd])
                ka_ref[0, hd, sl_rows, :] = jnp.where(
                    in_head, k_gain * inv_rms(ksq), aug_k).astype(BF16)

    h_all = jnp.concatenate(deferred, axis=0)
    vt_ref[0, 0] = _dot_nt(wvt_ref[...], h_all).astype(BF16)


def _in1_call(x, g, w, wvt, fb, gq, gk, layout):
    b, s, _ = x.shape
    tm = ROW_TILE
    nt = s // tm
    head_spec = pl.BlockSpec((1, FOX_HEADS, tm, LANES), lambda bi, i: (bi, 0, i, 0))
    lane_vec = _full((1, LANES))
    return pl.pallas_call(
        _in1_kernel,
        grid=(b, nt),
        in_specs=[
            pl.BlockSpec((1, tm, D_MODEL), lambda bi, i: (bi, i, 0)),
            _full((1, D_MODEL)),
            _full(w.shape),
            _full(wvt.shape),
            lane_vec, lane_vec, lane_vec, lane_vec,
        ],
        out_specs=[head_spec, head_spec,
                   pl.BlockSpec((1, 1, FOX_WIDTH, tm), lambda bi, i: (bi, i, 0, 0))],
        out_shape=[
            jax.ShapeDtypeStruct((b, FOX_HEADS, s, LANES), BF16),
            jax.ShapeDtypeStruct((b, FOX_HEADS, s, LANES), BF16),
            jax.ShapeDtypeStruct((b, nt, FOX_WIDTH, tm), BF16),
        ],
        scratch_shapes=[pltpu.VMEM((1, LANES), F32)],
        compiler_params=_compiler_params(2),
        name="l1_in",
    )(x, g, w, wvt, fb, gq, gk, layout)


def _pad_lanes(v, width=LANES):
    return jnp.pad(v, (0, width - v.shape[0])).reshape(1, width)


def _rope_partner(cols):
    half = MLA_ROPE // 2
    return jnp.concatenate([-cols[..., half:], cols[..., :half]], axis=-1)


def _prep_layer0(w_in, w_q_up, w_kv_up, q_norm, k_norm):
    o = MLA_Q_RANK + MLA_KV_RANK
    pad_rope = lambda c: jnp.pad(c, ((0, 0),) * (c.ndim - 1) + ((MLA_NOPE, LANES - MLA_QK),))
    w_rope = w_in[:, o:o + MLA_ROPE]
    sb = w_in[:, o + MLA_ROPE:]
    w = jnp.concatenate([w_in[:, :o], pad_rope(w_rope), pad_rope(_rope_partner(w_rope)),
                         sb[:, :2 * SB_WIDTH]], axis=1)
    wsvt = sb[:, 2 * SB_WIDTH:].T
    wq3 = w_q_up.reshape(MLA_Q_RANK, MLA_HEADS, MLA_QK)
    wq = jnp.pad(wq3, ((0, 0), (0, 0), (0, LANES - MLA_QK)))
    wq_partner = pad_rope(_rope_partner(wq3[:, :, MLA_NOPE:]))
    wq = jnp.concatenate([wq.reshape(MLA_Q_RANK, -1), wq_partner.reshape(MLA_Q_RANK, -1)], axis=1)
    kv = w_kv_up.reshape(MLA_KV_RANK, MLA_HEADS, MLA_NOPE + MLA_V)
    wk = jnp.pad(kv[:, :, :MLA_NOPE], ((0, 0), (0, 0), (0, LANES - MLA_NOPE)))
    wk = wk.reshape(MLA_KV_RANK, MLA_HEADS * LANES)
    wvt = kv[:, :, MLA_NOPE:].reshape(MLA_KV_RANK, MLA_HEADS * MLA_V).T
    gq = q_norm * (MLA_QK ** -0.5 * LOG2E)
    half = MLA_ROPE // 2
    swap_halves = lambda r: jnp.concatenate([r[half:], r[:half]])
    gqp = pad_rope(swap_halves(gq[MLA_NOPE:]))
    gkp = pad_rope(swap_halves(k_norm[MLA_NOPE:]))
    inv_freq = ROPE_THETA ** (-jnp.arange(half, dtype=F32) / half)
    invf = pad_rope(jnp.concatenate([inv_freq, inv_freq]))
    invf = jnp.stack([jnp.roll(invf, MLA_ROPE * gi) for gi in range(TRIG_PACK)])
    lanes = lambda v: v.reshape(1, LANES)
    return (w.astype(BF16), wsvt.astype(BF16), wq.astype(BF16), wk.astype(BF16), wvt.astype(BF16),
            _pad_lanes(gq), lanes(gqp), _pad_lanes(k_norm), lanes(gkp), invf)


def _prep_layer1(w_in, f_bias, q_norm, k_norm):
    wf = w_in[:, 3 * FOX_WIDTH:]
    rep = jnp.repeat(wf, AUG_TERMS, axis=1)
    pad = jnp.zeros((wf.shape[0], FOX_DIM - AUG_TERMS * FOX_HEADS), wf.dtype)
    f_group = jnp.concatenate([rep, pad, rep, pad], axis=1)
    w = jnp.concatenate([w_in[:, :2 * FOX_WIDTH], f_group], axis=1)
    wvt = w_in[:, 2 * FOX_WIDTH:3 * FOX_WIDTH].T
    fb = jnp.repeat(f_bias, AUG_TERMS)
    fpad = jnp.zeros((FOX_DIM - AUG_TERMS * FOX_HEADS,), f_bias.dtype)
    fb = jnp.concatenate([fb, fpad, fb, fpad]).reshape(1, LANES)
    gain2 = lambda gvec: jnp.concatenate([gvec, gvec]).reshape(1, LANES)
    used = AUG_TERMS * FOX_HEADS
    half_lane = jnp.arange(LANES, dtype=jnp.int32) % FOX_DIM
    layout = jnp.where(half_lane < used, half_lane % AUG_TERMS,
                       jnp.where(half_lane < used + AUG_TERMS, AUG_TERMS, AUG_TERMS + 1))
    return (w.astype(BF16), wvt.astype(BF16), fb,
            gain2(q_norm * (FOX_DIM ** -0.5 * LOG2E)), gain2(k_norm), layout.reshape(1, LANES))


def kernel(x, positions, l0_attn_norm, l0_w_in, l0_mla_q_a_norm, l0_mla_w_q_up, l0_mla_kv_a_norm, l0_mla_w_kv_up, l0_mla_q_norm, l0_mla_k_norm, l0_w_o, l0_ffn_norm, l0_w_gate, l0_w_up, l0_w_down, l1_attn_norm, l1_w_in, l1_fox_f_bias, l1_fox_q_norm, l1_fox_k_norm, l1_w_o, l1_ffn_norm, l1_w_gate, l1_w_up, l1_w_down):
    b, s, d = x.shape
    assert d == D_MODEL and s % Q_TILE == 0 and K_TILE == ROW_TILE and Q_TILE % K_TILE == 0
    assert K_TILE % CHAIN == 0 and CHAIN % SB_SUB == 0
    row = lambda v: v.reshape(1, -1)

    w0, wsvt, wq, wk, wvt0, gq0, gqp0, gk0, gkp0, invf = _prep_layer0(
        l0_w_in, l0_mla_w_q_up, l0_mla_w_kv_up, l0_mla_q_norm, l0_mla_k_norm)
    later_f32 = [l0_w_gate, l0_w_up, l0_w_down, l1_w_gate, l1_w_up, l1_w_down,
                 l0_w_o, l1_w_in, l1_w_o]
    qm, km, vtm, sq, sk, vts, *later_bf16 = _in0_call(
        x, positions.reshape(b, s, 1), row(l0_attn_norm), w0, wsvt, row(l0_mla_q_a_norm), wq,
        row(l0_mla_kv_a_norm), wk, wvt0, gq0, gqp0, gk0, gkp0, invf, later_f32)
    wg0, wu0, wd0, wg1, wu1, wd1, wo0, w_in1, wo1 = later_bf16
    o_mla = _softmax_attn_call(qm, km, vtm, "l0_mla_attn")
    o_sb = _sb_attn_call(sq, sk, vts)
    n_mla = MLA_HEADS * MLA_V
    x1 = _out_ffn_call(
        x.reshape(b * s, d),
        [o_mla, o_sb],
        [wo0[:n_mla], wo0[n_mla:]],
        row(l0_ffn_norm), wg0, wu0, wd0, "l0_out_ffn")

    w1, wvt1, fb, gq1, gk1, layout = _prep_layer1(
        w_in1, l1_fox_f_bias, l1_fox_q_norm, l1_fox_k_norm)
    qa, ka, vtf = _in1_call(
        x1.reshape(b, s, d), row(l1_attn_norm), w1, wvt1, fb, gq1, gk1, layout)
    o_fox = _softmax_attn_call(qa, ka, vtf, "l1_fox_attn")
    out = _out_ffn_call(
        x1, [o_fox], [wo1],
        row(l1_ffn_norm), wg1, wu1, wd1, "l1_out_ffn")
    return out.reshape(b, s, d)
```

```python
import functools

import jax
import jax.numpy as jnp
from jax import lax
from jax.experimental import pallas as pl
from jax.experimental.pallas import tpu as pltpu

F32 = jnp.float32
BF16 = jnp.bfloat16

D_MODEL = 1024
EPS = 1e-6
NEG_INF = -1e30
ROPE_THETA = 10000.0

MLA_HEADS = 8
MLA_Q_RANK = 256
MLA_KV_RANK = 128
MLA_NOPE = 64
MLA_ROPE = 32
MLA_QK = MLA_NOPE + MLA_ROPE
MLA_V = 64
SB_HEADS = 8
SB_DIM = 64
SB_WIDTH = SB_HEADS * SB_DIM
FOX_HEADS = 16
FOX_DIM = 64
FOX_WIDTH = FOX_HEADS * FOX_DIM

LANES = 128
BF16_SUBLANES = 16
V7X_VMEM_BYTES = 64 * 1024 * 1024
VMEM_LIMIT = V7X_VMEM_BYTES * 7 // 8

ROW_TILE = 1024
K_TILE = 512
Q_TILE = 1024
SB_STEP_HEADS = 8
SOFTMAX_HEADS = 8
CHAIN = 256
FFN_TILE = 512
PREFETCH = 2
LOOKAHEAD = 3
DIAG_LOOKAHEAD = 7
SB_LOOKAHEAD = 3
ONES_ROWS = 16
SB_SUB = 128
LOG2E = 1.4426950408889634
RUN_FLOOR = -160.0

AUG_TERMS = 3
TRIG_PACK = LANES // MLA_ROPE
IN_SUB = 256


def _compiler_params(n_axes):
    return pltpu.CompilerParams(
        dimension_semantics=("arbitrary",) * n_axes,
        vmem_limit_bytes=VMEM_LIMIT,
    )


def _full(shape):
    return pl.BlockSpec(shape, lambda *_: (0,) * len(shape), pipeline_mode=pl.Buffered(1))


def _lane_iota(rows):
    return lax.broadcasted_iota(jnp.int32, (rows, LANES), 1)


def _rms(x, gain, width):
    ss = jnp.sum(x * x, axis=-1, keepdims=True)
    return x * lax.rsqrt(ss * (1.0 / width) + EPS) * gain


def _log_sigmoid(z):
    return jnp.minimum(z, 0.0) - jnp.log(1.0 + jnp.exp(-jnp.abs(z)))


def _split3(x):
    a = x.astype(BF16)
    r = x - a.astype(F32)
    b = r.astype(BF16)
    c = (r - b.astype(F32)).astype(BF16)
    return a, b, c


def _dot(a, b):
    return jnp.dot(a, b, preferred_element_type=F32)


def _dot_nt(a, b):
    return lax.dot_general(a, b, (((1,), (1,)), ((), ())), preferred_element_type=F32)


def _in0_kernel(*refs, n_cast):
    n_in = 15
    (x_ref, pos_ref, g_ref, w_ref, wsvt_ref, qan_ref, wq_ref, kvan_ref, wk_ref,
     wvt_ref, gq_ref, gqp_ref, gk_ref, gkp_ref, invf_ref) = refs[:n_in]
    cast_in = refs[n_in:n_in + n_cast]
    qm_ref, km_ref, vtm_ref, sq_ref, sk_ref, vts_ref = refs[n_in + n_cast:n_in + n_cast + 6]
    cast_out = refs[n_in + n_cast + 6:]
    for src, dst in zip(cast_in, cast_out):
        dst[...] = src[...].astype(dst.dtype)
    g = g_ref[...]
    lane = _lane_iota(1)
    rope_lane = (lane >= MLA_NOPE) & (lane < MLA_QK)
    deferred = []
    for r in range(ROW_TILE // IN_SUB):
        rows = slice(r * IN_SUB, (r + 1) * IN_SUB)
        h = _rms(x_ref[0, rows, :], g, D_MODEL).astype(BF16)
        proj = _dot(h, w_ref[...])
        o = MLA_Q_RANK + MLA_KV_RANK
        cq = proj[:, 0:MLA_Q_RANK]
        ckv = proj[:, MLA_Q_RANK:o]
        k_rope = proj[:, o:o + LANES]
        k_rope_partner = proj[:, o + LANES:o + 2 * LANES]
        o += 2 * LANES
        sq_ref[0, rows, :] = (proj[:, o:o + SB_WIDTH] * (SB_DIM ** -0.5 * LOG2E)).astype(BF16)
        sk_ref[0, rows, :] = proj[:, o + SB_WIDTH:o + 2 * SB_WIDTH].astype(BF16)

        qn = _rms(cq, qan_ref[...], MLA_Q_RANK).astype(BF16)
        q_all = _dot(qn, wq_ref[...])
        kvn = _rms(ckv, kvan_ref[...], MLA_KV_RANK).astype(BF16)
        k_all = _dot(kvn, wk_ref[...])
        deferred.append((h, kvn))

        pos = pos_ref[0, rows, :].astype(F32)
        group = IN_SUB // TRIG_PACK
        packed = pos[0:group] * invf_ref[0:1, :]
        for gi in range(1, TRIG_PACK):
            packed = packed + pos[gi * group:(gi + 1) * group] * invf_ref[gi:gi + 1, :]

        def unpack(t, fill):
            parts = [t] + [pltpu.roll(t, LANES - MLA_ROPE * gi, 1) for gi in range(1, TRIG_PACK)]
            return jnp.where(rope_lane, jnp.concatenate(parts, axis=0), fill)

        cos = unpack(jnp.cos(packed), 1.0)
        sin = unpack(jnp.sin(packed), 0.0)
        gq_cos = gq_ref[...] * cos
        gq_sin = gqp_ref[...] * sin
        gk_cos = gk_ref[...] * cos
        k_partner_term = k_rope_partner * (gkp_ref[...] * sin)

        def inv_rms(t):
            ss = jnp.sum(t * t, axis=-1, keepdims=True)
            return lax.rsqrt(ss * (1.0 / MLA_QK) + EPS)

        for hd in range(MLA_HEADS):
            sl = slice(hd * LANES, (hd + 1) * LANES)
            t = q_all[:, sl]
            t_partner = q_all[:, MLA_HEADS * LANES + hd * LANES:MLA_HEADS * LANES + (hd + 1) * LANES]
            qm_ref[0, hd, rows, :] = (
                (t * gq_cos + t_partner * gq_sin) * inv_rms(t)).astype(BF16)
            tk = k_all[:, sl] + k_rope
            km_ref[0, hd, rows, :] = (
                (tk * gk_cos + k_partner_term) * inv_rms(tk)).astype(BF16)

    h_all = jnp.concatenate([h for h, _ in deferred], axis=0)
    kvn_all = jnp.concatenate([kvn for _, kvn in deferred], axis=0)
    vts = _dot_nt(wsvt_ref[...], h_all).astype(BF16)
    vtm = _dot_nt(wvt_ref[...], kvn_all).astype(BF16)
    for j in range(ROW_TILE // K_TILE):
        vts_ref[0, j] = vts[:, j * K_TILE:(j + 1) * K_TILE]
        vtm_ref[0, j] = vtm[:, j * K_TILE:(j + 1) * K_TILE]


def _in0_call(x, pos3, g, w, wsvt, qan, wq, kvan, wk, wvt, gq, gqp, gk, gkp, invf, to_cast):
    b, s, _ = x.shape
    tm = ROW_TILE
    nt = s // tm
    steps = b * nt

    def cast_spec(a):
        n_blocks = steps
        while a.shape[0] % n_blocks or (a.shape[0] // n_blocks) % BF16_SUBLANES:
            n_blocks //= 2
        share = steps // n_blocks
        return pl.BlockSpec((a.shape[0] // n_blocks, a.shape[1]),
                            lambda bi, i: ((bi * nt + i) // share, 0))

    cast_specs = [cast_spec(a) for a in to_cast]
    head_spec = pl.BlockSpec((1, MLA_HEADS, tm, LANES), lambda bi, i: (bi, 0, i, 0))
    vt_spec = pl.BlockSpec((1, tm // K_TILE, SB_WIDTH, K_TILE), lambda bi, i: (bi, i, 0, 0))
    row_spec = pl.BlockSpec((1, tm, SB_WIDTH), lambda bi, i: (bi, i, 0))
    lane_vec = _full((1, LANES))
    return pl.pallas_call(
        functools.partial(_in0_kernel, n_cast=len(to_cast)),
        grid=(b, nt),
        in_specs=[
            pl.BlockSpec((1, tm, D_MODEL), lambda bi, i: (bi, i, 0)),
            pl.BlockSpec((1, tm, 1), lambda bi, i: (bi, i, 0)),
            _full((1, D_MODEL)),
            _full(w.shape),
            _full(wsvt.shape),
            _full((1, MLA_Q_RANK)),
            _full(wq.shape),
            _full((1, MLA_KV_RANK)),
            _full(wk.shape),
            _full(wvt.shape),
            lane_vec, lane_vec, lane_vec, lane_vec, _full((TRIG_PACK, LANES)),
        ] + cast_specs,
        out_specs=[head_spec, head_spec, vt_spec, row_spec, row_spec, vt_spec] + cast_specs,
        out_shape=[
            jax.ShapeDtypeStruct((b, MLA_HEADS, s, LANES), BF16),
            jax.ShapeDtypeStruct((b, MLA_HEADS, s, LANES), BF16),
            jax.ShapeDtypeStruct((b, s // K_TILE, MLA_HEADS * MLA_V, K_TILE), BF16),
            jax.ShapeDtypeStruct((b, s, SB_WIDTH), BF16),
            jax.ShapeDtypeStruct((b, s, SB_WIDTH), BF16),
            jax.ShapeDtypeStruct((b, s // K_TILE, SB_WIDTH, K_TILE), BF16),
        ] + [jax.ShapeDtypeStruct(a.shape, BF16) for a in to_cast],
        compiler_params=_compiler_params(2),
        name="l0_in",
    )(x, pos3, g, w, wsvt, qan, wq, kvan, wk, wvt, gq, gqp, gk, gkp, invf, *to_cast)


def _diag_spans(n_chains, chain, tk):
    for c in range(n_chains):
        end = (c + 1) * chain
        for d in range(-(-end // tk)):
            nk = min(tk, end - d * tk)
            yield c, d, nk, d * tk + nk > c * chain


def _softmax_attn_kernel(q_ref, k_ref, vt_ref, o_ref, m_scr, acc_scr, pre_scr, *, head_dim):
    tq, tk, cw = Q_TILE, K_TILE, CHAIN
    n_c = tq // cw
    kt_per_q = tq // tk
    qi = pl.program_id(2)
    chains = [(hh, c) for hh in range(SOFTMAX_HEADS) for c in range(n_c)]
    qs = [q_ref[0, hh, c * cw:(c + 1) * cw, :] for hh, c in chains]

    m_scr[...] = jnp.full(m_scr.shape, NEG_INF, F32)
    acc_scr[...] = jnp.zeros(acc_scr.shape, F32)

    def with_ones(vt):
        return jnp.concatenate([vt, jnp.ones((ONES_ROWS, vt.shape[1]), vt.dtype)], axis=0)

    def update(idx, s, vt, visible):
        if visible is not None:
            s = jnp.where(visible, s, NEG_INF)
        m = m_scr[idx]
        m_new = jnp.maximum(m, jnp.max(s, axis=0, keepdims=True))
        p = jnp.exp2(s - m_new)
        m_scr[idx] = m_new
        acc_scr[idx] = jnp.exp2(m - m_new) * acc_scr[idx] + _dot(vt, p.astype(BF16))

    def scores_of(idx, tile, nk=tk):
        hh = chains[idx][0]
        return _dot_nt(k_ref[0, hh, pl.ds(pl.multiple_of(tile * tk, tk), nk), :], qs[idx])

    def run_pipeline(items, tail_fns, ahead):
        fns = [score_fn for score_fn, _ in items] + list(tail_fns)
        pending = {w: fns[w]() for w in range(min(ahead, len(fns)))}
        for v, (_, consume) in enumerate(items):
            if v + ahead < len(fns):
                pending[v + ahead] = fns[v + ahead]()
            consume(pending.pop(v))

    def prefetch_fns(tile):
        def store(a):
            def fn():
                pre_scr[a] = scores_of(a, tile)
            return fn
        return [store(a) for a in range(PREFETCH)]

    n_full = qi * kt_per_q
    items = []
    for hh in range(SOFTMAX_HEADS):
        for c, d, nk, mask in _diag_spans(n_c, cw, tk):
            idx = hh * n_c + c
            visible = None
            if mask:
                row = lax.broadcasted_iota(jnp.int32, (nk, cw), 0) + d * tk
                col = lax.broadcasted_iota(jnp.int32, (nk, cw), 1) + c * cw
                visible = row <= col

            def score_fn(idx=idx, d=d, nk=nk):
                return scores_of(idx, n_full + d, nk)

            def consume(s, idx=idx, hh=hh, d=d, nk=nk, visible=visible):
                vt = vt_ref[0, n_full + d, hh * head_dim:(hh + 1) * head_dim, 0:nk]
                update(idx, s, with_ones(vt), visible)

            items.append((score_fn, consume))
    run_pipeline(items, prefetch_fns(0), DIAG_LOOKAHEAD)

    def full_step(i, _):
        items = []
        for u in range(kt_per_q):
            j = i * kt_per_q + u
            for idx, (hh, _) in enumerate(chains):
                if u == 0 and idx < PREFETCH:
                    score_fn = lambda idx=idx: pre_scr[idx]
                else:
                    score_fn = lambda idx=idx, j=j: scores_of(idx, j)

                def consume(s, idx=idx, hh=hh, j=j):
                    vt = vt_ref[0, j, hh * head_dim:(hh + 1) * head_dim, :]
                    update(idx, s, with_ones(vt), None)

                items.append((score_fn, consume))
        run_pipeline(items, prefetch_fns(jnp.minimum(i + 1, qi - 1) * kt_per_q), LOOKAHEAD)
        return 0

    lax.fori_loop(0, qi, full_step, 0)

    def normalized(idx):
        acc = acc_scr[idx]
        return acc[:head_dim] / acc[head_dim:head_dim + 1]

    outs = [jnp.concatenate([normalized(hh * n_c + c) for c in range(n_c)], axis=1)
            for hh in range(SOFTMAX_HEADS)]
    o_ref[0] = jnp.concatenate(outs, axis=0).astype(o_ref.dtype)


def _softmax_attn_call(q, k, vt, name):
    b, h, s, _ = q.shape
    tq = Q_TILE
    nkt = s // K_TILE
    head_dim = vt.shape[2] // h
    hps = SOFTMAX_HEADS
    n_chains = hps * tq // CHAIN
    return pl.pallas_call(
        functools.partial(_softmax_attn_kernel, head_dim=head_dim),
        grid=(b, h // hps, s // tq),
        in_specs=[
            pl.BlockSpec((1, hps, tq, LANES), lambda bi, p, i: (bi, p, i, 0)),
            pl.BlockSpec((1, hps, s, LANES), lambda bi, p, i: (bi, p, 0, 0)),
            pl.BlockSpec((1, nkt, hps * head_dim, K_TILE), lambda bi, p, i: (bi, 0, p, 0)),
        ],
        out_specs=pl.BlockSpec((1, hps * head_dim, tq), lambda bi, p, i: (bi, p, i)),
        out_shape=jax.ShapeDtypeStruct((b, h * head_dim, s), BF16),
        scratch_shapes=[
            pltpu.VMEM((n_chains, 1, CHAIN), F32),
            pltpu.VMEM((n_chains, head_dim + ONES_ROWS, CHAIN), F32),
            pltpu.VMEM((PREFETCH, K_TILE, CHAIN), F32),
        ],
        compiler_params=_compiler_params(3),
        name=name,
    )(q, k, vt)


def _sb_attn_kernel(q_ref, k_ref, vt_ref, o_ref, run_scr, acc_scr):
    tq, tk, cw, sub = Q_TILE, K_TILE, CHAIN, SB_SUB
    n_c = tq // cw
    kt_per_q = tq // tk
    qi = pl.program_id(2)
    s_row = lax.broadcasted_iota(jnp.int32, (sub, sub), 0)
    s_col = lax.broadcasted_iota(jnp.int32, (sub, sub), 1)
    neg_later = -((s_col > s_row).astype(BF16))
    lane = _lane_iota(cw)
    heads = SB_STEP_HEADS
    chains = [(hh, c) for hh in range(heads) for c in range(n_c)]

    def pair_lanes(hh):
        return slice((hh // 2) * LANES, (hh // 2 + 1) * LANES)

    qs = []
    for hh, c in chains:
        q_pair = q_ref[0, c * cw:(c + 1) * cw, pair_lanes(hh)]
        in_head = (lane >= (hh % 2) * SB_DIM) & (lane < (hh % 2 + 1) * SB_DIM)
        qs.append(jnp.where(in_head, q_pair, jnp.zeros_like(q_pair)))

    run_scr[...] = jnp.zeros(run_scr.shape, F32)
    acc_scr[...] = jnp.zeros(acc_scr.shape, F32)

    def suffix_sums(z, visible):
        cost = jnp.maximum(z, 0.0) + jnp.log(1.0 + jnp.exp2(-jnp.abs(z))) * LOG2E
        log_beta = z - cost
        if visible is not None:
            cost = jnp.where(visible, cost, 0.0)
        cost16 = cost.astype(BF16)
        blocks = [slice(b * sub, (b + 1) * sub) for b in range(z.shape[0] // sub)]
        suffix = [_dot(neg_later, cost16[bl]) for bl in blocks]
        return log_beta, cost, blocks, suffix

    def accumulate(idx, stage, vt, visible):
        log_beta, cost, blocks, suffix = stage
        run = run_scr[idx]
        terms = []
        for bl, sfx in reversed(list(zip(blocks, suffix))):
            terms.append(log_beta[bl] + sfx + run)
            run = run + sfx[0:1, :] - cost[bl.start:bl.start + 1, :]
        a = jnp.exp2(jnp.concatenate(terms[::-1], axis=0))
        if visible is not None:
            a = jnp.where(visible, a, 0.0)
        run_scr[idx] = run
        acc_scr[idx] = acc_scr[idx] + _dot(vt, a.astype(BF16))

    def tile_update(idxs, fns, vts, masks):
        n = len(idxs)
        pending, stages = {}, {}
        for w in range(min(SB_LOOKAHEAD, len(fns))):
            pending[w] = fns[w]()
        for v in range(n + 1):
            if v + SB_LOOKAHEAD < len(fns):
                pending[v + SB_LOOKAHEAD] = fns[v + SB_LOOKAHEAD]()
            if v < n:
                stages[v] = suffix_sums(pending.pop(v), masks[v])
            if v >= 1:
                accumulate(idxs[v - 1], stages.pop(v - 1), vts[v - 1], masks[v - 1])

    def scores_fn(idx, start, nk):
        return lambda: _dot_nt(
            k_ref[0, pl.ds(pl.multiple_of(start, cw), nk), pair_lanes(chains[idx][0])], qs[idx])

    q0 = qi * tq
    tri_row = lax.broadcasted_iota(jnp.int32, (cw, cw), 0)
    tri_col = lax.broadcasted_iota(jnp.int32, (cw, cw), 1)
    idxs, fns, vts, masks = [], [], [], []
    covered_from = {}
    for phase in range(2):
        for hh in range(heads):
            for c in range(n_c):
                idx = hh * n_c + c
                rel = (c - phase) * cw
                start = jnp.maximum(q0 + rel, 0)
                if phase == 0:
                    visible = tri_row < tri_col
                elif rel < 0:
                    visible = jnp.broadcast_to(qi > 0, (cw, cw))
                else:
                    visible = None
                idxs.append(idx)
                fns.append(scores_fn(idx, start, cw))
                vts.append(vt_ref[0, start // tk, hh * SB_DIM:(hh + 1) * SB_DIM,
                                  rel % tk:rel % tk + cw])
                masks.append(visible)
                covered_from[idx] = q0 + rel
    tile_update(idxs, fns, vts, masks)

    all_idx = list(range(len(chains)))
    key_row = lax.broadcasted_iota(jnp.int32, (tk, cw), 0)

    def alive_at(j):
        flags = [jnp.logical_and(covered_from[idx] > j * tk, jnp.max(run_scr[idx]) > RUN_FLOOR)
                 for idx in all_idx]
        return functools.reduce(jnp.logical_or, flags)

    def step_and_test(carry):
        j, _ = carry
        tile_update(all_idx, [scores_fn(idx, j * tk, tk) for idx in all_idx],
                    [vt_ref[0, j, chains[idx][0] * SB_DIM:(chains[idx][0] + 1) * SB_DIM, :]
                     for idx in all_idx],
                    [key_row + j * tk < covered_from[idx] for idx in all_idx])
        return j - 1, alive_at(j - 1)

    top = qi * kt_per_q + ((n_c - 2) * cw - 1) // tk
    lax.while_loop(lambda carry: jnp.logical_and(carry[0] >= 0, carry[1]),
                   step_and_test, (top, alive_at(top)))
    outs = [jnp.concatenate([acc_scr[hh * n_c + c] for c in range(n_c)], axis=1)
            for hh in range(heads)]
    o_ref[0] = jnp.concatenate(outs, axis=0).astype(o_ref.dtype)


def _sb_attn_call(q, k, vt):
    b, s, w = q.shape
    tq = Q_TILE
    nkt = s // K_TILE
    width = SB_STEP_HEADS * SB_DIM
    n_chains = SB_STEP_HEADS * tq // CHAIN
    return pl.pallas_call(
        _sb_attn_kernel,
        grid=(b, w // width, s // tq),
        in_specs=[
            pl.BlockSpec((1, tq, width), lambda bi, p, i: (bi, i, p)),
            pl.BlockSpec((1, s, width), lambda bi, p, i: (bi, 0, p)),
            pl.BlockSpec((1, nkt, width, K_TILE), lambda bi, p, i: (bi, 0, p, 0)),
        ],
        out_specs=pl.BlockSpec((1, width, tq), lambda bi, p, i: (bi, p, i)),
        out_shape=jax.ShapeDtypeStruct((b, w, s), BF16),
        scratch_shapes=[
            pltpu.VMEM((n_chains, 1, CHAIN), F32),
            pltpu.VMEM((n_chains, SB_DIM, CHAIN), F32),
        ],
        compiler_params=_compiler_params(3),
        name="l0_sb_attn",
    )(q, k, vt)


def _out_ffn_kernel(*refs, n_mix):
    x_ref = refs[0]
    o_refs = refs[1:1 + n_mix]
    wo_refs = refs[1 + n_mix:1 + 2 * n_mix]
    g_ref, wg_ref, wu_ref, wd_ref, y_ref = refs[1 + 2 * n_mix:]
    y = x_ref[...]
    for o_ref, wo_ref in zip(o_refs, wo_refs):
        y = y + lax.dot_general(o_ref[0], wo_ref[...], (((0,), (0,)), ((), ())),
                                preferred_element_type=F32)
    h = _rms(y, g_ref[...], D_MODEL).astype(BF16)
    gate = _dot(h, wg_ref[...])
    up = _dot(h, wu_ref[...])
    act = gate * (1.0 / (1.0 + jnp.exp(-gate))) * up
    y_ref[...] = y + _dot(act.astype(BF16), wd_ref[...])


def _out_ffn_call(x2, mixes, wos, g, wg, wu, wd, name):
    n, d = x2.shape
    tm = FFN_TILE
    n_mix = len(mixes)
    nt = mixes[0].shape[2] // tm
    row = lambda width: pl.BlockSpec((tm, width), lambda i: (i, 0))
    mix_spec = lambda m: pl.BlockSpec((1, m.shape[1], tm), lambda i: (i // nt, 0, i % nt))
    return pl.pallas_call(
        functools.partial(_out_ffn_kernel, n_mix=n_mix),
        grid=(n // tm,),
        in_specs=([row(d)] + [mix_spec(m) for m in mixes]
                  + [_full(w.shape) for w in wos]
                  + [_full((1, d)), _full(wg.shape), _full(wu.shape), _full(wd.shape)]),
        out_specs=row(d),
        out_shape=jax.ShapeDtypeStruct((n, d), F32),
        compiler_params=_compiler_params(1),
        name=name,
    )(x2, *mixes, *wos, g, wg, wu, wd)


def _in1_kernel(x_ref, g_ref, w_ref, wvt_ref, fb_ref, gq_ref, gk_ref, layout_ref,
                qa_ref, ka_ref, vt_ref, carry_ref):
    i = pl.program_id(1)

    @pl.when(i == 0)
    def _():
        carry_ref[...] = jnp.zeros_like(carry_ref)

    n_terms = AUG_TERMS
    used = n_terms * FOX_HEADS
    rows = IN_SUB
    lane = _lane_iota(1)
    half_lane = lane & (FOX_DIM - 1)
    low_half = lane < FOX_DIM
    term = layout_ref[...]
    decay_lane = term < n_terms
    ones_lane = term == n_terms
    own_rows = [jnp.where((half_lane >= n_terms * hd) & (half_lane < n_terms * (hd + 1)), 1.0, 0.0)
                for hd in range(FOX_HEADS)]
    r_i = lax.broadcasted_iota(jnp.int32, (rows, rows), 0)
    c_i = lax.broadcasted_iota(jnp.int32, (rows, rows), 1)
    lower = (c_i <= r_i).astype(BF16)
    g = g_ref[...]
    gq = gq_ref[...]
    gk = gk_ref[...]

    deferred = []
    for r in range(ROW_TILE // IN_SUB):
        sl_rows = slice(r * rows, (r + 1) * rows)
        h = _rms(x_ref[0, sl_rows, :], g, D_MODEL).astype(BF16)
        proj = _dot(h, w_ref[...])
        deferred.append(h)
        q = proj[:, 0:FOX_WIDTH]
        k = proj[:, FOX_WIDTH:2 * FOX_WIDTH]
        f_logit = proj[:, 2 * FOX_WIDTH:]

        log_f = jnp.where(decay_lane, _log_sigmoid(f_logit + fb_ref[...]), 0.0)
        f1, f2, f3 = _split3(log_f)
        decay = carry_ref[...] + _dot(lower, f1) + _dot(lower, f2) + _dot(lower, f3)
        carry_ref[...] = decay[rows - 1:rows, :]
        d1, d2, d3 = (d.astype(F32) for d in _split3(decay * LOG2E))
        terms = jnp.where(term == 0, d1, jnp.where(term == 1, d2, d3))
        aug_q = jnp.where(decay_lane, terms, jnp.where(ones_lane, 1.0, 0.0))
        neg_terms = -terms

        for pair in range(FOX_HEADS // 2):
            sl = slice(pair * LANES, (pair + 1) * LANES)
            qp = q[:, sl]
            kp = k[:, sl]
            qsq = qp * qp
            ksq = kp * kp
            q_gain = qp * gq
            k_gain = kp * gk
            for parity in range(2):
                hd = 2 * pair + parity
                in_head = low_half if parity == 0 else jnp.logical_not(low_half)

                def inv_rms(sq):
                    ss = jnp.sum(jnp.where(in_head, sq, 0.0), axis=-1, keepdims=True)
                    return lax.rsqrt(ss * (1.0 / FOX_DIM) + EPS)

                qa_ref[0, hd, sl_rows, :] = jnp.where(
                    in_head, q_gain * inv_rms(qsq), aug_q).astype(BF16)
                moved = pltpu.roll(neg_terms, used - n_terms * hd, 1)
                aug_k = jnp.where(ones_lane, moved, own_rows[hd])
                ka_ref[0, hd, sl_rows, :] = jnp.where(
                    in_head, k_gain * inv_rms(ksq), aug_k).astype(BF16)

    h_all = jnp.concatenate(deferred, axis=0)
    vt = _dot_nt(wvt_ref[...], h_all).astype(BF16)
    for j in range(ROW_TILE // K_TILE):
        vt_ref[0, j] = vt[:, j * K_TILE:(j + 1) * K_TILE]


def _in1_call(x, g, w, wvt, fb, gq, gk, layout):
    b, s, _ = x.shape
    tm = ROW_TILE
    nt = s // tm
    head_spec = pl.BlockSpec((1, FOX_HEADS, tm, LANES), lambda bi, i: (bi, 0, i, 0))
    lane_vec = _full((1, LANES))
    return pl.pallas_call(
        _in1_kernel,
        grid=(b, nt),
        in_specs=[
            pl.BlockSpec((1, tm, D_MODEL), lambda bi, i: (bi, i, 0)),
            _full((1, D_MODEL)),
            _full(w.shape),
            _full(wvt.shape),
            lane_vec, lane_vec, lane_vec, lane_vec,
        ],
        out_specs=[head_spec, head_spec,
                   pl.BlockSpec((1, tm // K_TILE, FOX_WIDTH, K_TILE),
                                lambda bi, i: (bi, i, 0, 0))],
        out_shape=[
            jax.ShapeDtypeStruct((b, FOX_HEADS, s, LANES), BF16),
            jax.ShapeDtypeStruct((b, FOX_HEADS, s, LANES), BF16),
            jax.ShapeDtypeStruct((b, s // K_TILE, FOX_WIDTH, K_TILE), BF16),
        ],
        scratch_shapes=[pltpu.VMEM((1, LANES), F32)],
        compiler_params=_compiler_params(2),
        name="l1_in",
    )(x, g, w, wvt, fb, gq, gk, layout)


def _pad_lanes(v, width=LANES):
    return jnp.pad(v, (0, width - v.shape[0])).reshape(1, width)


def _rope_partner(cols):
    half = MLA_ROPE // 2
    return jnp.concatenate([-cols[..., half:], cols[..., :half]], axis=-1)


def _prep_layer0(w_in, w_q_up, w_kv_up, q_norm, k_norm):
    o = MLA_Q_RANK + MLA_KV_RANK
    pad_rope = lambda c: jnp.pad(c, ((0, 0),) * (c.ndim - 1) + ((MLA_NOPE, LANES - MLA_QK),))
    w_rope = w_in[:, o:o + MLA_ROPE]
    sb = w_in[:, o + MLA_ROPE:]
    w = jnp.concatenate([w_in[:, :o], pad_rope(w_rope), pad_rope(_rope_partner(w_rope)),
                         sb[:, :2 * SB_WIDTH]], axis=1)
    wsvt = sb[:, 2 * SB_WIDTH:].T
    wq3 = w_q_up.reshape(MLA_Q_RANK, MLA_HEADS, MLA_QK)
    wq = jnp.pad(wq3, ((0, 0), (0, 0), (0, LANES - MLA_QK)))
    wq_partner = pad_rope(_rope_partner(wq3[:, :, MLA_NOPE:]))
    wq = jnp.concatenate([wq.reshape(MLA_Q_RANK, -1), wq_partner.reshape(MLA_Q_RANK, -1)], axis=1)
    kv = w_kv_up.reshape(MLA_KV_RANK, MLA_HEADS, MLA_NOPE + MLA_V)
    wk = jnp.pad(kv[:, :, :MLA_NOPE], ((0, 0), (0, 0), (0, LANES - MLA_NOPE)))
    wk = wk.reshape(MLA_KV_RANK, MLA_HEADS * LANES)
    wvt = kv[:, :, MLA_NOPE:].reshape(MLA_KV_RANK, MLA_HEADS * MLA_V).T
    gq = q_norm * (MLA_QK ** -0.5 * LOG2E)
    half = MLA_ROPE // 2
    swap_halves = lambda r: jnp.concatenate([r[half:], r[:half]])
    gqp = pad_rope(swap_halves(gq[MLA_NOPE:]))
    gkp = pad_rope(swap_halves(k_norm[MLA_NOPE:]))
    inv_freq = ROPE_THETA ** (-jnp.arange(half, dtype=F32) / half)
    invf = pad_rope(jnp.concatenate([inv_freq, inv_freq]))
    invf = jnp.stack([jnp.roll(invf, MLA_ROPE * gi) for gi in range(TRIG_PACK)])
    lanes = lambda v: v.reshape(1, LANES)
    return (w.astype(BF16), wsvt.astype(BF16), wq.astype(BF16), wk.astype(BF16), wvt.astype(BF16),
            _pad_lanes(gq), lanes(gqp), _pad_lanes(k_norm), lanes(gkp), invf)


def _prep_layer1(w_in, f_bias, q_norm, k_norm):
    wf = w_in[:, 3 * FOX_WIDTH:]
    rep = jnp.repeat(wf, AUG_TERMS, axis=1)
    pad = jnp.zeros((wf.shape[0], FOX_DIM - AUG_TERMS * FOX_HEADS), wf.dtype)
    f_group = jnp.concatenate([rep, pad, rep, pad], axis=1)
    w = jnp.concatenate([w_in[:, :2 * FOX_WIDTH], f_group], axis=1)
    wvt = w_in[:, 2 * FOX_WIDTH:3 * FOX_WIDTH].T
    fb = jnp.repeat(f_bias, AUG_TERMS)
    fpad = jnp.zeros((FOX_DIM - AUG_TERMS * FOX_HEADS,), f_bias.dtype)
    fb = jnp.concatenate([fb, fpad, fb, fpad]).reshape(1, LANES)
    gain2 = lambda gvec: jnp.concatenate([gvec, gvec]).reshape(1, LANES)
    used = AUG_TERMS * FOX_HEADS
    half_lane = jnp.arange(LANES, dtype=jnp.int32) % FOX_DIM
    layout = jnp.where(half_lane < used, half_lane % AUG_TERMS,
                       jnp.where(half_lane < used + AUG_TERMS, AUG_TERMS, AUG_TERMS + 1))
    return (w.astype(BF16), wvt.astype(BF16), fb,
            gain2(q_norm * (FOX_DIM ** -0.5 * LOG2E)), gain2(k_norm), layout.reshape(1, LANES))


def kernel(x, positions, l0_attn_norm, l0_w_in, l0_mla_q_a_norm, l0_mla_w_q_up, l0_mla_kv_a_norm, l0_mla_w_kv_up, l0_mla_q_norm, l0_mla_k_norm, l0_w_o, l0_ffn_norm, l0_w_gate, l0_w_up, l0_w_down, l1_attn_norm, l1_w_in, l1_fox_f_bias, l1_fox_q_norm, l1_fox_k_norm, l1_w_o, l1_ffn_norm, l1_w_gate, l1_w_up, l1_w_down):
    b, s, d = x.shape
    assert d == D_MODEL and s % Q_TILE == 0 and ROW_TILE % K_TILE == 0 and Q_TILE % K_TILE == 0
    assert K_TILE % CHAIN == 0 and CHAIN % SB_SUB == 0
    row = lambda v: v.reshape(1, -1)

    w0, wsvt, wq, wk, wvt0, gq0, gqp0, gk0, gkp0, invf = _prep_layer0(
        l0_w_in, l0_mla_w_q_up, l0_mla_w_kv_up, l0_mla_q_norm, l0_mla_k_norm)
    ffn_f32 = [l0_w_gate, l0_w_up, l0_w_down, l1_w_gate, l1_w_up, l1_w_down]
    qm, km, vtm, sq, sk, vts, *ffn_bf16 = _in0_call(
        x, positions.reshape(b, s, 1), row(l0_attn_norm), w0, wsvt, row(l0_mla_q_a_norm), wq,
        row(l0_mla_kv_a_norm), wk, wvt0, gq0, gqp0, gk0, gkp0, invf, ffn_f32)
    wg0, wu0, wd0, wg1, wu1, wd1 = ffn_bf16
    o_mla = _softmax_attn_call(qm, km, vtm, "l0_mla_attn")
    o_sb = _sb_attn_call(sq, sk, vts)
    n_mla = MLA_HEADS * MLA_V
    x1 = _out_ffn_call(
        x.reshape(b * s, d),
        [o_mla, o_sb],
        [l0_w_o[:n_mla].astype(BF16), l0_w_o[n_mla:].astype(BF16)],
        row(l0_ffn_norm), wg0, wu0, wd0, "l0_out_ffn")

    w1, wvt1, fb, gq1, gk1, layout = _prep_layer1(
        l1_w_in, l1_fox_f_bias, l1_fox_q_norm, l1_fox_k_norm)
    qa, ka, vtf = _in1_call(
        x1.reshape(b, s, d), row(l1_attn_norm), w1, wvt1, fb, gq1, gk1, layout)
    o_fox = _softmax_attn_call(qa, ka, vtf, "l1_fox_attn")
    out = _out_ffn_call(
        x1, [o_fox], [l1_w_o.astype(BF16)],
        row(l1_ffn_norm), wg1, wu1, wd1, "l1_out_ffn")
    return out.reshape(b, s, d)
```

```python
import functools

import jax
import jax.numpy as jnp
from jax import lax
from jax.experimental import pallas as pl
from jax.experimental.pallas import tpu as pltpu

F32 = jnp.float32
BF16 = jnp.bfloat16

D_MODEL = 1024
EPS = 1e-6
NEG_INF = -1e30
ROPE_THETA = 10000.0

MLA_HEADS = 8
MLA_Q_RANK = 256
MLA_KV_RANK = 128
MLA_NOPE = 64
MLA_ROPE = 32
MLA_QK = MLA_NOPE + MLA_ROPE
MLA_V = 64
SB_HEADS = 8
SB_DIM = 64
SB_WIDTH = SB_HEADS * SB_DIM
FOX_HEADS = 16
FOX_DIM = 64
FOX_WIDTH = FOX_HEADS * FOX_DIM

LANES = 128
BF16_SUBLANES = 16
V7X_VMEM_BYTES = 64 * 1024 * 1024
VMEM_LIMIT = V7X_VMEM_BYTES * 7 // 8

ROW_TILE = 512
K_TILE = 512
Q_TILE = 1024
SB_STEP_HEADS = 8
SOFTMAX_HEADS = 8
CHAIN = 256
FFN_TILE = 512
PREFETCH = 2
LOOKAHEAD = 3
DIAG_LOOKAHEAD = 7
SB_LOOKAHEAD = 3
ONES_ROWS = 16
SB_SUB = 128
LOG2E = 1.4426950408889634
RUN_FLOOR = -160.0

AUG_TERMS = 3
TRIG_PACK = LANES // MLA_ROPE
IN_SUB = 256


def _compiler_params(n_axes):
    return pltpu.CompilerParams(
        dimension_semantics=("arbitrary",) * n_axes,
        vmem_limit_bytes=VMEM_LIMIT,
    )


def _full(shape):
    return pl.BlockSpec(shape, lambda *_: (0,) * len(shape), pipeline_mode=pl.Buffered(1))


def _lane_iota(rows):
    return lax.broadcasted_iota(jnp.int32, (rows, LANES), 1)


def _rms(x, gain, width):
    ss = jnp.sum(x * x, axis=-1, keepdims=True)
    return x * lax.rsqrt(ss * (1.0 / width) + EPS) * gain


def _log_sigmoid(z):
    return jnp.minimum(z, 0.0) - jnp.log(1.0 + jnp.exp(-jnp.abs(z)))


def _split3(x):
    a = x.astype(BF16)
    r = x - a.astype(F32)
    b = r.astype(BF16)
    c = (r - b.astype(F32)).astype(BF16)
    return a, b, c


def _dot(a, b):
    return jnp.dot(a, b, preferred_element_type=F32)


def _dot_nt(a, b):
    return lax.dot_general(a, b, (((1,), (1,)), ((), ())), preferred_element_type=F32)


def _in0_kernel(*refs, n_cast):
    n_in = 15
    (x_ref, pos_ref, g_ref, w_ref, wsvt_ref, qan_ref, wq_ref, kvan_ref, wk_ref,
     wvt_ref, gq_ref, gqp_ref, gk_ref, gkp_ref, invf_ref) = refs[:n_in]
    cast_in = refs[n_in:n_in + n_cast]
    qm_ref, km_ref, vtm_ref, sq_ref, sk_ref, vts_ref = refs[n_in + n_cast:n_in + n_cast + 6]
    cast_out = refs[n_in + n_cast + 6:]
    for src, dst in zip(cast_in, cast_out):
        dst[...] = src[...].astype(dst.dtype)
    g = g_ref[...]
    lane = _lane_iota(1)
    rope_lane = (lane >= MLA_NOPE) & (lane < MLA_QK)
    deferred = []
    for r in range(ROW_TILE // IN_SUB):
        rows = slice(r * IN_SUB, (r + 1) * IN_SUB)
        h = _rms(x_ref[0, rows, :], g, D_MODEL).astype(BF16)
        proj = _dot(h, w_ref[...])
        o = MLA_Q_RANK + MLA_KV_RANK
        cq = proj[:, 0:MLA_Q_RANK]
        ckv = proj[:, MLA_Q_RANK:o]
        k_rope = proj[:, o:o + LANES]
        k_rope_partner = proj[:, o + LANES:o + 2 * LANES]
        o += 2 * LANES
        sq_ref[0, rows, :] = (proj[:, o:o + SB_WIDTH] * (SB_DIM ** -0.5 * LOG2E)).astype(BF16)
        sk_ref[0, rows, :] = proj[:, o + SB_WIDTH:o + 2 * SB_WIDTH].astype(BF16)

        qn = _rms(cq, qan_ref[...], MLA_Q_RANK).astype(BF16)
        q_all = _dot(qn, wq_ref[...])
        kvn = _rms(ckv, kvan_ref[...], MLA_KV_RANK).astype(BF16)
        k_all = _dot(kvn, wk_ref[...])
        deferred.append((h, kvn))

        pos = pos_ref[0, rows, :].astype(F32)
        group = IN_SUB // TRIG_PACK
        packed = pos[0:group] * invf_ref[0:1, :]
        for gi in range(1, TRIG_PACK):
            packed = packed + pos[gi * group:(gi + 1) * group] * invf_ref[gi:gi + 1, :]

        def unpack(t, fill):
            parts = [t] + [pltpu.roll(t, LANES - MLA_ROPE * gi, 1) for gi in range(1, TRIG_PACK)]
            return jnp.where(rope_lane, jnp.concatenate(parts, axis=0), fill)

        cos = unpack(jnp.cos(packed), 1.0)
        sin = unpack(jnp.sin(packed), 0.0)
        gq_cos = gq_ref[...] * cos
        gq_sin = gqp_ref[...] * sin
        gk_cos = gk_ref[...] * cos
        k_partner_term = k_rope_partner * (gkp_ref[...] * sin)

        def inv_rms(t):
            ss = jnp.sum(t * t, axis=-1, keepdims=True)
            return lax.rsqrt(ss * (1.0 / MLA_QK) + EPS)

        for hd in range(MLA_HEADS):
            sl = slice(hd * LANES, (hd + 1) * LANES)
            t = q_all[:, sl]
            t_partner = q_all[:, MLA_HEADS * LANES + hd * LANES:MLA_HEADS * LANES + (hd + 1) * LANES]
            qm_ref[0, hd, rows, :] = (
                (t * gq_cos + t_partner * gq_sin) * inv_rms(t)).astype(BF16)
            tk = k_all[:, sl] + k_rope
            km_ref[0, hd, rows, :] = (
                (tk * gk_cos + k_partner_term) * inv_rms(tk)).astype(BF16)

    h_all = jnp.concatenate([h for h, _ in deferred], axis=0)
    kvn_all = jnp.concatenate([kvn for _, kvn in deferred], axis=0)
    vts_ref[0, 0] = _dot_nt(wsvt_ref[...], h_all).astype(BF16)
    vtm_ref[0, 0] = _dot_nt(wvt_ref[...], kvn_all).astype(BF16)


def _in0_call(x, pos3, g, w, wsvt, qan, wq, kvan, wk, wvt, gq, gqp, gk, gkp, invf, to_cast):
    b, s, _ = x.shape
    tm = ROW_TILE
    nt = s // tm
    steps = b * nt

    def cast_spec(a):
        n_blocks = steps
        while a.shape[0] % n_blocks or (a.shape[0] // n_blocks) % BF16_SUBLANES:
            n_blocks //= 2
        share = steps // n_blocks
        return pl.BlockSpec((a.shape[0] // n_blocks, a.shape[1]),
                            lambda bi, i: ((bi * nt + i) // share, 0))

    cast_specs = [cast_spec(a) for a in to_cast]
    head_spec = pl.BlockSpec((1, MLA_HEADS, tm, LANES), lambda bi, i: (bi, 0, i, 0))
    vt_spec = pl.BlockSpec((1, 1, SB_WIDTH, tm), lambda bi, i: (bi, i, 0, 0))
    row_spec = pl.BlockSpec((1, tm, SB_WIDTH), lambda bi, i: (bi, i, 0))
    lane_vec = _full((1, LANES))
    return pl.pallas_call(
        functools.partial(_in0_kernel, n_cast=len(to_cast)),
        grid=(b, nt),
        in_specs=[
            pl.BlockSpec((1, tm, D_MODEL), lambda bi, i: (bi, i, 0)),
            pl.BlockSpec((1, tm, 1), lambda bi, i: (bi, i, 0)),
            _full((1, D_MODEL)),
            _full(w.shape),
            _full(wsvt.shape),
            _full((1, MLA_Q_RANK)),
            _full(wq.shape),
            _full((1, MLA_KV_RANK)),
            _full(wk.shape),
            _full(wvt.shape),
            lane_vec, lane_vec, lane_vec, lane_vec, _full((TRIG_PACK, LANES)),
        ] + cast_specs,
        out_specs=[head_spec, head_spec, vt_spec, row_spec, row_spec, vt_spec] + cast_specs,
        out_shape=[
            jax.ShapeDtypeStruct((b, MLA_HEADS, s, LANES), BF16),
            jax.ShapeDtypeStruct((b, MLA_HEADS, s, LANES), BF16),
            jax.ShapeDtypeStruct((b, nt, MLA_HEADS * MLA_V, tm), BF16),
            jax.ShapeDtypeStruct((b, s, SB_WIDTH), BF16),
            jax.ShapeDtypeStruct((b, s, SB_WIDTH), BF16),
            jax.ShapeDtypeStruct((b, nt, SB_WIDTH, tm), BF16),
        ] + [jax.ShapeDtypeStruct(a.shape, BF16) for a in to_cast],
        compiler_params=_compiler_params(2),
        name="l0_in",
    )(x, pos3, g, w, wsvt, qan, wq, kvan, wk, wvt, gq, gqp, gk, gkp, invf, *to_cast)


def _diag_spans(n_chains, chain, tk):
    for c in range(n_chains):
        end = (c + 1) * chain
        for d in range(-(-end // tk)):
            nk = min(tk, end - d * tk)
            yield c, d, nk, d * tk + nk > c * chain


def _softmax_attn_kernel(q_ref, k_ref, vt_ref, o_ref, m_scr, acc_scr, pre_scr, *, head_dim):
    tq, tk, cw = Q_TILE, K_TILE, CHAIN
    n_c = tq // cw
    kt_per_q = tq // tk
    qi = pl.program_id(2)
    chains = [(hh, c) for hh in range(SOFTMAX_HEADS) for c in range(n_c)]
    qs = [q_ref[0, hh, c * cw:(c + 1) * cw, :] for hh, c in chains]

    m_scr[...] = jnp.full(m_scr.shape, NEG_INF, F32)
    acc_scr[...] = jnp.zeros(acc_scr.shape, F32)

    def with_ones(vt):
        return jnp.concatenate([vt, jnp.ones((ONES_ROWS, vt.shape[1]), vt.dtype)], axis=0)

    def update(idx, s, vt, visible):
        if visible is not None:
            s = jnp.where(visible, s, NEG_INF)
        m = m_scr[idx]
        m_new = jnp.maximum(m, jnp.max(s, axis=0, keepdims=True))
        p = jnp.exp2(s - m_new)
        m_scr[idx] = m_new
        acc_scr[idx] = jnp.exp2(m - m_new) * acc_scr[idx] + _dot(vt, p.astype(BF16))

    def scores_of(idx, tile, nk=tk):
        hh = chains[idx][0]
        return _dot_nt(k_ref[0, hh, pl.ds(pl.multiple_of(tile * tk, tk), nk), :], qs[idx])

    def run_pipeline(items, tail_fns, ahead):
        fns = [score_fn for score_fn, _ in items] + list(tail_fns)
        pending = {w: fns[w]() for w in range(min(ahead, len(fns)))}
        for v, (_, consume) in enumerate(items):
            if v + ahead < len(fns):
                pending[v + ahead] = fns[v + ahead]()
            consume(pending.pop(v))

    def prefetch_fns(tile):
        def store(a):
            def fn():
                pre_scr[a] = scores_of(a, tile)
            return fn
        return [store(a) for a in range(PREFETCH)]

    n_full = qi * kt_per_q
    items = []
    for hh in range(SOFTMAX_HEADS):
        for c, d, nk, mask in _diag_spans(n_c, cw, tk):
            idx = hh * n_c + c
            visible = None
            if mask:
                row = lax.broadcasted_iota(jnp.int32, (nk, cw), 0) + d * tk
                col = lax.broadcasted_iota(jnp.int32, (nk, cw), 1) + c * cw
                visible = row <= col

            def score_fn(idx=idx, d=d, nk=nk):
                return scores_of(idx, n_full + d, nk)

            def consume(s, idx=idx, hh=hh, d=d, nk=nk, visible=visible):
                vt = vt_ref[0, n_full + d, hh * head_dim:(hh + 1) * head_dim, 0:nk]
                update(idx, s, with_ones(vt), visible)

            items.append((score_fn, consume))
    run_pipeline(items, prefetch_fns(0), DIAG_LOOKAHEAD)

    def full_step(i, _):
        items = []
        for u in range(kt_per_q):
            j = i * kt_per_q + u
            for idx, (hh, _) in enumerate(chains):
                if u == 0 and idx < PREFETCH:
                    score_fn = lambda idx=idx: pre_scr[idx]
                else:
                    score_fn = lambda idx=idx, j=j: scores_of(idx, j)

                def consume(s, idx=idx, hh=hh, j=j):
                    vt = vt_ref[0, j, hh * head_dim:(hh + 1) * head_dim, :]
                    update(idx, s, with_ones(vt), None)

                items.append((score_fn, consume))
        run_pipeline(items, prefetch_fns(jnp.minimum(i + 1, qi - 1) * kt_per_q), LOOKAHEAD)
        return 0

    lax.fori_loop(0, qi, full_step, 0)

    def normalized(idx):
        acc = acc_scr[idx]
        return acc[:head_dim] / acc[head_dim:head_dim + 1]

    outs = [jnp.concatenate([normalized(hh * n_c + c) for c in range(n_c)], axis=1)
            for hh in range(SOFTMAX_HEADS)]
    o_ref[0] = jnp.concatenate(outs, axis=0).astype(o_ref.dtype)


def _softmax_attn_call(q, k, vt, name):
    b, h, s, _ = q.shape
    tq = Q_TILE
    nkt = s // K_TILE
    head_dim = vt.shape[2] // h
    hps = SOFTMAX_HEADS
    n_chains = hps * tq // CHAIN
    return pl.pallas_call(
        functools.partial(_softmax_attn_kernel, head_dim=head_dim),
        grid=(b, h // hps, s // tq),
        in_specs=[
            pl.BlockSpec((1, hps, tq, LANES), lambda bi, p, i: (bi, p, i, 0)),
            pl.BlockSpec((1, hps, s, LANES), lambda bi, p, i: (bi, p, 0, 0)),
            pl.BlockSpec((1, nkt, hps * head_dim, K_TILE), lambda bi, p, i: (bi, 0, p, 0)),
        ],
        out_specs=pl.BlockSpec((1, hps * head_dim, tq), lambda bi, p, i: (bi, p, i)),
        out_shape=jax.ShapeDtypeStruct((b, h * head_dim, s), BF16),
        scratch_shapes=[
            pltpu.VMEM((n_chains, 1, CHAIN), F32),
            pltpu.VMEM((n_chains, head_dim + ONES_ROWS, CHAIN), F32),
            pltpu.VMEM((PREFETCH, K_TILE, CHAIN), F32),
        ],
        compiler_params=_compiler_params(3),
        name=name,
    )(q, k, vt)


def _sb_attn_kernel(q_ref, k_ref, vt_ref, o_ref, run_scr, acc_scr):
    tq, tk, cw, sub = Q_TILE, K_TILE, CHAIN, SB_SUB
    n_c = tq // cw
    kt_per_q = tq // tk
    qi = pl.program_id(2)
    s_row = lax.broadcasted_iota(jnp.int32, (sub, sub), 0)
    s_col = lax.broadcasted_iota(jnp.int32, (sub, sub), 1)
    neg_later = -((s_col > s_row).astype(BF16))
    lane = _lane_iota(cw)
    heads = SB_STEP_HEADS
    chains = [(hh, c) for hh in range(heads) for c in range(n_c)]

    def pair_lanes(hh):
        return slice((hh // 2) * LANES, (hh // 2 + 1) * LANES)

    qs = []
    for hh, c in chains:
        q_pair = q_ref[0, c * cw:(c + 1) * cw, pair_lanes(hh)]
        in_head = (lane >= (hh % 2) * SB_DIM) & (lane < (hh % 2 + 1) * SB_DIM)
        qs.append(jnp.where(in_head, q_pair, jnp.zeros_like(q_pair)))

    run_scr[...] = jnp.zeros(run_scr.shape, F32)
    acc_scr[...] = jnp.zeros(acc_scr.shape, F32)

    def suffix_sums(z, visible):
        cost = jnp.maximum(z, 0.0) + jnp.log(1.0 + jnp.exp2(-jnp.abs(z))) * LOG2E
        log_beta = z - cost
        if visible is not None:
            cost = jnp.where(visible, cost, 0.0)
        cost16 = cost.astype(BF16)
        blocks = [slice(b * sub, (b + 1) * sub) for b in range(z.shape[0] // sub)]
        suffix = [_dot(neg_later, cost16[bl]) for bl in blocks]
        return log_beta, cost, blocks, suffix

    def accumulate(idx, stage, vt, visible):
        log_beta, cost, blocks, suffix = stage
        run = run_scr[idx]
        terms = []
        for bl, sfx in reversed(list(zip(blocks, suffix))):
            terms.append(log_beta[bl] + sfx + run)
            run = run + sfx[0:1, :] - cost[bl.start:bl.start + 1, :]
        a = jnp.exp2(jnp.concatenate(terms[::-1], axis=0))
        if visible is not None:
            a = jnp.where(visible, a, 0.0)
        run_scr[idx] = run
        acc_scr[idx] = acc_scr[idx] + _dot(vt, a.astype(BF16))

    def tile_update(idxs, fns, vts, masks):
        n = len(idxs)
        pending, stages = {}, {}
        for w in range(min(SB_LOOKAHEAD, len(fns))):
            pending[w] = fns[w]()
        for v in range(n + 1):
            if v + SB_LOOKAHEAD < len(fns):
                pending[v + SB_LOOKAHEAD] = fns[v + SB_LOOKAHEAD]()
            if v < n:
                stages[v] = suffix_sums(pending.pop(v), masks[v])
            if v >= 1:
                accumulate(idxs[v - 1], stages.pop(v - 1), vts[v - 1], masks[v - 1])

    def scores_fn(idx, start, nk):
        return lambda: _dot_nt(
            k_ref[0, pl.ds(pl.multiple_of(start, cw), nk), pair_lanes(chains[idx][0])], qs[idx])

    q0 = qi * tq
    tri_row = lax.broadcasted_iota(jnp.int32, (cw, cw), 0)
    tri_col = lax.broadcasted_iota(jnp.int32, (cw, cw), 1)
    idxs, fns, vts, masks = [], [], [], []
    covered_from = {}
    for phase in range(2):
        for hh in range(heads):
            for c in range(n_c):
                idx = hh * n_c + c
                rel = (c - phase) * cw
                start = jnp.maximum(q0 + rel, 0)
                if phase == 0:
                    visible = tri_row < tri_col
                elif rel < 0:
                    visible = jnp.broadcast_to(qi > 0, (cw, cw))
                else:
                    visible = None
                idxs.append(idx)
                fns.append(scores_fn(idx, start, cw))
                vts.append(vt_ref[0, start // tk, hh * SB_DIM:(hh + 1) * SB_DIM,
                                  rel % tk:rel % tk + cw])
                masks.append(visible)
                covered_from[idx] = q0 + rel
    tile_update(idxs, fns, vts, masks)

    all_idx = list(range(len(chains)))
    key_row = lax.broadcasted_iota(jnp.int32, (tk, cw), 0)

    def alive_at(j):
        flags = [jnp.logical_and(covered_from[idx] > j * tk, jnp.max(run_scr[idx]) > RUN_FLOOR)
                 for idx in all_idx]
        return functools.reduce(jnp.logical_or, flags)

    def step_and_test(carry):
        j, _ = carry
        tile_update(all_idx, [scores_fn(idx, j * tk, tk) for idx in all_idx],
                    [vt_ref[0, j, chains[idx][0] * SB_DIM:(chains[idx][0] + 1) * SB_DIM, :]
                     for idx in all_idx],
                    [key_row + j * tk < covered_from[idx] for idx in all_idx])
        return j - 1, alive_at(j - 1)

    top = qi * kt_per_q + ((n_c - 2) * cw - 1) // tk
    lax.while_loop(lambda carry: jnp.logical_and(carry[0] >= 0, carry[1]),
                   step_and_test, (top, alive_at(top)))
    outs = [jnp.concatenate([acc_scr[hh * n_c + c] for c in range(n_c)], axis=1)
            for hh in range(heads)]
    o_ref[0] = jnp.concatenate(outs, axis=0).astype(o_ref.dtype)


def _sb_attn_call(q, k, vt):
    b, s, w = q.shape
    tq = Q_TILE
    nkt = s // K_TILE
    width = SB_STEP_HEADS * SB_DIM
    n_chains = SB_STEP_HEADS * tq // CHAIN
    return pl.pallas_call(
        _sb_attn_kernel,
        grid=(b, w // width, s // tq),
        in_specs=[
            pl.BlockSpec((1, tq, width), lambda bi, p, i: (bi, i, p)),
            pl.BlockSpec((1, s, width), lambda bi, p, i: (bi, 0, p)),
            pl.BlockSpec((1, nkt, width, K_TILE), lambda bi, p, i: (bi, 0, p, 0)),
        ],
        out_specs=pl.BlockSpec((1, width, tq), lambda bi, p, i: (bi, p, i)),
        out_shape=jax.ShapeDtypeStruct((b, w, s), BF16),
        scratch_shapes=[
            pltpu.VMEM((n_chains, 1, CHAIN), F32),
            pltpu.VMEM((n_chains, SB_DIM, CHAIN), F32),
        ],
        compiler_params=_compiler_params(3),
        name="l0_sb_attn",
    )(q, k, vt)


def _out_ffn_kernel(*refs, n_mix):
    x_ref = refs[0]
    o_refs = refs[1:1 + n_mix]
    wo_refs = refs[1 + n_mix:1 + 2 * n_mix]
    g_ref, wg_ref, wu_ref, wd_ref, y_ref = refs[1 + 2 * n_mix:]
    y = x_ref[...]
    for o_ref, wo_ref in zip(o_refs, wo_refs):
        y = y + lax.dot_general(o_ref[0], wo_ref[...], (((0,), (0,)), ((), ())),
                                preferred_element_type=F32)
    h = _rms(y, g_ref[...], D_MODEL).astype(BF16)
    gate = _dot(h, wg_ref[...])
    up = _dot(h, wu_ref[...])
    act = gate * (1.0 / (1.0 + jnp.exp(-gate))) * up
    y_ref[...] = y + _dot(act.astype(BF16), wd_ref[...])


def _out_ffn_call(x2, mixes, wos, g, wg, wu, wd, name):
    n, d = x2.shape
    tm = FFN_TILE
    n_mix = len(mixes)
    nt = mixes[0].shape[2] // tm
    row = lambda width: pl.BlockSpec((tm, width), lambda i: (i, 0))
    mix_spec = lambda m: pl.BlockSpec((1, m.shape[1], tm), lambda i: (i // nt, 0, i % nt))
    return pl.pallas_call(
        functools.partial(_out_ffn_kernel, n_mix=n_mix),
        grid=(n // tm,),
        in_specs=([row(d)] + [mix_spec(m) for m in mixes]
                  + [_full(w.shape) for w in wos]
                  + [_full((1, d)), _full(wg.shape), _full(wu.shape), _full(wd.shape)]),
        out_specs=row(d),
        out_shape=jax.ShapeDtypeStruct((n, d), F32),
        compiler_params=_compiler_params(1),
        name=name,
    )(x2, *mixes, *wos, g, wg, wu, wd)


def _in1_kernel(x_ref, g_ref, w_ref, wvt_ref, fb_ref, gq_ref, gk_ref, layout_ref,
                qa_ref, ka_ref, vt_ref, carry_ref):
    i = pl.program_id(1)

    @pl.when(i == 0)
    def _():
        carry_ref[...] = jnp.zeros_like(carry_ref)

    n_terms = AUG_TERMS
    used = n_terms * FOX_HEADS
    rows = IN_SUB
    lane = _lane_iota(1)
    half_lane = lane & (FOX_DIM - 1)
    low_half = lane < FOX_DIM
    term = layout_ref[...]
    decay_lane = term < n_terms
    ones_lane = term == n_terms
    own_rows = [jnp.where((half_lane >= n_terms * hd) & (half_lane < n_terms * (hd + 1)), 1.0, 0.0)
                for hd in range(FOX_HEADS)]
    r_i = lax.broadcasted_iota(jnp.int32, (rows, rows), 0)
    c_i = lax.broadcasted_iota(jnp.int32, (rows, rows), 1)
    lower = (c_i <= r_i).astype(BF16)
    g = g_ref[...]
    gq = gq_ref[...]
    gk = gk_ref[...]

    deferred = []
    for r in range(ROW_TILE // IN_SUB):
        sl_rows = slice(r * rows, (r + 1) * rows)
        h = _rms(x_ref[0, sl_rows, :], g, D_MODEL).astype(BF16)
        proj = _dot(h, w_ref[...])
        deferred.append(h)
        q = proj[:, 0:FOX_WIDTH]
        k = proj[:, FOX_WIDTH:2 * FOX_WIDTH]
        f_logit = proj[:, 2 * FOX_WIDTH:]

        log_f = jnp.where(decay_lane, _log_sigmoid(f_logit + fb_ref[...]), 0.0)
        f1, f2, f3 = _split3(log_f)
        decay = carry_ref[...] + _dot(lower, f1) + _dot(lower, f2) + _dot(lower, f3)
        carry_ref[...] = decay[rows - 1:rows, :]
        d1, d2, d3 = (d.astype(F32) for d in _split3(decay * LOG2E))
        terms = jnp.where(term == 0, d1, jnp.where(term == 1, d2, d3))
        aug_q = jnp.where(decay_lane, terms, jnp.where(ones_lane, 1.0, 0.0))
        neg_terms = -terms

        for pair in range(FOX_HEADS // 2):
            sl = slice(pair * LANES, (pair + 1) * LANES)
            qp = q[:, sl]
            kp = k[:, sl]
            qsq = qp * qp
            ksq = kp * kp
            q_gain = qp * gq
            k_gain = kp * gk
            for parity in range(2):
                hd = 2 * pair + parity
                in_head = low_half if parity == 0 else jnp.logical_not(low_half)

                def inv_rms(sq):
                    ss = jnp.sum(jnp.where(in_head, sq, 0.0), axis=-1, keepdims=True)
                    return lax.rsqrt(ss * (1.0 / FOX_DIM) + EPS)

                qa_ref[0, hd, sl_rows, :] = jnp.where(
                    in_head, q_gain * inv_rms(qsq), aug_q).astype(BF16)
                moved = pltpu.roll(neg_terms, used - n_terms * hd, 1)
                aug_k = jnp.where(ones_lane, moved, own_rows[hd])
                ka_ref[0, hd, sl_rows, :] = jnp.where(
                    in_head, k_gain * inv_rms(ksq), aug_k).astype(BF16)

    h_all = jnp.concatenate(deferred, axis=0)
    vt_ref[0, 0] = _dot_nt(wvt_ref[...], h_all).astype(BF16)


def _in1_call(x, g, w, wvt, fb, gq, gk, layout):
    b, s, _ = x.shape
    tm = ROW_TILE
    nt = s // tm
    head_spec = pl.BlockSpec((1, FOX_HEADS, tm, LANES), lambda bi, i: (bi, 0, i, 0))
    lane_vec = _full((1, LANES))
    return pl.pallas_call(
        _in1_kernel,
        grid=(b, nt),
        in_specs=[
            pl.BlockSpec((1, tm, D_MODEL), lambda bi, i: (bi, i, 0)),
            _full((1, D_MODEL)),
            _full(w.shape),
            _full(wvt.shape),
            lane_vec, lane_vec, lane_vec, lane_vec,
        ],
        out_specs=[head_spec, head_spec,
                   pl.BlockSpec((1, 1, FOX_WIDTH, tm), lambda bi, i: (bi, i, 0, 0))],
        out_shape=[
            jax.ShapeDtypeStruct((b, FOX_HEADS, s, LANES), BF16),
            jax.ShapeDtypeStruct((b, FOX_HEADS, s, LANES), BF16),
            jax.ShapeDtypeStruct((b, nt, FOX_WIDTH, tm), BF16),
        ],
        scratch_shapes=[pltpu.VMEM((1, LANES), F32)],
        compiler_params=_compiler_params(2),
        name="l1_in",
    )(x, g, w, wvt, fb, gq, gk, layout)


def _pad_lanes(v, width=LANES):
    return jnp.pad(v, (0, width - v.shape[0])).reshape(1, width)


def _rope_partner(cols):
    half = MLA_ROPE // 2
    return jnp.concatenate([-cols[..., half:], cols[..., :half]], axis=-1)


def _prep_layer0(w_in, w_q_up, w_kv_up, q_norm, k_norm):
    o = MLA_Q_RANK + MLA_KV_RANK
    pad_rope = lambda c: jnp.pad(c, ((0, 0),) * (c.ndim - 1) + ((MLA_NOPE, LANES - MLA_QK),))
    w_rope = w_in[:, o:o + MLA_ROPE]
    sb = w_in[:, o + MLA_ROPE:]
    w = jnp.concatenate([w_in[:, :o], pad_rope(w_rope), pad_rope(_rope_partner(w_rope)),
                         sb[:, :2 * SB_WIDTH]], axis=1)
    wsvt = sb[:, 2 * SB_WIDTH:].T
    wq3 = w_q_up.reshape(MLA_Q_RANK, MLA_HEADS, MLA_QK)
    wq = jnp.pad(wq3, ((0, 0), (0, 0), (0, LANES - MLA_QK)))
    wq_partner = pad_rope(_rope_partner(wq3[:, :, MLA_NOPE:]))
    wq = jnp.concatenate([wq.reshape(MLA_Q_RANK, -1), wq_partner.reshape(MLA_Q_RANK, -1)], axis=1)
    kv = w_kv_up.reshape(MLA_KV_RANK, MLA_HEADS, MLA_NOPE + MLA_V)
    wk = jnp.pad(kv[:, :, :MLA_NOPE], ((0, 0), (0, 0), (0, LANES - MLA_NOPE)))
    wk = wk.reshape(MLA_KV_RANK, MLA_HEADS * LANES)
    wvt = kv[:, :, MLA_NOPE:].reshape(MLA_KV_RANK, MLA_HEADS * MLA_V).T
    gq = q_norm * (MLA_QK ** -0.5 * LOG2E)
    half = MLA_ROPE // 2
    swap_halves = lambda r: jnp.concatenate([r[half:], r[:half]])
    gqp = pad_rope(swap_halves(gq[MLA_NOPE:]))
    gkp = pad_rope(swap_halves(k_norm[MLA_NOPE:]))
    inv_freq = ROPE_THETA ** (-jnp.arange(half, dtype=F32) / half)
    invf = pad_rope(jnp.concatenate([inv_freq, inv_freq]))
    invf = jnp.stack([jnp.roll(invf, MLA_ROPE * gi) for gi in range(TRIG_PACK)])
    lanes = lambda v: v.reshape(1, LANES)
    return (w.astype(BF16), wsvt.astype(BF16), wq.astype(BF16), wk.astype(BF16), wvt.astype(BF16),
            _pad_lanes(gq), lanes(gqp), _pad_lanes(k_norm), lanes(gkp), invf)


def _prep_layer1(w_in, f_bias, q_norm, k_norm):
    wf = w_in[:, 3 * FOX_WIDTH:]
    rep = jnp.repeat(wf, AUG_TERMS, axis=1)
    pad = jnp.zeros((wf.shape[0], FOX_DIM - AUG_TERMS * FOX_HEADS), wf.dtype)
    f_group = jnp.concatenate([rep, pad, rep, pad], axis=1)
    w = jnp.concatenate([w_in[:, :2 * FOX_WIDTH], f_group], axis=1)
    wvt = w_in[:, 2 * FOX_WIDTH:3 * FOX_WIDTH].T
    fb = jnp.repeat(f_bias, AUG_TERMS)
    fpad = jnp.zeros((FOX_DIM - AUG_TERMS * FOX_HEADS,), f_bias.dtype)
    fb = jnp.concatenate([fb, fpad, fb, fpad]).reshape(1, LANES)
    gain2 = lambda gvec: jnp.concatenate([gvec, gvec]).reshape(1, LANES)
    used = AUG_TERMS * FOX_HEADS
    half_lane = jnp.arange(LANES, dtype=jnp.int32) % FOX_DIM
    layout = jnp.where(half_lane < used, half_lane % AUG_TERMS,
                       jnp.where(half_lane < used + AUG_TERMS, AUG_TERMS, AUG_TERMS + 1))
    return (w.astype(BF16), wvt.astype(BF16), fb,
            gain2(q_norm * (FOX_DIM ** -0.5 * LOG2E)), gain2(k_norm), layout.reshape(1, LANES))


def kernel(x, positions, l0_attn_norm, l0_w_in, l0_mla_q_a_norm, l0_mla_w_q_up, l0_mla_kv_a_norm, l0_mla_w_kv_up, l0_mla_q_norm, l0_mla_k_norm, l0_w_o, l0_ffn_norm, l0_w_gate, l0_w_up, l0_w_down, l1_attn_norm, l1_w_in, l1_fox_f_bias, l1_fox_q_norm, l1_fox_k_norm, l1_w_o, l1_ffn_norm, l1_w_gate, l1_w_up, l1_w_down):
    b, s, d = x.shape
    assert d == D_MODEL and s % Q_TILE == 0 and K_TILE == ROW_TILE and Q_TILE % K_TILE == 0
    assert K_TILE % CHAIN == 0 and CHAIN % SB_SUB == 0
    row = lambda v: v.reshape(1, -1)

    w0, wsvt, wq, wk, wvt0, gq0, gqp0, gk0, gkp0, invf = _prep_layer0(
        l0_w_in, l0_mla_w_q_up, l0_mla_w_kv_up, l0_mla_q_norm, l0_mla_k_norm)
    ffn_f32 = [l0_w_gate, l0_w_up, l0_w_down, l1_w_gate, l1_w_up, l1_w_down]
    qm, km, vtm, sq, sk, vts, *ffn_bf16 = _in0_call(
        x, positions.reshape(b, s, 1), row(l0_attn_norm), w0, wsvt, row(l0_mla_q_a_norm), wq,
        row(l0_mla_kv_a_norm), wk, wvt0, gq0, gqp0, gk0, gkp0, invf, ffn_f32)
    wg0, wu0, wd0, wg1, wu1, wd1 = ffn_bf16
    o_mla = _softmax_attn_call(qm, km, vtm, "l0_mla_attn")
    o_sb = _sb_attn_call(sq, sk, vts)
    n_mla = MLA_HEADS * MLA_V
    x1 = _out_ffn_call(
        x.reshape(b * s, d),
        [o_mla, o_sb],
        [l0_w_o[:n_mla].astype(BF16), l0_w_o[n_mla:].astype(BF16)],
        row(l0_ffn_norm), wg0, wu0, wd0, "l0_out_ffn")

    w1, wvt1, fb, gq1, gk1, layout = _prep_layer1(
        l1_w_in, l1_fox_f_bias, l1_fox_q_norm, l1_fox_k_norm)
    qa, ka, vtf = _in1_call(
        x1.reshape(b, s, d), row(l1_attn_norm), w1, wvt1, fb, gq1, gk1, layout)
    o_fox = _softmax_attn_call(qa, ka, vtf, "l1_fox_attn")
    out = _out_ffn_call(
        x1, [o_fox], [l1_w_o.astype(BF16)],
        row(l1_ffn_norm), wg1, wu1, wd1, "l1_out_ffn")
    return out.reshape(b, s, d)
```
